```python
import jax, jax.numpy as jnp
from jax import lax
import numpy as np

D_MODEL = 2048
BATCH = 1
SEQ = 16384
DEPTH = 1
DEC_BATCH = 8
DEC_SEQ = 32
PAST_LEN = 1024

CHUNK = 64
N_ATT_HEADS = 8
HEAD_DIM = 128
ATT_WIDTH = N_ATT_HEADS * HEAD_DIM
N_CONV_GROUPS = 8
CONV_GROUP = 128
CONV_WIDTH = N_CONV_GROUPS * CONV_GROUP
MIX_WIDTH = ATT_WIDTH + CONV_WIDTH
IN_WIDTH = 3 * ATT_WIDTH + 3 * CONV_WIDTH
CONV_K = 3
D_FF = 5632
Q_BLOCK = 128
EPS = 1e-6
SB_SCALE = HEAD_DIM ** -0.5

kernel_name = "hymba_stickbreak_shortconv_convffn_step"


def _rmsnorm(x, g):
    xf = x.astype(jnp.float32)
    y = xf * lax.rsqrt(jnp.mean(xf * xf, axis=-1, keepdims=True) + EPS)
    return (y * g.astype(jnp.float32)).astype(x.dtype)


def _causal_dwconv(u, buf, w):
    T = u.shape[1]
    full = jnp.concatenate([buf.astype(u.dtype), u], axis=1)
    y = full[:, 0:T] * w[0]
    for i in range(1, CONV_K):
        y = y + full[:, i:i + T] * w[i]
    return y, full[:, -(CONV_K - 1):]


def _sb_block(q_blk, k, v, q_pos, k_pos):
    z = jnp.einsum('bqhd,bkhd->bhqk', q_blk, k).astype(jnp.float32) * SB_SCALE
    causal = k_pos[None, :] < q_pos[:, None]
    log_1mb = jnp.where(causal, jax.nn.log_sigmoid(-z), 0.0)
    later = lax.cumsum(log_1mb, axis=3, reverse=True) - log_1mb
    w = jnp.where(causal, jnp.exp(jax.nn.log_sigmoid(z) + later), 0.0)
    return jnp.einsum('bhqk,bkhd->bqhd', w.astype(v.dtype), v)


def _stick_breaking(q, k_all, v_all, past):
    B, T, H, d = q.shape
    k_pos = jnp.arange(k_all.shape[1], dtype=jnp.int32)
    q_pos = past + jnp.arange(T, dtype=jnp.int32)
    if T > Q_BLOCK and T % Q_BLOCK == 0:
        n_blk = T // Q_BLOCK
        qb = q.reshape(B, n_blk, Q_BLOCK, H, d).transpose(1, 0, 2, 3, 4)
        pb = q_pos.reshape(n_blk, Q_BLOCK)
        out = lax.map(lambda a: _sb_block(a[0], k_all, v_all, a[1], k_pos), (qb, pb))
        return out.transpose(1, 0, 2, 3, 4).reshape(B, T, H, d)
    return _sb_block(q, k_all, v_all, q_pos, k_pos)


def _layer(x, cache_k, cache_v, conv_buf, ffn_buf, g_mix, w_in, w_conv, g_att_out,
           g_conv_out, w_o, g_ffn, w_gate_up, w_ffn_conv, w_down):
    B, T, _ = x.shape
    past = cache_k.shape[1]
    h = _rmsnorm(x, g_mix)
    proj = h @ w_in
    splits = [ATT_WIDTH, 2 * ATT_WIDTH, 3 * ATT_WIDTH,
              3 * ATT_WIDTH + CONV_WIDTH, 3 * ATT_WIDTH + 2 * CONV_WIDTH]
    q, k, v, b_gate, c_gate, u = jnp.split(proj, splits, axis=-1)
    q = q.reshape(B, T, N_ATT_HEADS, HEAD_DIM)
    k = k.reshape(B, T, N_ATT_HEADS, HEAD_DIM)
    v = v.reshape(B, T, N_ATT_HEADS, HEAD_DIM)
    k_all = jnp.concatenate([cache_k.astype(k.dtype), k], axis=1)
    v_all = jnp.concatenate([cache_v.astype(v.dtype), v], axis=1)
    att = _stick_breaking(q, k_all, v_all, past)
    att = _rmsnorm(att, g_att_out).reshape(B, T, ATT_WIDTH)
    yc, new_conv = _causal_dwconv(c_gate * u, conv_buf, w_conv)
    yc = (b_gate * yc).reshape(B, T, N_CONV_GROUPS, CONV_GROUP)
    yc = _rmsnorm(yc, g_conv_out).reshape(B, T, CONV_WIDTH)
    x = x + jnp.concatenate([att, yc], axis=-1) @ w_o
    h2 = _rmsnorm(x, g_ffn)
    gate, up = jnp.split(h2 @ w_gate_up, [D_FF], axis=-1)
    gate_c, new_ffn = _causal_dwconv(gate, ffn_buf, w_ffn_conv)
    x = x + (jax.nn.silu(gate_c) * up) @ w_down
    return x, k, v, new_conv, new_ffn


def _trunk(x, cache_k, cache_v, state_conv, state_ffn, g_mix, w_in, w_conv, g_att_out,
           g_conv_out, w_o, g_ffn, w_gate_up, w_ffn_conv, w_down, g_final):
    ks, vs, cs, fs = [], [], [], []
    for l in range(DEPTH):
        x, k, v, c, f = _layer(x, cache_k[l], cache_v[l], state_conv[l], state_ffn[l],
                               g_mix[l], w_in[l], w_conv[l], g_att_out[l], g_conv_out[l],
                               w_o[l], g_ffn[l], w_gate_up[l], w_ffn_conv[l], w_down[l])
        ks.append(k); vs.append(v); cs.append(c); fs.append(f)
    y = _rmsnorm(x, g_final)
    return y, jnp.stack(ks), jnp.stack(vs), jnp.stack(cs), jnp.stack(fs)


def setup_inputs(seed: int = 0) -> dict:
    key = jax.random.key(seed)
    ks = jax.random.split(key, 20)
    f32 = jnp.float32
    nrm = lambda k, s, sc: jax.random.normal(k, s, f32) * sc
    return {
        'x_prompt': nrm(ks[0], (BATCH, SEQ, D_MODEL), 1.0),
        'x_sample': nrm(ks[1], (DEC_BATCH, DEC_SEQ, D_MODEL), 1.0),
        'cache_k': nrm(ks[2], (DEPTH, DEC_BATCH, PAST_LEN, N_ATT_HEADS, HEAD_DIM), 1.0),
        'cache_v': nrm(ks[3], (DEPTH, DEC_BATCH, PAST_LEN, N_ATT_HEADS, HEAD_DIM), 1.0),
        'state_conv': nrm(ks[4], (DEPTH, DEC_BATCH, CONV_K - 1, CONV_WIDTH), 1.0),
        'state_ffn_conv': nrm(ks[5], (DEPTH, DEC_BATCH, CONV_K - 1, D_FF), 1.0),
        'g_mix': 1.0 + nrm(ks[6], (DEPTH, D_MODEL), 0.02),
        'w_in': nrm(ks[7], (DEPTH, D_MODEL, IN_WIDTH), D_MODEL ** -0.5),
        'w_conv': nrm(ks[8], (DEPTH, CONV_K, CONV_WIDTH), CONV_K ** -0.5),
        'g_att_out': 1.0 + nrm(ks[9], (DEPTH, N_ATT_HEADS, HEAD_DIM), 0.02),
        'g_conv_out': 1.0 + nrm(ks[10], (DEPTH, N_CONV_GROUPS, CONV_GROUP), 0.02),
        'w_o': nrm(ks[11], (DEPTH, MIX_WIDTH, D_MODEL), MIX_WIDTH ** -0.5),
        'g_ffn': 1.0 + nrm(ks[12], (DEPTH, D_MODEL), 0.02),
        'w_gate_up': nrm(ks[13], (DEPTH, D_MODEL, 2 * D_FF), D_MODEL ** -0.5),
        'w_ffn_conv': nrm(ks[14], (DEPTH, CONV_K, D_FF), CONV_K ** -0.5),
        'w_down': nrm(ks[15], (DEPTH, D_FF, D_MODEL), D_FF ** -0.5),
        'g_final': 1.0 + nrm(ks[16], (D_MODEL,), 0.02),
    }


def reference(x_prompt, x_sample, cache_k, cache_v, state_conv, state_ffn_conv, g_mix, w_in,
              w_conv, g_att_out, g_conv_out, w_o, g_ffn, w_gate_up, w_ffn_conv, w_down, g_final):
    assert x_sample.shape[1] <= CHUNK
    B, dt = x_prompt.shape[0], x_prompt.dtype
    empty_k = jnp.zeros((DEPTH, B, 0, N_ATT_HEADS, HEAD_DIM), dt)
    zero_conv = jnp.zeros((DEPTH, B, CONV_K - 1, CONV_WIDTH), dt)
    zero_ffn = jnp.zeros((DEPTH, B, CONV_K - 1, D_FF), dt)
    y_prompt, k_p, v_p, c_p, f_p = _trunk(
        x_prompt, empty_k, empty_k, zero_conv, zero_ffn, g_mix, w_in, w_conv, g_att_out,
        g_conv_out, w_o, g_ffn, w_gate_up, w_ffn_conv, w_down, g_final)
    y_sample, k_s, v_s, c_s, f_s = _trunk(
        x_sample, cache_k, cache_v, state_conv, state_ffn_conv, g_mix, w_in, w_conv, g_att_out,
        g_conv_out, w_o, g_ffn, w_gate_up, w_ffn_conv, w_down, g_final)
    return (y_prompt, y_sample, k_p, v_p, c_p, f_p, k_s, v_s, c_s, f_s)
```

```python
import functools
import math

import jax
import jax.numpy as jnp
from jax import lax
from jax.experimental import pallas as pl
from jax.experimental.pallas import tpu as pltpu

F32 = jnp.float32
BF16 = jnp.bfloat16

D_MODEL = 2048
N_HEADS = 8
HEAD_DIM = 128
ATT_WIDTH = N_HEADS * HEAD_DIM
N_GROUPS = 8
GROUP = 128
CONV_WIDTH = N_GROUPS * GROUP
IN_WIDTH = 3 * ATT_WIDTH + 3 * CONV_WIDTH
D_FF = 5632
CONV_K = 3
EPS = 1e-6
Q_SCALE = (HEAD_DIM ** -0.5) * math.log2(math.e)

SUBLANES = 8
LANES = 128
SEG_LEN = 32
KV_CHUNK = SUBLANES * SEG_LEN
VMEM_LIMIT_BYTES = 48 * 1024 * 1024

FF_TILE = 512
IN_PROJ_COL_TILE = 1024


def _params(sem):
    return pltpu.CompilerParams(dimension_semantics=sem, vmem_limit_bytes=VMEM_LIMIT_BYTES)


def _rms_rows(x, g):
    ms = jnp.mean(x * x, axis=-1, keepdims=True)
    return (x * lax.rsqrt(ms + EPS)) * g


def _in_proj_kernel(x_ref, g_ref, w_ref, o_ref, h_ref):
    j = pl.program_id(1)

    @pl.when(j == 0)
    def _():
        h_ref[...] = _rms_rows(x_ref[...], g_ref[...]).astype(BF16)

    acc = jnp.dot(h_ref[...], w_ref[...], preferred_element_type=F32)
    o_ref[...] = acc * jnp.where(j == 0, Q_SCALE, 1.0).astype(F32)


def _in_proj(x2d, g_mix, w_in_bf16, tm):
    m = x2d.shape[0]
    tn = IN_PROJ_COL_TILE
    return pl.pallas_call(
        _in_proj_kernel,
        out_shape=jax.ShapeDtypeStruct((m, IN_WIDTH), F32),
        grid=(m // tm, IN_WIDTH // tn),
        in_specs=[
            pl.BlockSpec((tm, D_MODEL), lambda i, j: (i, 0)),
            pl.BlockSpec((1, D_MODEL), lambda i, j: (0, 0)),
            pl.BlockSpec((D_MODEL, tn), lambda i, j: (0, j)),
        ],
        out_specs=pl.BlockSpec((tm, tn), lambda i, j: (i, j)),
        scratch_shapes=[pltpu.VMEM((tm, D_MODEL), BF16)],
        compiler_params=_params(("arbitrary", "arbitrary")),
        name="in_proj",
    )(x2d, g_mix.reshape(1, D_MODEL), w_in_bf16)


def _neg_abs(z):
    bits = lax.bitcast_convert_type(z, jnp.uint32) | jnp.uint32(0x80000000)
    return lax.bitcast_convert_type(bits, F32)


def _softplus2(z):
    e = jnp.exp2(_neg_abs(z))
    return jnp.maximum(z, 0.0) + jnp.log(1.0 + e) * math.log2(math.e)


def _attn_kernel(qT_ref, k_ref, vT_ref, g_ref, o_ref, z_ref, d_ref, w_ref, acc_ref, carry_ref,
                 *, past, tq, n_chunks_total):
    qi = pl.program_id(1)
    q_lo = past + qi * tq
    shift = KV_CHUNK.bit_length() - 1
    n_chunks = jnp.minimum((q_lo + tq + KV_CHUNK - 1) >> shift, n_chunks_total)
    j_mask = q_lo >> shift

    qT = qT_ref[0]
    acc_ref[...] = jnp.zeros_like(acc_ref)
    carry_ref[...] = jnp.zeros_like(carry_ref)

    def chunk(j, masked):
        base = pl.multiple_of(j * KV_CHUNK, KV_CHUNK)
        kc = k_ref[0, pl.ds(base, KV_CHUNK), :]
        z_ref[...] = jnp.dot(kc, qT, preferred_element_type=F32)

        if masked:
            qpos = q_lo + lax.broadcasted_iota(jnp.int32, (SUBLANES, tq), 1)
            seg_key = base + SEG_LEN * lax.broadcasted_iota(jnp.int32, (SUBLANES, tq), 0)

        run = jnp.zeros((SUBLANES, tq), F32)
        for i in reversed(range(SEG_LEN)):
            rows = slice(i * SUBLANES, (i + 1) * SUBLANES)
            z = z_ref[rows, :]
            sp = _softplus2(z)
            if masked:
                sp = jnp.where(seg_key + i < qpos, sp, 0.0)
            run = run + sp
            d_ref[rows, :] = z - run

        tail = carry_ref[0:1, :]
        offs = [None] * SUBLANES
        for s in reversed(range(SUBLANES)):
            offs[s] = tail
            tail = tail + run[s:s + 1, :]
        carry_ref[...] = jnp.broadcast_to(tail, carry_ref.shape)
        off = jnp.concatenate(offs, axis=0)
        off2 = jnp.concatenate([off, off], axis=0)

        for m in range(KV_CHUNK // (2 * SUBLANES)):
            rows = slice(m * 2 * SUBLANES, (m + 1) * 2 * SUBLANES)
            w = jnp.exp2(d_ref[rows, :] - off2)
            if masked:
                r = lax.broadcasted_iota(jnp.int32, (2 * SUBLANES, tq), 0)
                key = base + SEG_LEN * (r & (SUBLANES - 1)) + (2 * m + (r >> 3))
                qp = q_lo + lax.broadcasted_iota(jnp.int32, (2 * SUBLANES, tq), 1)
                w = jnp.where(key < qp, w, 0.0)
            w_ref[rows, :] = w.astype(BF16)

        vc = vT_ref[0, :, pl.ds(base, KV_CHUNK)]
        acc_ref[...] += jnp.dot(vc, w_ref[...], preferred_element_type=F32)

    def masked_body(t, c):
        chunk(n_chunks - 1 - t, True)
        return c

    def full_body(t, c):
        chunk(j_mask - 1 - t, False)
        return c

    lax.fori_loop(0, n_chunks - j_mask, masked_body, 0)
    lax.fori_loop(0, j_mask, full_body, 0)

    out = acc_ref[...]
    ms = jnp.mean(out * out, axis=0, keepdims=True)
    y = (out * lax.rsqrt(ms + EPS)) * g_ref[0]
    o_ref[0] = y.T


def _attention(qT, k_perm, vT_perm, g_att, *, batch, past, tq):
    bh, _, t_q = qT.shape
    t_k = k_perm.shape[1]
    kernel = functools.partial(_attn_kernel, past=past, tq=tq, n_chunks_total=t_k // KV_CHUNK)
    return pl.pallas_call(
        kernel,
        out_shape=jax.ShapeDtypeStruct((batch, t_q, ATT_WIDTH), F32),
        grid=(bh, t_q // tq),
        in_specs=[
            pl.BlockSpec((1, HEAD_DIM, tq), lambda b, i: (b, 0, i)),
            pl.BlockSpec((1, t_k, HEAD_DIM), lambda b, i: (b, 0, 0)),
            pl.BlockSpec((1, HEAD_DIM, t_k), lambda b, i: (b, 0, 0)),
            pl.BlockSpec((1, HEAD_DIM, 1), lambda b, i: (b % N_HEADS, 0, 0)),
        ],
        out_specs=pl.BlockSpec((1, tq, HEAD_DIM), lambda b, i: (b // N_HEADS, i, b % N_HEADS)),
        scratch_shapes=[
            pltpu.VMEM((KV_CHUNK, tq), F32),
            pltpu.VMEM((KV_CHUNK, tq), F32),
            pltpu.VMEM((KV_CHUNK, tq), BF16),
            pltpu.VMEM((HEAD_DIM, tq), F32),
            pltpu.VMEM((SUBLANES, tq), F32),
        ],
        compiler_params=_params(("arbitrary", "arbitrary")),
        name="sb_attention",
    )(qT, k_perm, vT_perm, g_att.reshape(N_HEADS, HEAD_DIM, 1))


def _permute_keys(a):
    b, t_k, h, d = a.shape
    a = a.reshape(b, t_k // KV_CHUNK, SUBLANES, SEG_LEN, h, d)
    a = a.transpose(0, 4, 1, 3, 2, 5)
    return a.reshape(b * h, t_k, d)


def _attn_layouts(q, k_all, v_all, tq_pad):
    b, t_q, _ = q.shape
    t_k = k_all.shape[1]
    tk_pad = -(-t_k // KV_CHUNK) * KV_CHUNK
    q = jnp.pad(q.astype(BF16), ((0, 0), (0, tq_pad - t_q), (0, 0)))
    qT = q.reshape(b, tq_pad, N_HEADS, HEAD_DIM).transpose(0, 2, 3, 1).reshape(b * N_HEADS, HEAD_DIM, tq_pad)
    pad = ((0, 0), (0, tk_pad - t_k), (0, 0), (0, 0))
    k_perm = _permute_keys(jnp.pad(k_all.astype(BF16), pad))
    vT_perm = _permute_keys(jnp.pad(v_all.astype(BF16), pad)).transpose(0, 2, 1)
    return qT, k_perm, vT_perm


def _dwconv3_rows(cur, w_ref, full_ref):
    n = cur.shape[0]
    full_ref[SUBLANES:SUBLANES + n, :] = cur
    return (full_ref[SUBLANES - 2:SUBLANES - 2 + n, :] * w_ref[0:1, :]
            + full_ref[SUBLANES - 1:SUBLANES - 1 + n, :] * w_ref[1:2, :]
            + cur * w_ref[2:3, :])


def _seq_geometry(tm, seq_len):
    sub_len = min(seq_len, tm)
    return sub_len, tm // sub_len, max(seq_len // tm, 1)


def _mix_kernel(att_ref, b_ref, c_ref, u_ref, cp_ref, up_ref, st_ref, wc_ref, gc_ref, woa_ref, wob_ref,
                x_ref, o_ref, nc_ref, full_ref, yc_ref, *, tm, seq_len):
    i = pl.program_id(0)
    sub_len, n_sub, tiles_per_seq = _seq_geometry(tm, seq_len)
    cu = c_ref[...] * u_ref[...]
    halo_rows = slice(0, SUBLANES)
    for s in range(n_sub):
        rows = slice(s * sub_len, (s + 1) * sub_len)
        if tiles_per_seq == 1:
            full_ref[halo_rows, :] = st_ref[s]
            seq = i * n_sub + s
        else:
            @pl.when(i % tiles_per_seq == 0)
            def _():
                full_ref[halo_rows, :] = st_ref[0]

            @pl.when(i % tiles_per_seq != 0)
            def _():
                full_ref[halo_rows, :] = cp_ref[...] * up_ref[...]

            seq = i // tiles_per_seq
        cur = cu[rows, :]
        y = b_ref[rows, :] * _dwconv3_rows(cur, wc_ref, full_ref)
        for g in range(N_GROUPS):
            cols = slice(g * GROUP, (g + 1) * GROUP)
            yc_ref[rows, cols] = _rms_rows(y[:, cols], gc_ref[:, cols]).astype(BF16)

        def store_state(cur=cur, seq=seq):
            nc_ref[seq] = cur[sub_len - (CONV_K - 1):, :]

        if tiles_per_seq == 1:
            store_state()
        else:
            pl.when(i % tiles_per_seq == tiles_per_seq - 1)(store_state)

    out = jnp.dot(att_ref[...].astype(BF16), woa_ref[...], preferred_element_type=F32)
    out = out + jnp.dot(yc_ref[...], wob_ref[...], preferred_element_type=F32)
    o_ref[...] = x_ref[...] + out


def _state_rows(state):
    return jnp.pad(state, ((0, 0), (SUBLANES - (CONV_K - 1), 0), (0, 0)))


def _mix(att2d, proj, state, w_conv, g_conv, w_o_bf16, x2d, *, tm, seq_len):
    m = x2d.shape[0]
    n_seq = m // seq_len
    sub_len, n_sub, tiles_per_seq = _seq_geometry(tm, seq_len)
    rows_per_block = tm // SUBLANES
    col0 = 3 * ATT_WIDTH // CONV_WIDTH
    prev_map = lambda c: (lambda i: (jnp.maximum(i * rows_per_block - 1, 0), c))
    kernel = functools.partial(_mix_kernel, tm=tm, seq_len=seq_len)
    return pl.pallas_call(
        kernel,
        out_shape=(jax.ShapeDtypeStruct((m, D_MODEL), F32),
                   jax.ShapeDtypeStruct((n_seq, CONV_K - 1, CONV_WIDTH), F32)),
        grid=(m // tm,),
        in_specs=[
            pl.BlockSpec((tm, ATT_WIDTH), lambda i: (i, 0)),
            pl.BlockSpec((tm, CONV_WIDTH), lambda i: (i, col0)),
            pl.BlockSpec((tm, CONV_WIDTH), lambda i: (i, col0 + 1)),
            pl.BlockSpec((tm, CONV_WIDTH), lambda i: (i, col0 + 2)),
            pl.BlockSpec((SUBLANES, CONV_WIDTH), prev_map(col0 + 1)),
            pl.BlockSpec((SUBLANES, CONV_WIDTH), prev_map(col0 + 2)),
            pl.BlockSpec((n_sub, SUBLANES, CONV_WIDTH), lambda i: (i // tiles_per_seq, 0, 0)),
            pl.BlockSpec((CONV_K, CONV_WIDTH), lambda i: (0, 0)),
            pl.BlockSpec((1, CONV_WIDTH), lambda i: (0, 0)),
            pl.BlockSpec((ATT_WIDTH, D_MODEL), lambda i: (0, 0)),
            pl.BlockSpec((CONV_WIDTH, D_MODEL), lambda i: (1, 0)),
            pl.BlockSpec((tm, D_MODEL), lambda i: (i, 0)),
        ],
        out_specs=(pl.BlockSpec((tm, D_MODEL), lambda i: (i, 0)),
                   pl.BlockSpec((n_seq, CONV_K - 1, CONV_WIDTH), lambda i: (0, 0, 0))),
        scratch_shapes=[pltpu.VMEM((SUBLANES + sub_len, CONV_WIDTH), F32),
                        pltpu.VMEM((tm, CONV_WIDTH), BF16)],
        compiler_params=_params(("arbitrary",)),
        name="mix_out_proj",
    )(att2d, proj, proj, proj, proj, proj, _state_rows(state), w_conv, g_conv.reshape(1, CONV_WIDTH),
      w_o_bf16, w_o_bf16, x2d)


def _ffn_kernel(x_ref, g_ref, wg_ref, wu_ref, wc_ref, st_ref, wd_ref, gf_ref, o_ref, nf_ref,
                h_ref, halo_ref, full_ref, act_ref, *, tm, seq_len, n_ff):
    i = pl.program_id(0)
    j = pl.program_id(1)
    sub_len, n_sub, tiles_per_seq = _seq_geometry(tm, seq_len)
    tf = wg_ref.shape[1]

    @pl.when(j == 0)
    def _():
        x = x_ref[...]
        h_ref[...] = _rms_rows(x, g_ref[...]).astype(BF16)
        o_ref[...] = x

    h = h_ref[...]
    gate = jnp.dot(h, wg_ref[...], preferred_element_type=F32)
    up = jnp.dot(h, wu_ref[...], preferred_element_type=F32)
    halo_rows = slice(0, SUBLANES)
    for s in range(n_sub):
        rows = slice(s * sub_len, (s + 1) * sub_len)
        if tiles_per_seq == 1:
            full_ref[halo_rows, :] = st_ref[s]
            seq = i * n_sub + s
        else:
            @pl.when(i % tiles_per_seq == 0)
            def _():
                full_ref[halo_rows, :] = st_ref[0]

            @pl.when(i % tiles_per_seq != 0)
            def _():
                full_ref[halo_rows, :] = halo_ref[j]

            seq = i // tiles_per_seq
        cur = gate[rows, :]
        gc = _dwconv3_rows(cur, wc_ref, full_ref)
        act_ref[rows, :] = (gc * jax.nn.sigmoid(gc) * up[rows, :]).astype(BF16)

        def store_state(cur=cur, seq=seq):
            nf_ref[seq, j] = cur[sub_len - (CONV_K - 1):, :]

        if tiles_per_seq == 1:
            store_state()
        else:
            halo_ref[j] = cur[sub_len - SUBLANES:, :]
            pl.when(i % tiles_per_seq == tiles_per_seq - 1)(store_state)

    o_ref[...] += jnp.dot(act_ref[...], wd_ref[...], preferred_element_type=F32)

    @pl.when(j == n_ff - 1)
    def _():
        o_ref[...] = _rms_rows(o_ref[...], gf_ref[...])


def _ffn(x2d, g_ffn, w_gate_up_bf16, w_ffn_conv, state, w_down_bf16, g_final, *, tm, seq_len):
    m = x2d.shape[0]
    n_seq = m // seq_len
    tf = FF_TILE
    n_ff = D_FF // tf
    sub_len, n_sub, tiles_per_seq = _seq_geometry(tm, seq_len)
    kernel = functools.partial(_ffn_kernel, tm=tm, seq_len=seq_len, n_ff=n_ff)
    y, new_ffn = pl.pallas_call(
        kernel,
        out_shape=(jax.ShapeDtypeStruct((m, D_MODEL), F32),
                   jax.ShapeDtypeStruct((n_seq, n_ff, CONV_K - 1, tf), F32)),
        grid=(m // tm, n_ff),
        in_specs=[
            pl.BlockSpec((tm, D_MODEL), lambda i, j: (i, 0)),
            pl.BlockSpec((1, D_MODEL), lambda i, j: (0, 0)),
            pl.BlockSpec((D_MODEL, tf), lambda i, j: (0, j)),
            pl.BlockSpec((D_MODEL, tf), lambda i, j: (0, j + n_ff)),
            pl.BlockSpec((CONV_K, tf), lambda i, j: (0, j)),
            pl.BlockSpec((n_sub, SUBLANES, tf), lambda i, j: (i // tiles_per_seq, 0, j)),
            pl.BlockSpec((tf, D_MODEL), lambda i, j: (j, 0)),
            pl.BlockSpec((1, D_MODEL), lambda i, j: (0, 0)),
        ],
        out_specs=(pl.BlockSpec((tm, D_MODEL), lambda i, j: (i, 0)),
                   pl.BlockSpec((n_seq, n_ff, CONV_K - 1, tf), lambda i, j: (0, 0, 0, 0))),
        scratch_shapes=[pltpu.VMEM((tm, D_MODEL), BF16),
                        pltpu.VMEM((n_ff, SUBLANES, tf), F32),
                        pltpu.VMEM((SUBLANES + sub_len, tf), F32),
                        pltpu.VMEM((tm, tf), BF16)],
        compiler_params=_params(("arbitrary", "arbitrary")),
        name="conv_ffn",
    )(x2d, g_ffn.reshape(1, D_MODEL), w_gate_up_bf16, w_gate_up_bf16, w_ffn_conv, _state_rows(state),
      w_down_bf16, g_final.reshape(1, D_MODEL))
    return y, new_ffn.transpose(0, 2, 1, 3).reshape(n_seq, CONV_K - 1, D_FF)


def _trunk(x, cache_k, cache_v, state_conv, state_ffn, weights, *, tm_proj, tm_mix, tm_ffn, tq, tq_pad):
    g_mix, w_in, w_conv, g_att, g_conv, w_o, g_ffn, w_gate_up, w_ffn_conv, w_down, g_final = weights
    b, t, _ = x.shape
    past = cache_k.shape[1]
    x2d = x.reshape(b * t, D_MODEL)

    proj = _in_proj(x2d, g_mix, w_in, tm_proj)
    proj3 = proj.reshape(b, t, IN_WIDTH)
    k = proj3[:, :, ATT_WIDTH:2 * ATT_WIDTH].reshape(b, t, N_HEADS, HEAD_DIM)
    v = proj3[:, :, 2 * ATT_WIDTH:3 * ATT_WIDTH].reshape(b, t, N_HEADS, HEAD_DIM)
    k_all = jnp.concatenate([cache_k, k], axis=1)
    v_all = jnp.concatenate([cache_v, v], axis=1)
    qT, k_perm, vT_perm = _attn_layouts(proj3[:, :, :ATT_WIDTH], k_all, v_all, tq_pad)
    att = _attention(qT, k_perm, vT_perm, g_att, batch=b, past=past, tq=tq)
    att2d = att[:, :t, :].reshape(b * t, ATT_WIDTH)

    x1, new_conv = _mix(att2d, proj, state_conv, w_conv, g_conv, w_o, x2d, tm=tm_mix, seq_len=t)
    y, new_ffn = _ffn(x1, g_ffn, w_gate_up, w_ffn_conv, state_ffn, w_down, g_final, tm=tm_ffn, seq_len=t)
    return y.reshape(b, t, D_MODEL), k, v, new_conv, new_ffn


def kernel(x_prompt, x_sample, cache_k, cache_v, state_conv, state_ffn_conv, g_mix, w_in, w_conv, g_att_out,
           g_conv_out, w_o, g_ffn, w_gate_up, w_ffn_conv, w_down, g_final):
    depth = g_mix.shape[0]
    assert depth == 1
    weights = (g_mix[0], w_in[0].astype(BF16), w_conv[0], g_att_out[0], g_conv_out[0], w_o[0].astype(BF16),
               g_ffn[0], w_gate_up[0].astype(BF16), w_ffn_conv[0], w_down[0].astype(BF16), g_final)
    bp = x_prompt.shape[0]
    dt = x_prompt.dtype
    empty = jnp.zeros((bp, 0, N_HEADS, HEAD_DIM), dt)
    y_p, k_p, v_p, c_p, f_p = _trunk(
        x_prompt, empty, empty, jnp.zeros((bp, CONV_K - 1, CONV_WIDTH), dt), jnp.zeros((bp, CONV_K - 1, D_FF), dt),
        weights, tm_proj=1024, tm_mix=256, tm_ffn=512, tq=256, tq_pad=x_prompt.shape[1])
    bs, ts, _ = x_sample.shape
    y_s, k_s, v_s, c_s, f_s = _trunk(
        x_sample, cache_k[0], cache_v[0], state_conv[0], state_ffn_conv[0],
        weights, tm_proj=bs * ts, tm_mix=bs * ts, tm_ffn=bs * ts, tq=LANES, tq_pad=LANES)
    return (y_p, y_s, k_p[None], v_p[None], c_p[None], f_p[None], k_s[None], v_s[None], c_s[None], f_s[None])
```

```python
import functools
import math

import jax
import jax.numpy as jnp
from jax import lax
from jax.experimental import pallas as pl
from jax.experimental.pallas import tpu as pltpu

F32 = jnp.float32
BF16 = jnp.bfloat16

D_MODEL = 2048
N_HEADS = 8
HEAD_DIM = 128
ATT_WIDTH = N_HEADS * HEAD_DIM
N_GROUPS = 8
GROUP = 128
CONV_WIDTH = N_GROUPS * GROUP
IN_WIDTH = 3 * ATT_WIDTH + 3 * CONV_WIDTH
D_FF = 5632
CONV_K = 3
EPS = 1e-6
Q_SCALE = (HEAD_DIM ** -0.5) * math.log2(math.e)

SUBLANES = 8
LANES = 128
SEGS = 2 * SUBLANES
SEG_LEN = 32
KV_CHUNK = SEGS * SEG_LEN
VMEM_LIMIT_BYTES = 48 * 1024 * 1024

FF_TILE = 512
IN_PROJ_COL_TILE = 1024


def _params(sem):
    return pltpu.CompilerParams(dimension_semantics=sem, vmem_limit_bytes=VMEM_LIMIT_BYTES)


def _rms_rows(x, g):
    ms = jnp.mean(x * x, axis=-1, keepdims=True)
    return (x * lax.rsqrt(ms + EPS)) * g


def _in_proj_kernel(x_ref, g_ref, w_ref, o_ref, h_ref):
    j = pl.program_id(1)

    @pl.when(j == 0)
    def _():
        h_ref[...] = _rms_rows(x_ref[...], g_ref[...]).astype(BF16)

    acc = jnp.dot(h_ref[...], w_ref[...], preferred_element_type=F32)
    o_ref[...] = acc * jnp.where(j == 0, Q_SCALE, 1.0).astype(F32)


def _in_proj(x2d, g_mix, w_in_bf16, tm):
    m = x2d.shape[0]
    tn = IN_PROJ_COL_TILE
    return pl.pallas_call(
        _in_proj_kernel,
        out_shape=jax.ShapeDtypeStruct((m, IN_WIDTH), F32),
        grid=(m // tm, IN_WIDTH // tn),
        in_specs=[
            pl.BlockSpec((tm, D_MODEL), lambda i, j: (i, 0)),
            pl.BlockSpec((1, D_MODEL), lambda i, j: (0, 0)),
            pl.BlockSpec((D_MODEL, tn), lambda i, j: (0, j)),
        ],
        out_specs=pl.BlockSpec((tm, tn), lambda i, j: (i, j)),
        scratch_shapes=[pltpu.VMEM((tm, D_MODEL), BF16)],
        compiler_params=_params(("arbitrary", "arbitrary")),
        name="in_proj",
    )(x2d, g_mix.reshape(1, D_MODEL), w_in_bf16)


def _softplus2(z):
    e = jnp.exp2(-jnp.abs(z))
    return jnp.maximum(z, 0.0) + jnp.log(1.0 + e) * math.log2(math.e)


def _attn_kernel(qT_ref, k_ref, vT_ref, g_ref, o_ref, d_ref, w_ref, off_ref, acc_ref, carry_ref,
                 *, past, tq):
    qi = pl.program_id(1)
    q_lo = past + qi * tq
    j_top = q_lo >> (KV_CHUNK.bit_length() - 1)

    qT = qT_ref[0]
    qpos = q_lo + lax.broadcasted_iota(jnp.int32, (SEGS, tq), 1)
    top_key = j_top * KV_CHUNK + SEG_LEN * lax.broadcasted_iota(jnp.int32, (SEGS, tq), 0) + (SEG_LEN - 1)

    def causal(i):
        return top_key - i < qpos

    def scores(j):
        base = pl.multiple_of(j * KV_CHUNK, KV_CHUNK)
        kc = k_ref[0, pl.ds(base, KV_CHUNK), :]
        return jnp.dot(kc, qT, preferred_element_type=F32)

    def pass1(z_all, masked):
        run = jnp.zeros((SEGS, tq), F32)
        for i in range(SEG_LEN):
            rows = slice(i * SEGS, (i + 1) * SEGS)
            z = z_all[rows, :]
            sp = _softplus2(z)
            if masked:
                sp = jnp.where(causal(i), sp, 0.0)
            run = run + sp
            d_ref[rows, :] = z - run
        tail = carry_ref[0:1, :]
        offs = [None] * SEGS
        for s in reversed(range(SEGS)):
            offs[s] = tail
            tail = tail + run[s:s + 1, :]
        carry_ref[...] = jnp.broadcast_to(tail, carry_ref.shape)
        off_ref[...] = jnp.concatenate(offs, axis=0)

    def pass2(masked):
        off = off_ref[...]
        for i in range(SEG_LEN):
            rows = slice(i * SEGS, (i + 1) * SEGS)
            w = jnp.exp2(d_ref[rows, :] - off)
            if masked:
                w = jnp.where(causal(i), w, 0.0)
            w_ref[rows, :] = w.astype(BF16)

    def add_values(j):
        base = pl.multiple_of(j * KV_CHUNK, KV_CHUNK)
        vc = vT_ref[0, :, pl.ds(base, KV_CHUNK)]
        acc_ref[...] += jnp.dot(vc, w_ref[...], preferred_element_type=F32)

    def step(j, prev_masked):
        z_all = scores(j)
        pass2(prev_masked)
        add_values(j + 1)
        pass1(z_all, False)

    acc_ref[...] = jnp.zeros_like(acc_ref)
    carry_ref[...] = jnp.zeros_like(carry_ref)
    pass1(scores(j_top), True)

    @pl.when(j_top > 0)
    def _():
        step(j_top - 1, True)

    def body(t, c):
        step(j_top - 2 - t, False)
        return c

    lax.fori_loop(0, jnp.maximum(j_top - 1, 0), body, 0)

    @pl.when(j_top == 0)
    def _():
        pass2(True)

    @pl.when(j_top > 0)
    def _():
        pass2(False)

    add_values(0)

    out = acc_ref[...]
    ms = jnp.mean(out * out, axis=0, keepdims=True)
    y = (out * lax.rsqrt(ms + EPS)) * g_ref[0]
    o_ref[0] = y.T


def _attention(qT, k_perm, vT_perm, g_att, *, batch, past, tq):
    bh, _, t_q = qT.shape
    t_k = k_perm.shape[1]
    assert KV_CHUNK % tq == 0 and past % KV_CHUNK == 0 and t_q % tq == 0
    assert t_k % KV_CHUNK == 0 and t_k >= -(-(past + t_q) // KV_CHUNK) * KV_CHUNK
    kernel = functools.partial(_attn_kernel, past=past, tq=tq)
    return pl.pallas_call(
        kernel,
        out_shape=jax.ShapeDtypeStruct((batch, t_q, ATT_WIDTH), F32),
        grid=(bh, t_q // tq),
        in_specs=[
            pl.BlockSpec((1, HEAD_DIM, tq), lambda b, i: (b, 0, i)),
            pl.BlockSpec((1, t_k, HEAD_DIM), lambda b, i: (b, 0, 0)),
            pl.BlockSpec((1, HEAD_DIM, t_k), lambda b, i: (b, 0, 0)),
            pl.BlockSpec((1, HEAD_DIM, 1), lambda b, i: (b % N_HEADS, 0, 0)),
        ],
        out_specs=pl.BlockSpec((1, tq, HEAD_DIM), lambda b, i: (b // N_HEADS, i, b % N_HEADS)),
        scratch_shapes=[
            pltpu.VMEM((KV_CHUNK, tq), F32),
            pltpu.VMEM((KV_CHUNK, tq), BF16),
            pltpu.VMEM((SEGS, tq), F32),
            pltpu.VMEM((HEAD_DIM, tq), F32),
            pltpu.VMEM((SUBLANES, tq), F32),
        ],
        compiler_params=_params(("arbitrary", "arbitrary")),
        name="sb_attention",
    )(qT, k_perm, vT_perm, g_att.reshape(N_HEADS, HEAD_DIM, 1))


def _permute_keys(a):
    b, t_k, h, d = a.shape
    a = a.reshape(b, t_k // KV_CHUNK, SEGS, SEG_LEN, h, d)[:, :, :, ::-1]
    a = a.transpose(0, 4, 1, 3, 2, 5)
    return a.reshape(b * h, t_k, d)


def _attn_layouts(q, k_all, v_all, tq_pad):
    b, t_q, _ = q.shape
    t_k = k_all.shape[1]
    tk_pad = -(-t_k // KV_CHUNK) * KV_CHUNK
    q = jnp.pad(q.astype(BF16), ((0, 0), (0, tq_pad - t_q), (0, 0)))
    qT = q.reshape(b, tq_pad, N_HEADS, HEAD_DIM).transpose(0, 2, 3, 1).reshape(b * N_HEADS, HEAD_DIM, tq_pad)
    pad = ((0, 0), (0, tk_pad - t_k), (0, 0), (0, 0))
    k_perm = _permute_keys(jnp.pad(k_all.astype(BF16), pad))
    vT_perm = _permute_keys(jnp.pad(v_all.astype(BF16), pad)).transpose(0, 2, 1)
    return qT, k_perm, vT_perm


def _dwconv3_rows(cur, w_ref, full_ref):
    n = cur.shape[0]
    full_ref[SUBLANES:SUBLANES + n, :] = cur
    return (full_ref[SUBLANES - 2:SUBLANES - 2 + n, :] * w_ref[0:1, :]
            + full_ref[SUBLANES - 1:SUBLANES - 1 + n, :] * w_ref[1:2, :]
            + cur * w_ref[2:3, :])


def _seq_geometry(tm, seq_len):
    sub_len = min(seq_len, tm)
    return sub_len, tm // sub_len, max(seq_len // tm, 1)


def _mix_kernel(att_ref, b_ref, c_ref, u_ref, cp_ref, up_ref, st_ref, wc_ref, gc_ref, woa_ref, wob_ref,
                x_ref, o_ref, nc_ref, full_ref, yc_ref, *, tm, seq_len):
    i = pl.program_id(0)
    sub_len, n_sub, tiles_per_seq = _seq_geometry(tm, seq_len)
    cu = c_ref[...] * u_ref[...]
    halo_rows = slice(0, SUBLANES)
    for s in range(n_sub):
        rows = slice(s * sub_len, (s + 1) * sub_len)
        if tiles_per_seq == 1:
            full_ref[halo_rows, :] = st_ref[s]
            seq = i * n_sub + s
        else:
            @pl.when(i % tiles_per_seq == 0)
            def _():
                full_ref[halo_rows, :] = st_ref[0]

            @pl.when(i % tiles_per_seq != 0)
            def _():
                full_ref[halo_rows, :] = cp_ref[...] * up_ref[...]

            seq = i // tiles_per_seq
        cur = cu[rows, :]
        y = b_ref[rows, :] * _dwconv3_rows(cur, wc_ref, full_ref)
        for g in range(N_GROUPS):
            cols = slice(g * GROUP, (g + 1) * GROUP)
            yc_ref[rows, cols] = _rms_rows(y[:, cols], gc_ref[:, cols]).astype(BF16)

        def store_state(cur=cur, seq=seq):
            nc_ref[seq] = cur[sub_len - (CONV_K - 1):, :]

        if tiles_per_seq == 1:
            store_state()
        else:
            pl.when(i % tiles_per_seq == tiles_per_seq - 1)(store_state)

    out = jnp.dot(att_ref[...].astype(BF16), woa_ref[...], preferred_element_type=F32)
    out = out + jnp.dot(yc_ref[...], wob_ref[...], preferred_element_type=F32)
    o_ref[...] = x_ref[...] + out


def _state_rows(state):
    return jnp.pad(state, ((0, 0), (SUBLANES - (CONV_K - 1), 0), (0, 0)))


def _mix(att2d, proj, state, w_conv, g_conv, w_o_bf16, x2d, *, tm, seq_len):
    m = x2d.shape[0]
    n_seq = m // seq_len
    sub_len, n_sub, tiles_per_seq = _seq_geometry(tm, seq_len)
    rows_per_block = tm // SUBLANES
    col0 = 3 * ATT_WIDTH // CONV_WIDTH
    prev_map = lambda c: (lambda i: (jnp.maximum(i * rows_per_block - 1, 0), c))
    kernel = functools.partial(_mix_kernel, tm=tm, seq_len=seq_len)
    return pl.pallas_call(
        kernel,
        out_shape=(jax.ShapeDtypeStruct((m, D_MODEL), F32),
                   jax.ShapeDtypeStruct((n_seq, CONV_K - 1, CONV_WIDTH), F32)),
        grid=(m // tm,),
        in_specs=[
            pl.BlockSpec((tm, ATT_WIDTH), lambda i: (i, 0)),
            pl.BlockSpec((tm, CONV_WIDTH), lambda i: (i, col0)),
            pl.BlockSpec((tm, CONV_WIDTH), lambda i: (i, col0 + 1)),
            pl.BlockSpec((tm, CONV_WIDTH), lambda i: (i, col0 + 2)),
            pl.BlockSpec((SUBLANES, CONV_WIDTH), prev_map(col0 + 1)),
            pl.BlockSpec((SUBLANES, CONV_WIDTH), prev_map(col0 + 2)),
            pl.BlockSpec((n_sub, SUBLANES, CONV_WIDTH), lambda i: (i // tiles_per_seq, 0, 0)),
            pl.BlockSpec((CONV_K, CONV_WIDTH), lambda i: (0, 0)),
            pl.BlockSpec((1, CONV_WIDTH), lambda i: (0, 0)),
            pl.BlockSpec((ATT_WIDTH, D_MODEL), lambda i: (0, 0)),
            pl.BlockSpec((CONV_WIDTH, D_MODEL), lambda i: (1, 0)),
            pl.BlockSpec((tm, D_MODEL), lambda i: (i, 0)),
        ],
        out_specs=(pl.BlockSpec((tm, D_MODEL), lambda i: (i, 0)),
                   pl.BlockSpec((n_seq, CONV_K - 1, CONV_WIDTH), lambda i: (0, 0, 0))),
        scratch_shapes=[pltpu.VMEM((SUBLANES + sub_len, CONV_WIDTH), F32),
                        pltpu.VMEM((tm, CONV_WIDTH), BF16)],
        compiler_params=_params(("arbitrary",)),
        name="mix_out_proj",
    )(att2d, proj, proj, proj, proj, proj, _state_rows(state), w_conv, g_conv.reshape(1, CONV_WIDTH),
      w_o_bf16, w_o_bf16, x2d)


def _ffn_kernel(x_ref, g_ref, wg_ref, wu_ref, wc_ref, st_ref, wd_ref, gf_ref, o_ref, nf_ref,
                h_ref, halo_ref, full_ref, act_ref, *, tm, seq_len, n_ff):
    i = pl.program_id(0)
    j = pl.program_id(1)
    sub_len, n_sub, tiles_per_seq = _seq_geometry(tm, seq_len)
    tf = wg_ref.shape[1]

    @pl.when(j == 0)
    def _():
        x = x_ref[...]
        h_ref[...] = _rms_rows(x, g_ref[...]).astype(BF16)
        o_ref[...] = x

    h = h_ref[...]
    gate = jnp.dot(h, wg_ref[...], preferred_element_type=F32)
    up = jnp.dot(h, wu_ref[...], preferred_element_type=F32)
    halo_rows = slice(0, SUBLANES)
    for s in range(n_sub):
        rows = slice(s * sub_len, (s + 1) * sub_len)
        if tiles_per_seq == 1:
            full_ref[halo_rows, :] = st_ref[s]
            seq = i * n_sub + s
        else:
            @pl.when(i % tiles_per_seq == 0)
            def _():
                full_ref[halo_rows, :] = st_ref[0]

            @pl.when(i % tiles_per_seq != 0)
            def _():
                full_ref[halo_rows, :] = halo_ref[j]

            seq = i // tiles_per_seq
        cur = gate[rows, :]
        gc = _dwconv3_rows(cur, wc_ref, full_ref)
        act_ref[rows, :] = (gc * jax.nn.sigmoid(gc) * up[rows, :]).astype(BF16)

        def store_state(cur=cur, seq=seq):
            nf_ref[seq, j] = cur[sub_len - (CONV_K - 1):, :]

        if tiles_per_seq == 1:
            store_state()
        else:
            halo_ref[j] = cur[sub_len - SUBLANES:, :]
            pl.when(i % tiles_per_seq == tiles_per_seq - 1)(store_state)

    o_ref[...] += jnp.dot(act_ref[...], wd_ref[...], preferred_element_type=F32)

    @pl.when(j == n_ff - 1)
    def _():
        o_ref[...] = _rms_rows(o_ref[...], gf_ref[...])


def _ffn(x2d, g_ffn, w_gate_up_bf16, w_ffn_conv, state, w_down_bf16, g_final, *, tm, seq_len):
    m = x2d.shape[0]
    n_seq = m // seq_len
    tf = FF_TILE
    n_ff = D_FF // tf
    sub_len, n_sub, tiles_per_seq = _seq_geometry(tm, seq_len)
    kernel = functools.partial(_ffn_kernel, tm=tm, seq_len=seq_len, n_ff=n_ff)
    y, new_ffn = pl.pallas_call(
        kernel,
        out_shape=(jax.ShapeDtypeStruct((m, D_MODEL), F32),
                   jax.ShapeDtypeStruct((n_seq, n_ff, CONV_K - 1, tf), F32)),
        grid=(m // tm, n_ff),
        in_specs=[
            pl.BlockSpec((tm, D_MODEL), lambda i, j: (i, 0)),
            pl.BlockSpec((1, D_MODEL), lambda i, j: (0, 0)),
            pl.BlockSpec((D_MODEL, tf), lambda i, j: (0, j)),
            pl.BlockSpec((D_MODEL, tf), lambda i, j: (0, j + n_ff)),
            pl.BlockSpec((CONV_K, tf), lambda i, j: (0, j)),
            pl.BlockSpec((n_sub, SUBLANES, tf), lambda i, j: (i // tiles_per_seq, 0, j)),
            pl.BlockSpec((tf, D_MODEL), lambda i, j: (j, 0)),
            pl.BlockSpec((1, D_MODEL), lambda i, j: (0, 0)),
        ],
        out_specs=(pl.BlockSpec((tm, D_MODEL), lambda i, j: (i, 0)),
                   pl.BlockSpec((n_seq, n_ff, CONV_K - 1, tf), lambda i, j: (0, 0, 0, 0))),
        scratch_shapes=[pltpu.VMEM((tm, D_MODEL), BF16),
                        pltpu.VMEM((n_ff, SUBLANES, tf), F32),
                        pltpu.VMEM((SUBLANES + sub_len, tf), F32),
                        pltpu.VMEM((tm, tf), BF16)],
        compiler_params=_params(("arbitrary", "arbitrary")),
        name="conv_ffn",
    )(x2d, g_ffn.reshape(1, D_MODEL), w_gate_up_bf16, w_gate_up_bf16, w_ffn_conv, _state_rows(state),
      w_down_bf16, g_final.reshape(1, D_MODEL))
    return y, new_ffn.transpose(0, 2, 1, 3).reshape(n_seq, CONV_K - 1, D_FF)


def _trunk(x, cache_k, cache_v, state_conv, state_ffn, weights, *, tm_proj, tm_mix, tm_ffn, tq, tq_pad):
    g_mix, w_in, w_conv, g_att, g_conv, w_o, g_ffn, w_gate_up, w_ffn_conv, w_down, g_final = weights
    b, t, _ = x.shape
    past = cache_k.shape[1]
    x2d = x.reshape(b * t, D_MODEL)

    proj = _in_proj(x2d, g_mix, w_in, tm_proj)
    proj3 = proj.reshape(b, t, IN_WIDTH)
    k = proj3[:, :, ATT_WIDTH:2 * ATT_WIDTH].reshape(b, t, N_HEADS, HEAD_DIM)
    v = proj3[:, :, 2 * ATT_WIDTH:3 * ATT_WIDTH].reshape(b, t, N_HEADS, HEAD_DIM)
    k_all = jnp.concatenate([cache_k, k], axis=1)
    v_all = jnp.concatenate([cache_v, v], axis=1)
    qT, k_perm, vT_perm = _attn_layouts(proj3[:, :, :ATT_WIDTH], k_all, v_all, tq_pad)
    att = _attention(qT, k_perm, vT_perm, g_att, batch=b, past=past, tq=tq)
    att2d = att[:, :t, :].reshape(b * t, ATT_WIDTH)

    x1, new_conv = _mix(att2d, proj, state_conv, w_conv, g_conv, w_o, x2d, tm=tm_mix, seq_len=t)
    y, new_ffn = _ffn(x1, g_ffn, w_gate_up, w_ffn_conv, state_ffn, w_down, g_final, tm=tm_ffn, seq_len=t)
    return y.reshape(b, t, D_MODEL), k, v, new_conv, new_ffn


def kernel(x_prompt, x_sample, cache_k, cache_v, state_conv, state_ffn_conv, g_mix, w_in, w_conv, g_att_out,
           g_conv_out, w_o, g_ffn, w_gate_up, w_ffn_conv, w_down, g_final):
    depth = g_mix.shape[0]
    assert depth == 1
    weights = (g_mix[0], w_in[0].astype(BF16), w_conv[0], g_att_out[0], g_conv_out[0], w_o[0].astype(BF16),
               g_ffn[0], w_gate_up[0].astype(BF16), w_ffn_conv[0], w_down[0].astype(BF16), g_final)
    bp = x_prompt.shape[0]
    dt = x_prompt.dtype
    empty = jnp.zeros((bp, 0, N_HEADS, HEAD_DIM), dt)
    y_p, k_p, v_p, c_p, f_p = _trunk(
        x_prompt, empty, empty, jnp.zeros((bp, CONV_K - 1, CONV_WIDTH), dt), jnp.zeros((bp, CONV_K - 1, D_FF), dt),
        weights, tm_proj=1024, tm_mix=256, tm_ffn=512, tq=512, tq_pad=x_prompt.shape[1])
    bs, ts, _ = x_sample.shape
    y_s, k_s, v_s, c_s, f_s = _trunk(
        x_sample, cache_k[0], cache_v[0], state_conv[0], state_ffn_conv[0],
        weights, tm_proj=bs * ts, tm_mix=bs * ts, tm_ffn=bs * ts, tq=LANES, tq_pad=LANES)
    return (y_p, y_s, k_p[None], v_p[None], c_p[None], f_p[None], k_s[None], v_s[None], c_s[None], f_s[None])
```

```python
import functools
import math

import jax
import jax.numpy as jnp
import numpy as np
from jax import lax
from jax.experimental import pallas as pl
from jax.experimental.pallas import tpu as pltpu

F32 = jnp.float32
BF16 = jnp.bfloat16

D_MODEL = 2048
N_HEADS = 8
HEAD_DIM = 128
ATT_WIDTH = N_HEADS * HEAD_DIM
N_GROUPS = 8
GROUP = 128
CONV_WIDTH = N_GROUPS * GROUP
IN_WIDTH = 3 * ATT_WIDTH + 3 * CONV_WIDTH
D_FF = 5632
CONV_K = 3
EPS = 1e-6
Q_SCALE = (HEAD_DIM ** -0.5) * math.log2(math.e)

SUBLANES = 8
LANES = 128
SEGS = 2 * SUBLANES
SEG_LEN = 32
KV_CHUNK = SEGS * SEG_LEN
VMEM_LIMIT_BYTES = 48 * 1024 * 1024

FF_TILE = 512
IN_PROJ_COL_TILE = 1024
assert ATT_WIDTH == IN_PROJ_COL_TILE and CONV_WIDTH == IN_PROJ_COL_TILE


def _params(sem):
    return pltpu.CompilerParams(dimension_semantics=sem, vmem_limit_bytes=VMEM_LIMIT_BYTES)


def _rms_rows(x, g):
    ms = jnp.mean(x * x, axis=-1, keepdims=True)
    return (x * lax.rsqrt(ms + EPS)) * g


def _key_row_permutation():
    p = np.zeros((KV_CHUNK, KV_CHUNK), np.float32)
    for i in range(SEG_LEN):
        for s in range(SEGS):
            p[SEGS * i + s, SEG_LEN * s + SEG_LEN - 1 - i] = 1.0
    return jnp.asarray(p, BF16)


def _permute_key_rows(a):
    b, t_k, c = a.shape
    a = a.reshape(b, t_k // KV_CHUNK, SEGS, SEG_LEN, c)[:, :, :, ::-1]
    return a.transpose(0, 1, 3, 2, 4).reshape(b, t_k, c)


def _in_proj_kernel(x_ref, g_ref, w_ref, p_ref, q_ref, k_ref, v_ref, bcu_ref, *rest, permute_kv):
    if permute_kv:
        kp_ref, vp_ref, h_ref = rest
    else:
        (h_ref,) = rest
    j = pl.program_id(1)

    @pl.when(j == 0)
    def _():
        h_ref[...] = _rms_rows(x_ref[...], g_ref[...]).astype(BF16)

    acc = jnp.dot(h_ref[...], w_ref[...], preferred_element_type=F32)

    def store_permuted(dst_ref):
        rows_b = acc.astype(BF16)
        for c in range(rows_b.shape[0] // KV_CHUNK):
            rows = slice(c * KV_CHUNK, (c + 1) * KV_CHUNK)
            dst_ref[rows, :] = jnp.dot(p_ref[...], rows_b[rows, :], preferred_element_type=F32).astype(BF16)

    @pl.when(j == 0)
    def _():
        q_ref[...] = (acc * Q_SCALE).astype(BF16)

    @pl.when(j == 1)
    def _():
        k_ref[...] = acc
        if permute_kv:
            store_permuted(kp_ref)

    @pl.when(j == 2)
    def _():
        v_ref[...] = acc
        if permute_kv:
            store_permuted(vp_ref)

    @pl.when(j >= 3)
    def _():
        bcu_ref[...] = acc


def _in_proj(x2d, g_mix, w_in_bf16, *, tm, permute_kv):
    m = x2d.shape[0]
    tn = IN_PROJ_COL_TILE
    n_qkv = 3
    assert not permute_kv or tm % KV_CHUNK == 0
    row_block = lambda i, j: (i, 0)
    out_shape = [jax.ShapeDtypeStruct((m, ATT_WIDTH), BF16),
                 jax.ShapeDtypeStruct((m, ATT_WIDTH), F32),
                 jax.ShapeDtypeStruct((m, ATT_WIDTH), F32),
                 jax.ShapeDtypeStruct((m, 3 * CONV_WIDTH), F32)]
    out_specs = [pl.BlockSpec((tm, tn), row_block),
                 pl.BlockSpec((tm, tn), row_block),
                 pl.BlockSpec((tm, tn), row_block),
                 pl.BlockSpec((tm, tn), lambda i, j: (i, jnp.maximum(j - n_qkv, 0)))]
    if permute_kv:
        out_shape += [jax.ShapeDtypeStruct((m, ATT_WIDTH), BF16)] * 2
        out_specs += [pl.BlockSpec((tm, tn), row_block)] * 2
    return pl.pallas_call(
        functools.partial(_in_proj_kernel, permute_kv=permute_kv),
        out_shape=tuple(out_shape),
        grid=(m // tm, IN_WIDTH // tn),
        in_specs=[
            pl.BlockSpec((tm, D_MODEL), lambda i, j: (i, 0)),
            pl.BlockSpec((1, D_MODEL), lambda i, j: (0, 0)),
            pl.BlockSpec((D_MODEL, tn), lambda i, j: (0, j)),
            pl.BlockSpec((KV_CHUNK, KV_CHUNK), lambda i, j: (0, 0)),
        ],
        out_specs=tuple(out_specs),
        scratch_shapes=[pltpu.VMEM((tm, D_MODEL), BF16)],
        compiler_params=_params(("arbitrary", "arbitrary")),
        name="in_proj",
    )(x2d, g_mix.reshape(1, D_MODEL), w_in_bf16, _key_row_permutation())


def _softplus2(z):
    e = jnp.exp2(-jnp.abs(z))
    return jnp.maximum(z, 0.0) + jnp.log(1.0 + e) * math.log2(math.e)


def _attn_kernel(q_ref, k_ref, v_ref, g_ref, o_ref, d_ref, w_ref, off_ref, acc_ref, carry_ref,
                 *, past, tq, n_chunks_total):
    qi = pl.program_id(1)
    q_lo = past + qi * tq
    j_top = q_lo >> (KV_CHUNK.bit_length() - 1)

    q = q_ref[0]
    qpos = q_lo + lax.broadcasted_iota(jnp.int32, (SEGS, tq), 1)
    top_key = j_top * KV_CHUNK + SEG_LEN * lax.broadcasted_iota(jnp.int32, (SEGS, tq), 0) + (SEG_LEN - 1)

    def causal(i):
        return top_key - i < qpos

    def scores(j):
        base = pl.multiple_of(j * KV_CHUNK, KV_CHUNK)
        kc = k_ref[0, pl.ds(base, KV_CHUNK), :]
        return lax.dot_general(kc, q, (((1,), (1,)), ((), ())), preferred_element_type=F32)

    def add_values(j):
        base = pl.multiple_of(j * KV_CHUNK, KV_CHUNK)
        vc = v_ref[0, pl.ds(base, KV_CHUNK), :]
        acc_ref[...] += lax.dot_general(vc, w_ref[...], (((0,), (0,)), ((), ())), preferred_element_type=F32)

    def pass1(z_all, masked):
        run = jnp.zeros((SEGS, tq), F32)
        for i in range(SEG_LEN):
            rows = slice(i * SEGS, (i + 1) * SEGS)
            z = z_all[rows, :]
            sp = _softplus2(z)
            if masked:
                sp = jnp.where(causal(i), sp, 0.0)
            run = run + sp
            d_ref[rows, :] = z - run
        tail = carry_ref[0:1, :]
        offs = [None] * SEGS
        for s in reversed(range(SEGS)):
            offs[s] = tail
            tail = tail + run[s:s + 1, :]
        carry_ref[...] = jnp.broadcast_to(tail, carry_ref.shape)
        off_ref[...] = jnp.concatenate(offs, axis=0)

    def pass2(masked):
        off = off_ref[...]
        for i in range(SEG_LEN):
            rows = slice(i * SEGS, (i + 1) * SEGS)
            w = jnp.exp2(d_ref[rows, :] - off)
            if masked:
                w = jnp.where(causal(i), w, 0.0)
            w_ref[rows, :] = w.astype(BF16)

    def step(j, prev_masked):
        z_all = scores(j)
        pass2(prev_masked)
        pass1(z_all, False)
        add_values(j + 1)

    acc_ref[...] = jnp.zeros_like(acc_ref)
    carry_ref[...] = jnp.zeros_like(carry_ref)
    pass1(scores(j_top), True)

    @pl.when(j_top > 0)
    def _():
        step(j_top - 1, True)

    def body(t, c):
        step(j_top - 2 - t, False)
        return c

    lax.fori_loop(0, jnp.maximum(j_top - 1, 0), body, 0)

    @pl.when(j_top == 0)
    def _():
        pass2(True)

    @pl.when(j_top > 0)
    def _():
        pass2(False)

    add_values(0)

    out = acc_ref[...]
    ms = jnp.mean(out * out, axis=0, keepdims=True)
    y = (out * lax.rsqrt(ms + EPS)) * g_ref[0]
    o_ref[0] = y.T


def _attention(q, k_perm, v_perm, g_att, *, past, tq):
    b, t_q, _ = q.shape
    t_k = k_perm.shape[1]
    assert KV_CHUNK % tq == 0 and past % KV_CHUNK == 0 and t_q % tq == 0
    assert t_k % KV_CHUNK == 0 and t_k >= -(-(past + t_q) // KV_CHUNK) * KV_CHUNK
    kernel = functools.partial(_attn_kernel, past=past, tq=tq, n_chunks_total=t_k // KV_CHUNK)
    return pl.pallas_call(
        kernel,
        out_shape=jax.ShapeDtypeStruct((b, t_q, ATT_WIDTH), F32),
        grid=(b * N_HEADS, t_q // tq),
        in_specs=[
            pl.BlockSpec((1, tq, HEAD_DIM), lambda bh, i: (bh // N_HEADS, i, bh % N_HEADS)),
            pl.BlockSpec((1, t_k, HEAD_DIM), lambda bh, i: (bh // N_HEADS, 0, bh % N_HEADS)),
            pl.BlockSpec((1, t_k, HEAD_DIM), lambda bh, i: (bh // N_HEADS, 0, bh % N_HEADS)),
            pl.BlockSpec((1, HEAD_DIM, 1), lambda bh, i: (bh % N_HEADS, 0, 0)),
        ],
        out_specs=pl.BlockSpec((1, tq, HEAD_DIM), lambda bh, i: (bh // N_HEADS, i, bh % N_HEADS)),
        scratch_shapes=[
            pltpu.VMEM((KV_CHUNK, tq), F32),
            pltpu.VMEM((KV_CHUNK, tq), BF16),
            pltpu.VMEM((SEGS, tq), F32),
            pltpu.VMEM((HEAD_DIM, tq), F32),
            pltpu.VMEM((SUBLANES, tq), F32),
        ],
        compiler_params=_params(("arbitrary", "arbitrary")),
        name="sb_attention",
    )(q, k_perm, v_perm, g_att.reshape(N_HEADS, HEAD_DIM, 1))


def _dwconv3_rows(cur, w_ref, full_ref, cols):
    n = cur.shape[0]
    full_ref[SUBLANES:SUBLANES + n, cols] = cur
    return (full_ref[SUBLANES - 2:SUBLANES - 2 + n, cols] * w_ref[0:1, cols]
            + full_ref[SUBLANES - 1:SUBLANES - 1 + n, cols] * w_ref[1:2, cols]
            + cur * w_ref[2:3, cols])


def _seq_geometry(tm, seq_len):
    sub_len = min(seq_len, tm)
    return sub_len, tm // sub_len, max(seq_len // tm, 1)


def _mix_kernel(att_ref, b_ref, c_ref, u_ref, cp_ref, up_ref, st_ref, wc_ref, gc_ref, woa_ref, wob_ref,
                x_ref, o_ref, nc_ref, full_ref, yc_ref, *, tm, seq_len):
    i = pl.program_id(0)
    sub_len, n_sub, tiles_per_seq = _seq_geometry(tm, seq_len)
    halo_rows = slice(0, SUBLANES)
    all_cols = slice(0, CONV_WIDTH)
    if tiles_per_seq > 1:
        @pl.when(i % tiles_per_seq == 0)
        def _():
            full_ref[halo_rows, :] = st_ref[0]

        @pl.when(i % tiles_per_seq != 0)
        def _():
            full_ref[halo_rows, :] = cp_ref[...] * up_ref[...]

    out = x_ref[...] + jnp.dot(att_ref[...].astype(BF16), woa_ref[...], preferred_element_type=F32)
    cu = c_ref[...] * u_ref[...]
    for s in range(n_sub):
        rows = slice(s * sub_len, (s + 1) * sub_len)
        if tiles_per_seq == 1:
            full_ref[halo_rows, :] = st_ref[s]
        cur = cu[rows, :]
        y = b_ref[rows, :] * _dwconv3_rows(cur, wc_ref, full_ref, all_cols)
        for g in range(N_GROUPS):
            cols = slice(g * GROUP, (g + 1) * GROUP)
            yc_ref[rows, cols] = _rms_rows(y[:, cols], gc_ref[:, cols]).astype(BF16)
        if tiles_per_seq == 1:
            nc_ref[i * n_sub + s] = cur[sub_len - (CONV_K - 1):, :]

    o_ref[...] = out + jnp.dot(yc_ref[...], wob_ref[...], preferred_element_type=F32)

    if tiles_per_seq > 1:
        @pl.when(i % tiles_per_seq == tiles_per_seq - 1)
        def _():
            last = SUBLANES + sub_len
            nc_ref[i // tiles_per_seq] = full_ref[last - (CONV_K - 1):last, :]


def _state_rows(state):
    return jnp.pad(state, ((0, 0), (SUBLANES - (CONV_K - 1), 0), (0, 0)))


def _mix(att2d, bcu, state, w_conv, g_conv, w_o_bf16, x2d, *, tm, seq_len):
    m = x2d.shape[0]
    n_seq = m // seq_len
    sub_len, n_sub, tiles_per_seq = _seq_geometry(tm, seq_len)
    rows_per_block = tm // SUBLANES
    prev_map = lambda c: (lambda i: (jnp.maximum(i * rows_per_block - 1, 0), c))
    kernel = functools.partial(_mix_kernel, tm=tm, seq_len=seq_len)
    return pl.pallas_call(
        kernel,
        out_shape=(jax.ShapeDtypeStruct((m, D_MODEL), F32),
                   jax.ShapeDtypeStruct((n_seq, CONV_K - 1, CONV_WIDTH), F32)),
        grid=(m // tm,),
        in_specs=[
            pl.BlockSpec((tm, ATT_WIDTH), lambda i: (i, 0)),
            pl.BlockSpec((tm, CONV_WIDTH), lambda i: (i, 0)),
            pl.BlockSpec((tm, CONV_WIDTH), lambda i: (i, 1)),
            pl.BlockSpec((tm, CONV_WIDTH), lambda i: (i, 2)),
            pl.BlockSpec((SUBLANES, CONV_WIDTH), prev_map(1)),
            pl.BlockSpec((SUBLANES, CONV_WIDTH), prev_map(2)),
            pl.BlockSpec((n_sub, SUBLANES, CONV_WIDTH), lambda i: (i // tiles_per_seq, 0, 0)),
            pl.BlockSpec((CONV_K, CONV_WIDTH), lambda i: (0, 0)),
            pl.BlockSpec((1, CONV_WIDTH), lambda i: (0, 0)),
            pl.BlockSpec((ATT_WIDTH, D_MODEL), lambda i: (0, 0)),
            pl.BlockSpec((CONV_WIDTH, D_MODEL), lambda i: (1, 0)),
            pl.BlockSpec((tm, D_MODEL), lambda i: (i, 0)),
        ],
        out_specs=(pl.BlockSpec((tm, D_MODEL), lambda i: (i, 0)),
                   pl.BlockSpec((n_seq, CONV_K - 1, CONV_WIDTH), lambda i: (0, 0, 0))),
        scratch_shapes=[pltpu.VMEM((SUBLANES + sub_len, CONV_WIDTH), F32),
                        pltpu.VMEM((tm, CONV_WIDTH), BF16)],
        compiler_params=_params(("arbitrary",)),
        name="mix_out_proj",
    )(att2d, bcu, bcu, bcu, bcu, bcu, _state_rows(state), w_conv, g_conv.reshape(1, CONV_WIDTH),
      w_o_bf16, w_o_bf16, x2d)


def _ffn_kernel(x_ref, g_ref, wg_ref, wu_ref, wc_ref, st_ref, wd_ref, gf_ref, o_ref, nf_ref,
                h_ref, halo_ref, full_ref, act_ref, *, tm, seq_len, n_ff):
    i = pl.program_id(0)
    j = pl.program_id(1)
    sub_len, n_sub, tiles_per_seq = _seq_geometry(tm, seq_len)
    tf = wg_ref.shape[1]
    half = tf // 2

    @pl.when(j == 0)
    def _():
        x = x_ref[...]
        h_ref[...] = _rms_rows(x, g_ref[...]).astype(BF16)
        o_ref[...] = x

    halo_rows = slice(0, SUBLANES)
    if tiles_per_seq > 1:
        @pl.when(i % tiles_per_seq == 0)
        def _():
            halo_ref[j] = st_ref[0]

    h = h_ref[...]
    for c in range(2):
        cols = slice(c * half, (c + 1) * half)
        gate = jnp.dot(h, wg_ref[:, cols], preferred_element_type=F32)
        up = jnp.dot(h, wu_ref[:, cols], preferred_element_type=F32)
        for s in range(n_sub):
            rows = slice(s * sub_len, (s + 1) * sub_len)
            if tiles_per_seq == 1:
                full_ref[halo_rows, cols] = st_ref[s, :, cols]
            else:
                full_ref[halo_rows, cols] = halo_ref[j, :, cols]
            cur = gate[rows, :]
            gc = _dwconv3_rows(cur, wc_ref, full_ref, cols)
            act_ref[rows, cols] = (gc * jax.nn.sigmoid(gc) * up[rows, :]).astype(BF16)
            if tiles_per_seq == 1:
                nf_ref[i * n_sub + s, j, :, cols] = cur[sub_len - (CONV_K - 1):, :]
            else:
                halo_ref[j, :, cols] = cur[sub_len - SUBLANES:, :]

    o_ref[...] += jnp.dot(act_ref[...], wd_ref[...], preferred_element_type=F32)

    if tiles_per_seq > 1:
        @pl.when(i % tiles_per_seq == tiles_per_seq - 1)
        def _():
            nf_ref[i // tiles_per_seq, j] = halo_ref[j, SUBLANES - (CONV_K - 1):, :]

    @pl.when(j == n_ff - 1)
    def _():
        o_ref[...] = _rms_rows(o_ref[...], gf_ref[...])


def _ffn(x2d, g_ffn, w_gate_up_bf16, w_ffn_conv, state, w_down_bf16, g_final, *, tm, seq_len):
    m = x2d.shape[0]
    n_seq = m // seq_len
    tf = FF_TILE
    n_ff = D_FF // tf
    sub_len, n_sub, tiles_per_seq = _seq_geometry(tm, seq_len)
    kernel = functools.partial(_ffn_kernel, tm=tm, seq_len=seq_len, n_ff=n_ff)
    y, new_ffn = pl.pallas_call(
        kernel,
        out_shape=(jax.ShapeDtypeStruct((m, D_MODEL), F32),
                   jax.ShapeDtypeStruct((n_seq, n_ff, CONV_K - 1, tf), F32)),
        grid=(m // tm, n_ff),
        in_specs=[
            pl.BlockSpec((tm, D_MODEL), lambda i, j: (i, 0)),
            pl.BlockSpec((1, D_MODEL), lambda i, j: (0, 0)),
            pl.BlockSpec((D_MODEL, tf), lambda i, j: (0, j)),
            pl.BlockSpec((D_MODEL, tf), lambda i, j: (0, j + n_ff)),
            pl.BlockSpec((CONV_K, tf), lambda i, j: (0, j)),
            pl.BlockSpec((n_sub, SUBLANES, tf), lambda i, j: (i // tiles_per_seq, 0, j)),
            pl.BlockSpec((tf, D_MODEL), lambda i, j: (j, 0)),
            pl.BlockSpec((1, D_MODEL), lambda i, j: (0, 0)),
        ],
        out_specs=(pl.BlockSpec((tm, D_MODEL), lambda i, j: (i, 0)),
                   pl.BlockSpec((n_seq, n_ff, CONV_K - 1, tf), lambda i, j: (0, 0, 0, 0))),
        scratch_shapes=[pltpu.VMEM((tm, D_MODEL), BF16),
                        pltpu.VMEM((n_ff, SUBLANES, tf), F32),
                        pltpu.VMEM((SUBLANES + sub_len, tf), F32),
                        pltpu.VMEM((tm, tf), BF16)],
        compiler_params=_params(("arbitrary", "arbitrary")),
        name="conv_ffn",
    )(x2d, g_ffn.reshape(1, D_MODEL), w_gate_up_bf16, w_gate_up_bf16, w_ffn_conv, _state_rows(state),
      w_down_bf16, g_final.reshape(1, D_MODEL))
    return y, new_ffn.transpose(0, 2, 1, 3).reshape(n_seq, CONV_K - 1, D_FF)


def _tiles(batch, seq_len):
    rows = batch * seq_len
    if rows >= 4096:
        return dict(tm_proj=512, tm_mix=256, tm_ffn=512, tq=KV_CHUNK)
    return dict(tm_proj=rows, tm_mix=rows, tm_ffn=rows, tq=LANES)


def _trunk(x, cache_k, cache_v, state_conv, state_ffn, weights):
    g_mix, w_in, w_conv, g_att, g_conv, w_o, g_ffn, w_gate_up, w_ffn_conv, w_down, g_final = weights
    b, t, _ = x.shape
    past = cache_k.shape[1]
    tiles = _tiles(b, t)
    tq = tiles["tq"]
    x2d = x.reshape(b * t, D_MODEL)

    permute_in_proj = past == 0 and t % KV_CHUNK == 0 and tiles["tm_proj"] % KV_CHUNK == 0
    outs = _in_proj(x2d, g_mix, w_in, tm=tiles["tm_proj"], permute_kv=permute_in_proj)
    q, k, v, bcu = outs[:4]
    tq_pad = -(-t // tq) * tq
    q3 = jnp.pad(q.reshape(b, t, ATT_WIDTH), ((0, 0), (0, tq_pad - t), (0, 0)))
    if permute_in_proj:
        k_perm, v_perm = (a.reshape(b, t, ATT_WIDTH) for a in outs[4:])
    else:
        def key_layout(cache, new):
            allk = jnp.concatenate([cache.reshape(b, past, ATT_WIDTH), new.reshape(b, t, ATT_WIDTH)], axis=1)
            tk_pad = -(-(past + tq_pad) // KV_CHUNK) * KV_CHUNK
            allk = jnp.pad(allk.astype(BF16), ((0, 0), (0, tk_pad - past - t), (0, 0)))
            return _permute_key_rows(allk)
        k_perm, v_perm = key_layout(cache_k, k), key_layout(cache_v, v)
    att = _attention(q3, k_perm, v_perm, g_att, past=past, tq=tq)
    att2d = att[:, :t, :].reshape(b * t, ATT_WIDTH)

    x1, new_conv = _mix(att2d, bcu, state_conv, w_conv, g_conv, w_o, x2d, tm=tiles["tm_mix"], seq_len=t)
    y, new_ffn = _ffn(x1, g_ffn, w_gate_up, w_ffn_conv, state_ffn, w_down, g_final, tm=tiles["tm_ffn"], seq_len=t)
    shape4 = (b, t, N_HEADS, HEAD_DIM)
    return y.reshape(b, t, D_MODEL), k.reshape(shape4), v.reshape(shape4), new_conv, new_ffn


def kernel(x_prompt, x_sample, cache_k, cache_v, state_conv, state_ffn_conv, g_mix, w_in, w_conv, g_att_out,
           g_conv_out, w_o, g_ffn, w_gate_up, w_ffn_conv, w_down, g_final):
    depth = g_mix.shape[0]
    assert depth == 1
    weights = (g_mix[0], w_in[0].astype(BF16), w_conv[0], g_att_out[0], g_conv_out[0], w_o[0].astype(BF16),
               g_ffn[0], w_gate_up[0].astype(BF16), w_ffn_conv[0], w_down[0].astype(BF16), g_final)
    bp = x_prompt.shape[0]
    dt = x_prompt.dtype
    empty = jnp.zeros((bp, 0, N_HEADS, HEAD_DIM), dt)
    y_p, k_p, v_p, c_p, f_p = _trunk(
        x_prompt, empty, empty, jnp.zeros((bp, CONV_K - 1, CONV_WIDTH), dt), jnp.zeros((bp, CONV_K - 1, D_FF), dt),
        weights)
    y_s, k_s, v_s, c_s, f_s = _trunk(x_sample, cache_k[0], cache_v[0], state_conv[0], state_ffn_conv[0], weights)
    return (y_p, y_s, k_p[None], v_p[None], c_p[None], f_p[None], k_s[None], v_s[None], c_s[None], f_s[None])
```

```python
import functools
import math

import jax
import jax.numpy as jnp
import numpy as np
from jax import lax
from jax.experimental import pallas as pl
from jax.experimental.pallas import tpu as pltpu

F32 = jnp.float32
BF16 = jnp.bfloat16

D_MODEL = 2048
N_HEADS = 8
HEAD_DIM = 128
ATT_WIDTH = N_HEADS * HEAD_DIM
N_GROUPS = 8
GROUP = 128
CONV_WIDTH = N_GROUPS * GROUP
IN_WIDTH = 3 * ATT_WIDTH + 3 * CONV_WIDTH
D_FF = 5632
CONV_K = 3
EPS = 1e-6
Q_SCALE = (HEAD_DIM ** -0.5) * math.log2(math.e)

SUBLANES = 8
LANES = 128
SEGS = 2 * SUBLANES
SEG_LEN = 32
KV_CHUNK = SEGS * SEG_LEN
VMEM_LIMIT_BYTES = 48 * 1024 * 1024

FF_TILE = 512
IN_PROJ_COL_TILE = 1024
assert ATT_WIDTH == IN_PROJ_COL_TILE and CONV_WIDTH == IN_PROJ_COL_TILE


def _params(sem):
    return pltpu.CompilerParams(dimension_semantics=sem, vmem_limit_bytes=VMEM_LIMIT_BYTES)


def _rms_rows(x, g):
    ms = jnp.mean(x * x, axis=-1, keepdims=True)
    return (x * lax.rsqrt(ms + EPS)) * g


def _key_row_permutation():
    p = np.zeros((KV_CHUNK, KV_CHUNK), np.float32)
    for i in range(SEG_LEN):
        for s in range(SEGS):
            p[SEGS * i + s, SEG_LEN * s + SEG_LEN - 1 - i] = 1.0
    return jnp.asarray(p, BF16)


def _in_proj_kernel(x_ref, g_ref, w_ref, o_ref, h_ref):
    j = pl.program_id(1)

    @pl.when(j == 0)
    def _():
        h_ref[...] = _rms_rows(x_ref[...], g_ref[...]).astype(BF16)

    acc = jnp.dot(h_ref[...], w_ref[...], preferred_element_type=F32)
    o_ref[...] = acc * jnp.where(j == 0, Q_SCALE, 1.0).astype(F32)


def _in_proj(x2d, g_mix, w_in_bf16, *, tm):
    m = x2d.shape[0]
    tn = IN_PROJ_COL_TILE
    return pl.pallas_call(
        _in_proj_kernel,
        out_shape=jax.ShapeDtypeStruct((m, IN_WIDTH), F32),
        grid=(m // tm, IN_WIDTH // tn),
        in_specs=[
            pl.BlockSpec((tm, D_MODEL), lambda i, j: (i, 0)),
            pl.BlockSpec((1, D_MODEL), lambda i, j: (0, 0)),
            pl.BlockSpec((D_MODEL, tn), lambda i, j: (0, j)),
        ],
        out_specs=pl.BlockSpec((tm, tn), lambda i, j: (i, j)),
        scratch_shapes=[pltpu.VMEM((tm, D_MODEL), BF16)],
        compiler_params=_params(("arbitrary", "arbitrary")),
        name="in_proj",
    )(x2d, g_mix.reshape(1, D_MODEL), w_in_bf16)


def _to_heads_kernel(p_ref, x_ref, o_ref, *copy_ref, permute):
    x32 = x_ref[0]
    for ref in copy_ref:
        ref[0] = x32
    x = x32.astype(BF16)
    if permute:
        x = jnp.dot(p_ref[...], x, preferred_element_type=F32).astype(BF16)
    for h in range(N_HEADS):
        o_ref[0, h] = x[:, h * HEAD_DIM:(h + 1) * HEAD_DIM]


def _to_heads(src, col_block, *, permute, keep_f32=False):
    b, t, _ = src.shape
    rows = KV_CHUNK if permute else math.gcd(t, KV_CHUNK)
    assert t % rows == 0 and rows % (2 * SUBLANES) == 0
    out_shape = [jax.ShapeDtypeStruct((b, N_HEADS, t, HEAD_DIM), BF16)]
    out_specs = [pl.BlockSpec((1, N_HEADS, rows, HEAD_DIM), lambda bi, i: (bi, 0, i, 0))]
    if keep_f32:
        out_shape.append(jax.ShapeDtypeStruct((b, t, ATT_WIDTH), F32))
        out_specs.append(pl.BlockSpec((1, rows, ATT_WIDTH), lambda bi, i: (bi, i, 0)))
    outs = pl.pallas_call(
        functools.partial(_to_heads_kernel, permute=permute),
        out_shape=tuple(out_shape),
        grid=(b, t // rows),
        in_specs=[
            pl.BlockSpec((KV_CHUNK, KV_CHUNK), lambda bi, i: (0, 0)),
            pl.BlockSpec((1, rows, ATT_WIDTH), lambda bi, i: (bi, i, col_block)),
        ],
        out_specs=tuple(out_specs),
        compiler_params=_params(("arbitrary", "arbitrary")),
        name="to_heads",
    )(_key_row_permutation(), src)
    heads = outs[0].reshape(b * N_HEADS, t, HEAD_DIM)
    return (heads, outs[1]) if keep_f32 else heads


def _softplus2(z):
    e = jnp.exp2(-jnp.abs(z))
    return jnp.maximum(z, 0.0) + jnp.log(1.0 + e) * math.log2(math.e)


def _attn_kernel(q_ref, k_ref, v_ref, g_ref, o_ref, d_ref, w_ref, off_ref, acc_ref, carry_ref,
                 *, past, tq, n_chunks_total):
    qi = pl.program_id(1)
    q_lo = past + qi * tq
    j_top = q_lo >> (KV_CHUNK.bit_length() - 1)

    q = q_ref[0]
    qpos = q_lo + lax.broadcasted_iota(jnp.int32, (SEGS, tq), 1)
    top_key = j_top * KV_CHUNK + SEG_LEN * lax.broadcasted_iota(jnp.int32, (SEGS, tq), 0) + (SEG_LEN - 1)

    def causal(i):
        return top_key - i < qpos

    def scores(j):
        base = pl.multiple_of(j * KV_CHUNK, KV_CHUNK)
        kc = k_ref[0, pl.ds(base, KV_CHUNK), :]
        return lax.dot_general(kc, q, (((1,), (1,)), ((), ())), preferred_element_type=F32)

    def add_values(j):
        base = pl.multiple_of(j * KV_CHUNK, KV_CHUNK)
        vc = v_ref[0, pl.ds(base, KV_CHUNK), :]
        acc_ref[...] += lax.dot_general(vc, w_ref[...], (((0,), (0,)), ((), ())), preferred_element_type=F32)

    def pass1(z_all, masked):
        run = jnp.zeros((SEGS, tq), F32)
        for i in range(SEG_LEN):
            rows = slice(i * SEGS, (i + 1) * SEGS)
            z = z_all[rows, :]
            sp = _softplus2(z)
            if masked:
                sp = jnp.where(causal(i), sp, 0.0)
            run = run + sp
            d_ref[rows, :] = z - run
        tail = carry_ref[0:1, :]
        offs = [None] * SEGS
        for s in reversed(range(SEGS)):
            offs[s] = tail
            tail = tail + run[s:s + 1, :]
        carry_ref[...] = jnp.broadcast_to(tail, carry_ref.shape)
        off_ref[...] = jnp.concatenate(offs, axis=0)

    def pass2(masked):
        off = off_ref[...]
        for i in range(SEG_LEN):
            rows = slice(i * SEGS, (i + 1) * SEGS)
            w = jnp.exp2(d_ref[rows, :] - off)
            if masked:
                w = jnp.where(causal(i), w, 0.0)
            w_ref[rows, :] = w.astype(BF16)

    def step(j, prev_masked):
        z_all = scores(j)
        pass2(prev_masked)
        pass1(z_all, False)
        add_values(j + 1)

    acc_ref[...] = jnp.zeros_like(acc_ref)
    carry_ref[...] = jnp.zeros_like(carry_ref)
    pass1(scores(j_top), True)

    @pl.when(j_top > 0)
    def _():
        step(j_top - 1, True)

    def body(t, c):
        step(j_top - 2 - t, False)
        return c

    lax.fori_loop(0, jnp.maximum(j_top - 1, 0), body, 0)

    @pl.when(j_top == 0)
    def _():
        pass2(True)

    @pl.when(j_top > 0)
    def _():
        pass2(False)

    add_values(0)

    out = acc_ref[...]
    ms = jnp.mean(out * out, axis=0, keepdims=True)
    y = (out * lax.rsqrt(ms + EPS)) * g_ref[0]
    o_ref[0] = y.T


def _attention(q, k_perm, v_perm, g_att, *, past, tq):
    bh_total, t_q, _ = q.shape
    b = bh_total // N_HEADS
    t_k = k_perm.shape[1]
    assert KV_CHUNK % tq == 0 and past % KV_CHUNK == 0 and t_q % tq == 0
    assert t_k % KV_CHUNK == 0 and t_k >= -(-(past + t_q) // KV_CHUNK) * KV_CHUNK
    kernel = functools.partial(_attn_kernel, past=past, tq=tq, n_chunks_total=t_k // KV_CHUNK)
    return pl.pallas_call(
        kernel,
        out_shape=jax.ShapeDtypeStruct((b, t_q, ATT_WIDTH), F32),
        grid=(b * N_HEADS, t_q // tq),
        in_specs=[
            pl.BlockSpec((1, tq, HEAD_DIM), lambda bh, i: (bh, i, 0)),
            pl.BlockSpec((1, t_k, HEAD_DIM), lambda bh, i: (bh, 0, 0)),
            pl.BlockSpec((1, t_k, HEAD_DIM), lambda bh, i: (bh, 0, 0)),
            pl.BlockSpec((1, HEAD_DIM, 1), lambda bh, i: (bh % N_HEADS, 0, 0)),
        ],
        out_specs=pl.BlockSpec((1, tq, HEAD_DIM), lambda bh, i: (bh // N_HEADS, i, bh % N_HEADS)),
        scratch_shapes=[
            pltpu.VMEM((KV_CHUNK, tq), F32),
            pltpu.VMEM((KV_CHUNK, tq), BF16),
            pltpu.VMEM((SEGS, tq), F32),
            pltpu.VMEM((HEAD_DIM, tq), F32),
            pltpu.VMEM((SUBLANES, tq), F32),
        ],
        compiler_params=_params(("arbitrary", "arbitrary")),
        name="sb_attention",
    )(q, k_perm, v_perm, g_att.reshape(N_HEADS, HEAD_DIM, 1))


def _dwconv3_rows(cur, w_ref, full_ref, cols):
    n = cur.shape[0]
    full_ref[SUBLANES:SUBLANES + n, cols] = cur
    return (full_ref[SUBLANES - 2:SUBLANES - 2 + n, cols] * w_ref[0:1, cols]
            + full_ref[SUBLANES - 1:SUBLANES - 1 + n, cols] * w_ref[1:2, cols]
            + cur * w_ref[2:3, cols])


def _seq_geometry(tm, seq_len):
    sub_len = min(seq_len, tm)
    return sub_len, tm // sub_len, max(seq_len // tm, 1)


def _mix_kernel(att_ref, b_ref, c_ref, u_ref, cp_ref, up_ref, st_ref, wc_ref, gc_ref, woa_ref, wob_ref,
                x_ref, o_ref, nc_ref, full_ref, yc_ref, *, tm, seq_len):
    i = pl.program_id(0)
    sub_len, n_sub, tiles_per_seq = _seq_geometry(tm, seq_len)
    halo_rows = slice(0, SUBLANES)
    all_cols = slice(0, CONV_WIDTH)
    if tiles_per_seq > 1:
        @pl.when(i % tiles_per_seq == 0)
        def _():
            full_ref[halo_rows, :] = st_ref[0]

        @pl.when(i % tiles_per_seq != 0)
        def _():
            full_ref[halo_rows, :] = cp_ref[...] * up_ref[...]

    out = x_ref[...] + jnp.dot(att_ref[...].astype(BF16), woa_ref[...], preferred_element_type=F32)
    cu = c_ref[...] * u_ref[...]
    for s in range(n_sub):
        rows = slice(s * sub_len, (s + 1) * sub_len)
        if tiles_per_seq == 1:
            full_ref[halo_rows, :] = st_ref[s]
        cur = cu[rows, :]
        y = b_ref[rows, :] * _dwconv3_rows(cur, wc_ref, full_ref, all_cols)
        for g in range(N_GROUPS):
            cols = slice(g * GROUP, (g + 1) * GROUP)
            yc_ref[rows, cols] = _rms_rows(y[:, cols], gc_ref[:, cols]).astype(BF16)
        if tiles_per_seq == 1:
            nc_ref[i * n_sub + s] = cur[sub_len - (CONV_K - 1):, :]

    o_ref[...] = out + jnp.dot(yc_ref[...], wob_ref[...], preferred_element_type=F32)

    if tiles_per_seq > 1:
        @pl.when(i % tiles_per_seq == tiles_per_seq - 1)
        def _():
            last = SUBLANES + sub_len
            nc_ref[i // tiles_per_seq] = full_ref[last - (CONV_K - 1):last, :]


def _state_rows(state):
    return jnp.pad(state, ((0, 0), (SUBLANES - (CONV_K - 1), 0), (0, 0)))


def _mix(att2d, proj, state, w_conv, g_conv, w_o_bf16, x2d, *, tm, seq_len):
    m = x2d.shape[0]
    n_seq = m // seq_len
    sub_len, n_sub, tiles_per_seq = _seq_geometry(tm, seq_len)
    rows_per_block = tm // SUBLANES
    col0 = 3 * ATT_WIDTH // CONV_WIDTH
    prev_map = lambda c: (lambda i: (jnp.maximum(i * rows_per_block - 1, 0), c))
    kernel = functools.partial(_mix_kernel, tm=tm, seq_len=seq_len)
    return pl.pallas_call(
        kernel,
        out_shape=(jax.ShapeDtypeStruct((m, D_MODEL), F32),
                   jax.ShapeDtypeStruct((n_seq, CONV_K - 1, CONV_WIDTH), F32)),
        grid=(m // tm,),
        in_specs=[
            pl.BlockSpec((tm, ATT_WIDTH), lambda i: (i, 0)),
            pl.BlockSpec((tm, CONV_WIDTH), lambda i: (i, col0)),
            pl.BlockSpec((tm, CONV_WIDTH), lambda i: (i, col0 + 1)),
            pl.BlockSpec((tm, CONV_WIDTH), lambda i: (i, col0 + 2)),
            pl.BlockSpec((SUBLANES, CONV_WIDTH), prev_map(col0 + 1)),
            pl.BlockSpec((SUBLANES, CONV_WIDTH), prev_map(col0 + 2)),
            pl.BlockSpec((n_sub, SUBLANES, CONV_WIDTH), lambda i: (i // tiles_per_seq, 0, 0)),
            pl.BlockSpec((CONV_K, CONV_WIDTH), lambda i: (0, 0)),
            pl.BlockSpec((1, CONV_WIDTH), lambda i: (0, 0)),
            pl.BlockSpec((ATT_WIDTH, D_MODEL), lambda i: (0, 0)),
            pl.BlockSpec((CONV_WIDTH, D_MODEL), lambda i: (1, 0)),
            pl.BlockSpec((tm, D_MODEL), lambda i: (i, 0)),
        ],
        out_specs=(pl.BlockSpec((tm, D_MODEL), lambda i: (i, 0)),
                   pl.BlockSpec((n_seq, CONV_K - 1, CONV_WIDTH), lambda i: (0, 0, 0))),
        scratch_shapes=[pltpu.VMEM((SUBLANES + sub_len, CONV_WIDTH), F32),
                        pltpu.VMEM((tm, CONV_WIDTH), BF16)],
        compiler_params=_params(("arbitrary",)),
        name="mix_out_proj",
    )(att2d, proj, proj, proj, proj, proj, _state_rows(state), w_conv, g_conv.reshape(1, CONV_WIDTH),
      w_o_bf16, w_o_bf16, x2d)


def _ffn_kernel(x_ref, g_ref, wg_ref, wu_ref, wc_ref, st_ref, wd_ref, gf_ref, o_ref, nf_ref,
                h_ref, halo_ref, full_ref, act_ref, *, tm, seq_len, n_ff):
    i = pl.program_id(0)
    j = pl.program_id(1)
    sub_len, n_sub, tiles_per_seq = _seq_geometry(tm, seq_len)
    tf = wg_ref.shape[1]
    half = tf // 2

    @pl.when(j == 0)
    def _():
        x = x_ref[...]
        h_ref[...] = _rms_rows(x, g_ref[...]).astype(BF16)
        o_ref[...] = x

    halo_rows = slice(0, SUBLANES)
    if tiles_per_seq > 1:
        @pl.when(i % tiles_per_seq == 0)
        def _():
            halo_ref[j] = st_ref[0]

    h = h_ref[...]
    for c in range(2):
        cols = slice(c * half, (c + 1) * half)
        gate = jnp.dot(h, wg_ref[:, cols], preferred_element_type=F32)
        up = jnp.dot(h, wu_ref[:, cols], preferred_element_type=F32)
        for s in range(n_sub):
            rows = slice(s * sub_len, (s + 1) * sub_len)
            if tiles_per_seq == 1:
                full_ref[halo_rows, cols] = st_ref[s, :, cols]
            else:
                full_ref[halo_rows, cols] = halo_ref[j, :, cols]
            cur = gate[rows, :]
            gc = _dwconv3_rows(cur, wc_ref, full_ref, cols)
            act_ref[rows, cols] = (gc * jax.nn.sigmoid(gc) * up[rows, :]).astype(BF16)
            if tiles_per_seq == 1:
                nf_ref[i * n_sub + s, j, :, cols] = cur[sub_len - (CONV_K - 1):, :]
            else:
                halo_ref[j, :, cols] = cur[sub_len - SUBLANES:, :]

    o_ref[...] += jnp.dot(act_ref[...], wd_ref[...], preferred_element_type=F32)

    if tiles_per_seq > 1:
        @pl.when(i % tiles_per_seq == tiles_per_seq - 1)
        def _():
            nf_ref[i // tiles_per_seq, j] = halo_ref[j, SUBLANES - (CONV_K - 1):, :]

    @pl.when(j == n_ff - 1)
    def _():
        o_ref[...] = _rms_rows(o_ref[...], gf_ref[...])


def _ffn(x2d, g_ffn, w_gate_up_bf16, w_ffn_conv, state, w_down_bf16, g_final, *, tm, seq_len):
    m = x2d.shape[0]
    n_seq = m // seq_len
    tf = FF_TILE
    n_ff = D_FF // tf
    sub_len, n_sub, tiles_per_seq = _seq_geometry(tm, seq_len)
    kernel = functools.partial(_ffn_kernel, tm=tm, seq_len=seq_len, n_ff=n_ff)
    y, new_ffn = pl.pallas_call(
        kernel,
        out_shape=(jax.ShapeDtypeStruct((m, D_MODEL), F32),
                   jax.ShapeDtypeStruct((n_seq, n_ff, CONV_K - 1, tf), F32)),
        grid=(m // tm, n_ff),
        in_specs=[
            pl.BlockSpec((tm, D_MODEL), lambda i, j: (i, 0)),
            pl.BlockSpec((1, D_MODEL), lambda i, j: (0, 0)),
            pl.BlockSpec((D_MODEL, tf), lambda i, j: (0, j)),
            pl.BlockSpec((D_MODEL, tf), lambda i, j: (0, j + n_ff)),
            pl.BlockSpec((CONV_K, tf), lambda i, j: (0, j)),
            pl.BlockSpec((n_sub, SUBLANES, tf), lambda i, j: (i // tiles_per_seq, 0, j)),
            pl.BlockSpec((tf, D_MODEL), lambda i, j: (j, 0)),
            pl.BlockSpec((1, D_MODEL), lambda i, j: (0, 0)),
        ],
        out_specs=(pl.BlockSpec((tm, D_MODEL), lambda i, j: (i, 0)),
                   pl.BlockSpec((n_seq, n_ff, CONV_K - 1, tf), lambda i, j: (0, 0, 0, 0))),
        scratch_shapes=[pltpu.VMEM((tm, D_MODEL), BF16),
                        pltpu.VMEM((n_ff, SUBLANES, tf), F32),
                        pltpu.VMEM((SUBLANES + sub_len, tf), F32),
                        pltpu.VMEM((tm, tf), BF16)],
        compiler_params=_params(("arbitrary", "arbitrary")),
        name="conv_ffn",
    )(x2d, g_ffn.reshape(1, D_MODEL), w_gate_up_bf16, w_gate_up_bf16, w_ffn_conv, _state_rows(state),
      w_down_bf16, g_final.reshape(1, D_MODEL))
    return y, new_ffn.transpose(0, 2, 1, 3).reshape(n_seq, CONV_K - 1, D_FF)


def _tiles(batch, seq_len):
    rows = batch * seq_len
    if rows >= 4096:
        return dict(tm_proj=1024, tm_mix=256, tm_ffn=512, tq=KV_CHUNK)
    return dict(tm_proj=rows, tm_mix=rows, tm_ffn=rows, tq=LANES)


def _trunk(x, cache_k, cache_v, state_conv, state_ffn, weights):
    g_mix, w_in, w_conv, g_att, g_conv, w_o, g_ffn, w_gate_up, w_ffn_conv, w_down, g_final = weights
    b, t, _ = x.shape
    past = cache_k.shape[1]
    tiles = _tiles(b, t)
    tq = tiles["tq"]
    x2d = x.reshape(b * t, D_MODEL)

    proj = _in_proj(x2d, g_mix, w_in, tm=tiles["tm_proj"])
    proj3 = proj.reshape(b, t, IN_WIDTH)
    tq_pad = -(-t // tq) * tq
    tk_pad = -(-(past + tq_pad) // KV_CHUNK) * KV_CHUNK
    if past == 0 and tq_pad == t and tk_pad == t:
        q_h = _to_heads(proj3, 0, permute=False)
        k_h, k = _to_heads(proj3, 1, permute=True, keep_f32=True)
        v_h, v = _to_heads(proj3, 2, permute=True, keep_f32=True)
    else:
        k = proj3[:, :, ATT_WIDTH:2 * ATT_WIDTH]
        v = proj3[:, :, 2 * ATT_WIDTH:3 * ATT_WIDTH]

        def padded_keys(cache, new):
            allk = jnp.concatenate([cache.reshape(b, past, ATT_WIDTH), new], axis=1)
            return jnp.pad(allk, ((0, 0), (0, tk_pad - past - t), (0, 0)))

        q_h = _to_heads(jnp.pad(proj3[:, :, :ATT_WIDTH], ((0, 0), (0, tq_pad - t), (0, 0))), 0, permute=False)
        k_h = _to_heads(padded_keys(cache_k, k), 0, permute=True)
        v_h = _to_heads(padded_keys(cache_v, v), 0, permute=True)
    att = _attention(q_h, k_h, v_h, g_att, past=past, tq=tq)
    att2d = att[:, :t, :].reshape(b * t, ATT_WIDTH)

    x1, new_conv = _mix(att2d, proj, state_conv, w_conv, g_conv, w_o, x2d, tm=tiles["tm_mix"], seq_len=t)
    y, new_ffn = _ffn(x1, g_ffn, w_gate_up, w_ffn_conv, state_ffn, w_down, g_final, tm=tiles["tm_ffn"], seq_len=t)
    shape4 = (b, t, N_HEADS, HEAD_DIM)
    return y.reshape(b, t, D_MODEL), k.reshape(shape4), v.reshape(shape4), new_conv, new_ffn


def kernel(x_prompt, x_sample, cache_k, cache_v, state_conv, state_ffn_conv, g_mix, w_in, w_conv, g_att_out,
           g_conv_out, w_o, g_ffn, w_gate_up, w_ffn_conv, w_down, g_final):
    depth = g_mix.shape[0]
    assert depth == 1
    weights = (g_mix[0], w_in[0].astype(BF16), w_conv[0], g_att_out[0], g_conv_out[0], w_o[0].astype(BF16),
               g_ffn[0], w_gate_up[0].astype(BF16), w_ffn_conv[0], w_down[0].astype(BF16), g_final)
    bp = x_prompt.shape[0]
    dt = x_prompt.dtype
    empty = jnp.zeros((bp, 0, N_HEADS, HEAD_DIM), dt)
    y_p, k_p, v_p, c_p, f_p = _trunk(
        x_prompt, empty, empty, jnp.zeros((bp, CONV_K - 1, CONV_WIDTH), dt), jnp.zeros((bp, CONV_K - 1, D_FF), dt),
        weights)
    y_s, k_s, v_s, c_s, f_s = _trunk(x_sample, cache_k[0], cache_v[0], state_conv[0], state_ffn_conv[0], weights)
    return (y_p, y_s, k_p[None], v_p[None], c_p[None], f_p[None], k_s[None], v_s[None], c_s[None], f_s[None])
```

```python
import functools
import math

import jax
import jax.numpy as jnp
import numpy as np
from jax import lax
from jax.experimental import pallas as pl
from jax.experimental.pallas import tpu as pltpu

F32 = jnp.float32
BF16 = jnp.bfloat16

D_MODEL = 2048
N_HEADS = 8
HEAD_DIM = 128
ATT_WIDTH = N_HEADS * HEAD_DIM
N_GROUPS = 8
GROUP = 128
CONV_WIDTH = N_GROUPS * GROUP
IN_WIDTH = 3 * ATT_WIDTH + 3 * CONV_WIDTH
D_FF = 5632
CONV_K = 3
EPS = 1e-6
Q_SCALE = (HEAD_DIM ** -0.5) * math.log2(math.e)

SUBLANES = 8
LANES = 128
SEGS = 2 * SUBLANES
SEG_LEN = 32
KV_CHUNK = SEGS * SEG_LEN
VMEM_LIMIT_BYTES = 48 * 1024 * 1024

FF_TILE = 512
IN_PROJ_COL_TILE = 1024
assert ATT_WIDTH == IN_PROJ_COL_TILE and CONV_WIDTH == IN_PROJ_COL_TILE


def _params(sem):
    return pltpu.CompilerParams(dimension_semantics=sem, vmem_limit_bytes=VMEM_LIMIT_BYTES)


def _rms_rows(x, g):
    ms = jnp.mean(x * x, axis=-1, keepdims=True)
    return (x * lax.rsqrt(ms + EPS)) * g


def _key_row_permutation():
    p = np.zeros((KV_CHUNK, KV_CHUNK), np.float32)
    for i in range(SEG_LEN):
        for s in range(SEGS):
            p[SEGS * i + s, SEG_LEN * s + SEG_LEN - 1 - i] = 1.0
    return jnp.asarray(p, BF16)


def _in_proj_kernel(x_ref, g_ref, w_ref, o_ref, h_ref):
    j = pl.program_id(1)

    @pl.when(j == 0)
    def _():
        h_ref[...] = _rms_rows(x_ref[...], g_ref[...]).astype(BF16)

    acc = jnp.dot(h_ref[...], w_ref[...], preferred_element_type=F32)
    o_ref[...] = acc * jnp.where(j == 0, Q_SCALE, 1.0).astype(F32)


def _in_proj(x2d, g_mix, w_in_bf16, *, tm):
    m = x2d.shape[0]
    tn = IN_PROJ_COL_TILE
    return pl.pallas_call(
        _in_proj_kernel,
        out_shape=jax.ShapeDtypeStruct((m, IN_WIDTH), F32),
        grid=(m // tm, IN_WIDTH // tn),
        in_specs=[
            pl.BlockSpec((tm, D_MODEL), lambda i, j: (i, 0)),
            pl.BlockSpec((1, D_MODEL), lambda i, j: (0, 0)),
            pl.BlockSpec((D_MODEL, tn), lambda i, j: (0, j)),
        ],
        out_specs=pl.BlockSpec((tm, tn), lambda i, j: (i, j)),
        scratch_shapes=[pltpu.VMEM((tm, D_MODEL), BF16)],
        compiler_params=_params(("arbitrary", "arbitrary")),
        name="in_proj",
    )(x2d, g_mix.reshape(1, D_MODEL), w_in_bf16)


def _to_heads_kernel(p_ref, x_ref, o_ref, *copy_ref, permute, transpose):
    x32 = x_ref[0]
    for ref in copy_ref:
        ref[0] = x32
    if permute:
        x32 = jnp.dot(p_ref[...], x32.astype(BF16), preferred_element_type=F32)
    for h in range(N_HEADS):
        xh = x32[:, h * HEAD_DIM:(h + 1) * HEAD_DIM]
        o_ref[0, h] = (xh.T if transpose else xh).astype(BF16)


def _to_heads(src, col_block, *, permute, transpose, keep_f32=False):
    b, t, _ = src.shape
    rows = KV_CHUNK if permute else math.gcd(t, KV_CHUNK)
    assert t % rows == 0 and rows % LANES == 0
    if transpose:
        out_shape = [jax.ShapeDtypeStruct((b, N_HEADS, HEAD_DIM, t), BF16)]
        out_specs = [pl.BlockSpec((1, N_HEADS, HEAD_DIM, rows), lambda bi, i: (bi, 0, 0, i))]
    else:
        out_shape = [jax.ShapeDtypeStruct((b, N_HEADS, t, HEAD_DIM), BF16)]
        out_specs = [pl.BlockSpec((1, N_HEADS, rows, HEAD_DIM), lambda bi, i: (bi, 0, i, 0))]
    if keep_f32:
        out_shape.append(jax.ShapeDtypeStruct((b, t, ATT_WIDTH), F32))
        out_specs.append(pl.BlockSpec((1, rows, ATT_WIDTH), lambda bi, i: (bi, i, 0)))
    outs = pl.pallas_call(
        functools.partial(_to_heads_kernel, permute=permute, transpose=transpose),
        out_shape=tuple(out_shape),
        grid=(b, t // rows),
        in_specs=[
            pl.BlockSpec((KV_CHUNK, KV_CHUNK), lambda bi, i: (0, 0)),
            pl.BlockSpec((1, rows, ATT_WIDTH), lambda bi, i: (bi, i, col_block)),
        ],
        out_specs=tuple(out_specs),
        compiler_params=_params(("arbitrary", "arbitrary")),
        name="to_heads",
    )(_key_row_permutation(), src)
    heads = outs[0].reshape((b * N_HEADS,) + outs[0].shape[2:])
    return (heads, outs[1]) if keep_f32 else heads


def _softplus2(z):
    e = jnp.exp2(-jnp.abs(z))
    return jnp.maximum(z, 0.0) + jnp.log(1.0 + e) * math.log2(math.e)


def _attn_kernel(qT_ref, k_ref, vT_ref, g_ref, o_ref, z_ref, d_ref, w_ref, acc_ref, carry_ref,
                 *, past, tq, n_chunks_total):
    qi = pl.program_id(1)
    q_lo = past + qi * tq
    j_top = q_lo >> (KV_CHUNK.bit_length() - 1)

    qT = qT_ref[0]
    qpos = q_lo + lax.broadcasted_iota(jnp.int32, (SEGS, tq), 1)
    top_key = j_top * KV_CHUNK + SEG_LEN * lax.broadcasted_iota(jnp.int32, (SEGS, tq), 0) + (SEG_LEN - 1)

    def causal(i):
        return top_key - i < qpos

    def scores(j):
        base = pl.multiple_of(j * KV_CHUNK, KV_CHUNK)
        kc = k_ref[0, pl.ds(base, KV_CHUNK), :]
        return jnp.dot(kc, qT, preferred_element_type=F32)

    def add_values(j):
        base = pl.multiple_of(j * KV_CHUNK, KV_CHUNK)
        vc = vT_ref[0, :, pl.ds(base, KV_CHUNK)]
        acc_ref[...] += jnp.dot(vc, w_ref[...], preferred_element_type=F32)

    def visit(j, cur, masked):
        z_ref[1 - cur] = scores(jnp.maximum(j - 1, 0))
        add_values(jnp.minimum(j + 1, n_chunks_total - 1))
        run = jnp.zeros((SEGS, tq), F32)
        for i in range(SEG_LEN):
            rows = slice(i * SEGS, (i + 1) * SEGS)
            z = z_ref[cur, rows, :]
            sp = _softplus2(z)
            if masked:
                sp = jnp.where(causal(i), sp, 0.0)
            run = run + sp
            d_ref[rows, :] = z - run
        tail = carry_ref[0:1, :]
        offs = [None] * SEGS
        for s in reversed(range(SEGS)):
            offs[s] = tail
            tail = tail + run[s:s + 1, :]
        carry_ref[...] = jnp.broadcast_to(tail, carry_ref.shape)
        off = jnp.concatenate(offs, axis=0)
        for i in range(SEG_LEN):
            rows = slice(i * SEGS, (i + 1) * SEGS)
            w = jnp.exp2(d_ref[rows, :] - off)
            if masked:
                w = jnp.where(causal(i), w, 0.0)
            w_ref[rows, :] = w.astype(BF16)

    acc_ref[...] = jnp.zeros_like(acc_ref)
    carry_ref[...] = jnp.zeros_like(carry_ref)
    w_ref[...] = jnp.zeros_like(w_ref)
    z_ref[0] = scores(j_top)
    visit(j_top, 0, True)

    def pair(t, c):
        j = j_top - 1 - 2 * t
        visit(j, 1, False)
        visit(j - 1, 0, False)
        return c

    lax.fori_loop(0, j_top >> 1, pair, 0)

    @pl.when((j_top & 1) == 1)
    def _():
        visit(0, 1, False)

    add_values(0)

    out = acc_ref[...]
    ms = jnp.mean(out * out, axis=0, keepdims=True)
    y = (out * lax.rsqrt(ms + EPS)) * g_ref[0]
    o_ref[0] = y.T


def _attention(qT, k_perm, vT_perm, g_att, *, past, tq):
    bh_total, _, t_q = qT.shape
    b = bh_total // N_HEADS
    t_k = k_perm.shape[1]
    assert KV_CHUNK % tq == 0 and past % KV_CHUNK == 0 and t_q % tq == 0
    assert t_k % KV_CHUNK == 0 and t_k >= -(-(past + t_q) // KV_CHUNK) * KV_CHUNK
    kernel = functools.partial(_attn_kernel, past=past, tq=tq, n_chunks_total=t_k // KV_CHUNK)
    return pl.pallas_call(
        kernel,
        out_shape=jax.ShapeDtypeStruct((b, t_q, ATT_WIDTH), F32),
        grid=(b * N_HEADS, t_q // tq),
        in_specs=[
            pl.BlockSpec((1, HEAD_DIM, tq), lambda bh, i: (bh, 0, i)),
            pl.BlockSpec((1, t_k, HEAD_DIM), lambda bh, i: (bh, 0, 0)),
            pl.BlockSpec((1, HEAD_DIM, t_k), lambda bh, i: (bh, 0, 0)),
            pl.BlockSpec((1, HEAD_DIM, 1), lambda bh, i: (bh % N_HEADS, 0, 0)),
        ],
        out_specs=pl.BlockSpec((1, tq, HEAD_DIM), lambda bh, i: (bh // N_HEADS, i, bh % N_HEADS)),
        scratch_shapes=[
            pltpu.VMEM((2, KV_CHUNK, tq), F32),
            pltpu.VMEM((KV_CHUNK, tq), F32),
            pltpu.VMEM((KV_CHUNK, tq), BF16),
            pltpu.VMEM((HEAD_DIM, tq), F32),
            pltpu.VMEM((SUBLANES, tq), F32),
        ],
        compiler_params=_params(("arbitrary", "arbitrary")),
        name="sb_attention",
    )(qT, k_perm, vT_perm, g_att.reshape(N_HEADS, HEAD_DIM, 1))


def _dwconv3_rows(cur, w_ref, full_ref, cols):
    n = cur.shape[0]
    full_ref[SUBLANES:SUBLANES + n, cols] = cur
    return (full_ref[SUBLANES - 2:SUBLANES - 2 + n, cols] * w_ref[0:1, cols]
            + full_ref[SUBLANES - 1:SUBLANES - 1 + n, cols] * w_ref[1:2, cols]
            + cur * w_ref[2:3, cols])


def _seq_geometry(tm, seq_len):
    sub_len = min(seq_len, tm)
    return sub_len, tm // sub_len, max(seq_len // tm, 1)


def _mix_kernel(att_ref, b_ref, c_ref, u_ref, cp_ref, up_ref, st_ref, wc_ref, gc_ref, woa_ref, wob_ref,
                x_ref, o_ref, nc_ref, full_ref, yc_ref, *, tm, seq_len):
    i = pl.program_id(0)
    sub_len, n_sub, tiles_per_seq = _seq_geometry(tm, seq_len)
    halo_rows = slice(0, SUBLANES)
    all_cols = slice(0, CONV_WIDTH)
    if tiles_per_seq > 1:
        @pl.when(i % tiles_per_seq == 0)
        def _():
            full_ref[halo_rows, :] = st_ref[0]

        @pl.when(i % tiles_per_seq != 0)
        def _():
            full_ref[halo_rows, :] = cp_ref[...] * up_ref[...]

    out = x_ref[...] + jnp.dot(att_ref[...].astype(BF16), woa_ref[...], preferred_element_type=F32)
    cu = c_ref[...] * u_ref[...]
    for s in range(n_sub):
        rows = slice(s * sub_len, (s + 1) * sub_len)
        if tiles_per_seq == 1:
            full_ref[halo_rows, :] = st_ref[s]
        cur = cu[rows, :]
        y = b_ref[rows, :] * _dwconv3_rows(cur, wc_ref, full_ref, all_cols)
        for g in range(N_GROUPS):
            cols = slice(g * GROUP, (g + 1) * GROUP)
            yc_ref[rows, cols] = _rms_rows(y[:, cols], gc_ref[:, cols]).astype(BF16)
        if tiles_per_seq == 1:
            nc_ref[i * n_sub + s] = cur[sub_len - (CONV_K - 1):, :]

    o_ref[...] = out + jnp.dot(yc_ref[...], wob_ref[...], preferred_element_type=F32)

    if tiles_per_seq > 1:
        @pl.when(i % tiles_per_seq == tiles_per_seq - 1)
        def _():
            last = SUBLANES + sub_len
            nc_ref[i // tiles_per_seq] = full_ref[last - (CONV_K - 1):last, :]


def _state_rows(state):
    return jnp.pad(state, ((0, 0), (SUBLANES - (CONV_K - 1), 0), (0, 0)))


def _mix(att2d, proj, state, w_conv, g_conv, w_o_bf16, x2d, *, tm, seq_len):
    m = x2d.shape[0]
    n_seq = m // seq_len
    sub_len, n_sub, tiles_per_seq = _seq_geometry(tm, seq_len)
    rows_per_block = tm // SUBLANES
    col0 = 3 * ATT_WIDTH // CONV_WIDTH
    prev_map = lambda c: (lambda i: (jnp.maximum(i * rows_per_block - 1, 0), c))
    kernel = functools.partial(_mix_kernel, tm=tm, seq_len=seq_len)
    return pl.pallas_call(
        kernel,
        out_shape=(jax.ShapeDtypeStruct((m, D_MODEL), F32),
                   jax.ShapeDtypeStruct((n_seq, CONV_K - 1, CONV_WIDTH), F32)),
        grid=(m // tm,),
        in_specs=[
            pl.BlockSpec((tm, ATT_WIDTH), lambda i: (i, 0)),
            pl.BlockSpec((tm, CONV_WIDTH), lambda i: (i, col0)),
            pl.BlockSpec((tm, CONV_WIDTH), lambda i: (i, col0 + 1)),
            pl.BlockSpec((tm, CONV_WIDTH), lambda i: (i, col0 + 2)),
            pl.BlockSpec((SUBLANES, CONV_WIDTH), prev_map(col0 + 1)),
            pl.BlockSpec((SUBLANES, CONV_WIDTH), prev_map(col0 + 2)),
            pl.BlockSpec((n_sub, SUBLANES, CONV_WIDTH), lambda i: (i // tiles_per_seq, 0, 0)),
            pl.BlockSpec((CONV_K, CONV_WIDTH), lambda i: (0, 0)),
            pl.BlockSpec((1, CONV_WIDTH), lambda i: (0, 0)),
            pl.BlockSpec((ATT_WIDTH, D_MODEL), lambda i: (0, 0)),
            pl.BlockSpec((CONV_WIDTH, D_MODEL), lambda i: (1, 0)),
            pl.BlockSpec((tm, D_MODEL), lambda i: (i, 0)),
        ],
        out_specs=(pl.BlockSpec((tm, D_MODEL), lambda i: (i, 0)),
                   pl.BlockSpec((n_seq, CONV_K - 1, CONV_WIDTH), lambda i: (0, 0, 0))),
        scratch_shapes=[pltpu.VMEM((SUBLANES + sub_len, CONV_WIDTH), F32),
                        pltpu.VMEM((tm, CONV_WIDTH), BF16)],
        compiler_params=_params(("arbitrary",)),
        name="mix_out_proj",
    )(att2d, proj, proj, proj, proj, proj, _state_rows(state), w_conv, g_conv.reshape(1, CONV_WIDTH),
      w_o_bf16, w_o_bf16, x2d)


def _ffn_kernel(x_ref, g_ref, wg_ref, wu_ref, wc_ref, st_ref, wd_ref, gf_ref, o_ref, nf_ref,
                h_ref, halo_ref, full_ref, act_ref, *, tm, seq_len, n_ff):
    i = pl.program_id(0)
    j = pl.program_id(1)
    sub_len, n_sub, tiles_per_seq = _seq_geometry(tm, seq_len)
    tf = wg_ref.shape[1]
    half = tf // 2

    @pl.when(j == 0)
    def _():
        x = x_ref[...]
        h_ref[...] = _rms_rows(x, g_ref[...]).astype(BF16)
        o_ref[...] = x

    halo_rows = slice(0, SUBLANES)
    if tiles_per_seq > 1:
        @pl.when(i % tiles_per_seq == 0)
        def _():
            halo_ref[j] = st_ref[0]

    h = h_ref[...]
    for c in range(2):
        cols = slice(c * half, (c + 1) * half)
        gate = jnp.dot(h, wg_ref[:, cols], preferred_element_type=F32)
        up = jnp.dot(h, wu_ref[:, cols], preferred_element_type=F32)
        for s in range(n_sub):
            rows = slice(s * sub_len, (s + 1) * sub_len)
            if tiles_per_seq == 1:
                full_ref[halo_rows, cols] = st_ref[s, :, cols]
            else:
                full_ref[halo_rows, cols] = halo_ref[j, :, cols]
            cur = gate[rows, :]
            gc = _dwconv3_rows(cur, wc_ref, full_ref, cols)
            act_ref[rows, cols] = (gc * jax.nn.sigmoid(gc) * up[rows, :]).astype(BF16)
            if tiles_per_seq == 1:
                nf_ref[i * n_sub + s, j, :, cols] = cur[sub_len - (CONV_K - 1):, :]
            else:
                halo_ref[j, :, cols] = cur[sub_len - SUBLANES:, :]

    o_ref[...] += jnp.dot(act_ref[...], wd_ref[...], preferred_element_type=F32)

    if tiles_per_seq > 1:
        @pl.when(i % tiles_per_seq == tiles_per_seq - 1)
        def _():
            nf_ref[i // tiles_per_seq, j] = halo_ref[j, SUBLANES - (CONV_K - 1):, :]

    @pl.when(j == n_ff - 1)
    def _():
        o_ref[...] = _rms_rows(o_ref[...], gf_ref[...])


def _ffn(x2d, g_ffn, w_gate_up_bf16, w_ffn_conv, state, w_down_bf16, g_final, *, tm, seq_len):
    m = x2d.shape[0]
    n_seq = m // seq_len
    tf = FF_TILE
    n_ff = D_FF // tf
    sub_len, n_sub, tiles_per_seq = _seq_geometry(tm, seq_len)
    kernel = functools.partial(_ffn_kernel, tm=tm, seq_len=seq_len, n_ff=n_ff)
    y, new_ffn = pl.pallas_call(
        kernel,
        out_shape=(jax.ShapeDtypeStruct((m, D_MODEL), F32),
                   jax.ShapeDtypeStruct((n_seq, n_ff, CONV_K - 1, tf), F32)),
        grid=(m // tm, n_ff),
        in_specs=[
            pl.BlockSpec((tm, D_MODEL), lambda i, j: (i, 0)),
            pl.BlockSpec((1, D_MODEL), lambda i, j: (0, 0)),
            pl.BlockSpec((D_MODEL, tf), lambda i, j: (0, j)),
            pl.BlockSpec((D_MODEL, tf), lambda i, j: (0, j + n_ff)),
            pl.BlockSpec((CONV_K, tf), lambda i, j: (0, j)),
            pl.BlockSpec((n_sub, SUBLANES, tf), lambda i, j: (i // tiles_per_seq, 0, j)),
            pl.BlockSpec((tf, D_MODEL), lambda i, j: (j, 0)),
            pl.BlockSpec((1, D_MODEL), lambda i, j: (0, 0)),
        ],
        out_specs=(pl.BlockSpec((tm, D_MODEL), lambda i, j: (i, 0)),
                   pl.BlockSpec((n_seq, n_ff, CONV_K - 1, tf), lambda i, j: (0, 0, 0, 0))),
        scratch_shapes=[pltpu.VMEM((tm, D_MODEL), BF16),
                        pltpu.VMEM((n_ff, SUBLANES, tf), F32),
                        pltpu.VMEM((SUBLANES + sub_len, tf), F32),
                        pltpu.VMEM((tm, tf), BF16)],
        compiler_params=_params(("arbitrary", "arbitrary")),
        name="conv_ffn",
    )(x2d, g_ffn.reshape(1, D_MODEL), w_gate_up_bf16, w_gate_up_bf16, w_ffn_conv, _state_rows(state),
      w_down_bf16, g_final.reshape(1, D_MODEL))
    return y, new_ffn.transpose(0, 2, 1, 3).reshape(n_seq, CONV_K - 1, D_FF)


def _tiles(batch, seq_len):
    rows = batch * seq_len
    if rows >= 4096:
        return dict(tm_proj=1024, tm_mix=256, tm_ffn=512, tq=KV_CHUNK)
    return dict(tm_proj=rows, tm_mix=rows, tm_ffn=rows, tq=LANES)


def _trunk(x, cache_k, cache_v, state_conv, state_ffn, weights):
    g_mix, w_in, w_conv, g_att, g_conv, w_o, g_ffn, w_gate_up, w_ffn_conv, w_down, g_final = weights
    b, t, _ = x.shape
    past = cache_k.shape[1]
    tiles = _tiles(b, t)
    tq = tiles["tq"]
    x2d = x.reshape(b * t, D_MODEL)

    proj = _in_proj(x2d, g_mix, w_in, tm=tiles["tm_proj"])
    proj3 = proj.reshape(b, t, IN_WIDTH)
    tq_pad = -(-t // tq) * tq
    tk_pad = -(-(past + tq_pad) // KV_CHUNK) * KV_CHUNK
    if past == 0 and tq_pad == t and tk_pad == t:
        q_h = _to_heads(proj3, 0, permute=False, transpose=True)
        k_h, k = _to_heads(proj3, 1, permute=True, transpose=False, keep_f32=True)
        v_h, v = _to_heads(proj3, 2, permute=True, transpose=True, keep_f32=True)
    else:
        k = proj3[:, :, ATT_WIDTH:2 * ATT_WIDTH]
        v = proj3[:, :, 2 * ATT_WIDTH:3 * ATT_WIDTH]

        def padded_keys(cache, new):
            allk = jnp.concatenate([cache.reshape(b, past, ATT_WIDTH), new], axis=1)
            return jnp.pad(allk, ((0, 0), (0, tk_pad - past - t), (0, 0)))

        q_pad = jnp.pad(proj3[:, :, :ATT_WIDTH], ((0, 0), (0, tq_pad - t), (0, 0)))
        q_h = _to_heads(q_pad, 0, permute=False, transpose=True)
        k_h = _to_heads(padded_keys(cache_k, k), 0, permute=True, transpose=False)
        v_h = _to_heads(padded_keys(cache_v, v), 0, permute=True, transpose=True)
    att = _attention(q_h, k_h, v_h, g_att, past=past, tq=tq)
    att2d = att[:, :t, :].reshape(b * t, ATT_WIDTH)

    x1, new_conv = _mix(att2d, proj, state_conv, w_conv, g_conv, w_o, x2d, tm=tiles["tm_mix"], seq_len=t)
    y, new_ffn = _ffn(x1, g_ffn, w_gate_up, w_ffn_conv, state_ffn, w_down, g_final, tm=tiles["tm_ffn"], seq_len=t)
    shape4 = (b, t, N_HEADS, HEAD_DIM)
    return y.reshape(b, t, D_MODEL), k.reshape(shape4), v.reshape(shape4), new_conv, new_ffn


def kernel(x_prompt, x_sample, cache_k, cache_v, state_conv, state_ffn_conv, g_mix, w_in, w_conv, g_att_out,
           g_conv_out, w_o, g_ffn, w_gate_up, w_ffn_conv, w_down, g_final):
    depth = g_mix.shape[0]
    assert depth == 1
    weights = (g_mix[0], w_in[0].astype(BF16), w_conv[0], g_att_out[0], g_conv_out[0], w_o[0].astype(BF16),
               g_ffn[0], w_gate_up[0].astype(BF16), w_ffn_conv[0], w_down[0].astype(BF16), g_final)
    bp = x_prompt.shape[0]
    dt = x_prompt.dtype
    empty = jnp.zeros((bp, 0, N_HEADS, HEAD_DIM), dt)
    y_p, k_p, v_p, c_p, f_p = _trunk(
        x_prompt, empty, empty, jnp.zeros((bp, CONV_K - 1, CONV_WIDTH), dt), jnp.zeros((bp, CONV_K - 1, D_FF), dt),
        weights)
    y_s, k_s, v_s, c_s, f_s = _trunk(x_sample, cache_k[0], cache_v[0], state_conv[0], state_ffn_conv[0], weights)
    return (y_p, y_s, k_p[None], v_p[None], c_p[None], f_p[None], k_s[None], v_s[None], c_s[None], f_s[None])
```

```python
import functools
import math

import jax
import jax.numpy as jnp
import numpy as np
from jax import lax
from jax.experimental import pallas as pl
from jax.experimental.pallas import tpu as pltpu

F32 = jnp.float32
BF16 = jnp.bfloat16

D_MODEL = 2048
N_HEADS = 8
HEAD_DIM = 128
ATT_WIDTH = N_HEADS * HEAD_DIM
N_GROUPS = 8
GROUP = 128
CONV_WIDTH = N_GROUPS * GROUP
IN_WIDTH = 3 * ATT_WIDTH + 3 * CONV_WIDTH
D_FF = 5632
CONV_K = 3
EPS = 1e-6
Q_SCALE = (HEAD_DIM ** -0.5) * math.log2(math.e)

SUBLANES = 8
LANES = 128
SEGS = 2 * SUBLANES
SEG_LEN = 32
KV_CHUNK = SEGS * SEG_LEN
VMEM_LIMIT_BYTES = 48 * 1024 * 1024

FF_TILE = 512
IN_PROJ_COL_TILE = 1024
assert ATT_WIDTH == IN_PROJ_COL_TILE and CONV_WIDTH == IN_PROJ_COL_TILE


def _params(sem):
    return pltpu.CompilerParams(dimension_semantics=sem, vmem_limit_bytes=VMEM_LIMIT_BYTES)


def _rms_rows(x, g):
    ms = jnp.mean(x * x, axis=-1, keepdims=True)
    return (x * lax.rsqrt(ms + EPS)) * g


def _key_row_permutation():
    p = np.zeros((KV_CHUNK, KV_CHUNK), np.float32)
    for i in range(SEG_LEN):
        for s in range(SEGS):
            p[SEGS * i + s, SEG_LEN * s + SEG_LEN - 1 - i] = 1.0
    return jnp.asarray(p, BF16)


def _in_proj_kernel(x_ref, g_ref, w_ref, o_ref, h_ref):
    j = pl.program_id(1)

    @pl.when(j == 0)
    def _():
        h_ref[...] = _rms_rows(x_ref[...], g_ref[...]).astype(BF16)

    acc = jnp.dot(h_ref[...], w_ref[...], preferred_element_type=F32)
    o_ref[...] = acc * jnp.where(j == 0, Q_SCALE, 1.0).astype(F32)


def _in_proj(x2d, g_mix, w_in_bf16, *, tm):
    m = x2d.shape[0]
    tn = IN_PROJ_COL_TILE
    return pl.pallas_call(
        _in_proj_kernel,
        out_shape=jax.ShapeDtypeStruct((m, IN_WIDTH), F32),
        grid=(m // tm, IN_WIDTH // tn),
        in_specs=[
            pl.BlockSpec((tm, D_MODEL), lambda i, j: (i, 0)),
            pl.BlockSpec((1, D_MODEL), lambda i, j: (0, 0)),
            pl.BlockSpec((D_MODEL, tn), lambda i, j: (0, j)),
        ],
        out_specs=pl.BlockSpec((tm, tn), lambda i, j: (i, j)),
        scratch_shapes=[pltpu.VMEM((tm, D_MODEL), BF16)],
        compiler_params=_params(("arbitrary", "arbitrary")),
        name="in_proj",
    )(x2d, g_mix.reshape(1, D_MODEL), w_in_bf16)


def _to_heads_kernel(p_ref, x_ref, o_ref, *copy_ref, permute, transpose):
    x32 = x_ref[0]
    for ref in copy_ref:
        ref[0] = x32
    if permute:
        x32 = jnp.dot(p_ref[...], x32.astype(BF16), preferred_element_type=F32)
    for h in range(N_HEADS):
        xh = x32[:, h * HEAD_DIM:(h + 1) * HEAD_DIM]
        o_ref[0, h] = (xh.T if transpose else xh).astype(BF16)


def _to_heads(src, col_block, *, permute, transpose, keep_f32=False):
    b, t, _ = src.shape
    rows = KV_CHUNK if permute else math.gcd(t, KV_CHUNK)
    assert t % rows == 0 and rows % LANES == 0
    if transpose:
        out_shape = [jax.ShapeDtypeStruct((b, N_HEADS, HEAD_DIM, t), BF16)]
        out_specs = [pl.BlockSpec((1, N_HEADS, HEAD_DIM, rows), lambda bi, i: (bi, 0, 0, i))]
    else:
        out_shape = [jax.ShapeDtypeStruct((b, N_HEADS, t, HEAD_DIM), BF16)]
        out_specs = [pl.BlockSpec((1, N_HEADS, rows, HEAD_DIM), lambda bi, i: (bi, 0, i, 0))]
    if keep_f32:
        out_shape.append(jax.ShapeDtypeStruct((b, t, ATT_WIDTH), F32))
        out_specs.append(pl.BlockSpec((1, rows, ATT_WIDTH), lambda bi, i: (bi, i, 0)))
    outs = pl.pallas_call(
        functools.partial(_to_heads_kernel, permute=permute, transpose=transpose),
        out_shape=tuple(out_shape),
        grid=(b, t // rows),
        in_specs=[
            pl.BlockSpec((KV_CHUNK, KV_CHUNK), lambda bi, i: (0, 0)),
            pl.BlockSpec((1, rows, ATT_WIDTH), lambda bi, i: (bi, i, col_block)),
        ],
        out_specs=tuple(out_specs),
        compiler_params=_params(("arbitrary", "arbitrary")),
        name="to_heads",
    )(_key_row_permutation(), src)
    heads = outs[0].reshape((b * N_HEADS,) + outs[0].shape[2:])
    return (heads, outs[1]) if keep_f32 else heads


EXP2_UNDERFLOW = 160.0
SCORE_BOUND_SLACK = 1.01


def _softplus2(z):
    e = jnp.exp2(-jnp.abs(z))
    return jnp.maximum(z, 0.0) + jnp.log(1.0 + e) * math.log2(math.e)


def _attn_kernel(qT_ref, k_ref, vT_ref, g_ref, o_ref, z_ref, d_ref, w_ref, acc_ref, carry_ref, kmax_ref,
                 *, past, tq, n_chunks_total):
    qi = pl.program_id(1)
    q_lo = past + qi * tq
    j_top = q_lo >> (KV_CHUNK.bit_length() - 1)

    qT = qT_ref[0]
    qpos = q_lo + lax.broadcasted_iota(jnp.int32, (SEGS, tq), 1)
    top_key = j_top * KV_CHUNK + SEG_LEN * lax.broadcasted_iota(jnp.int32, (SEGS, tq), 0) + (SEG_LEN - 1)

    def causal(i):
        return top_key - i < qpos

    def scores(j):
        base = pl.multiple_of(j * KV_CHUNK, KV_CHUNK)
        kc = k_ref[0, pl.ds(base, KV_CHUNK), :]
        return jnp.dot(kc, qT, preferred_element_type=F32)

    def add_values(j):
        base = pl.multiple_of(j * KV_CHUNK, KV_CHUNK)
        vc = vT_ref[0, :, pl.ds(base, KV_CHUNK)]
        acc_ref[...] += jnp.dot(vc, w_ref[...], preferred_element_type=F32)

    def visit(j, cur, masked):
        z_ref[1 - cur] = scores(jnp.maximum(j - 1, 0))
        add_values(jnp.minimum(j + 1, n_chunks_total - 1))
        run = jnp.zeros((SEGS, tq), F32)
        for i in range(SEG_LEN):
            rows = slice(i * SEGS, (i + 1) * SEGS)
            z = z_ref[cur, rows, :]
            sp = _softplus2(z)
            if masked:
                sp = jnp.where(causal(i), sp, 0.0)
            run = run + sp
            d_ref[rows, :] = z - run
        tail = carry_ref[0:1, :]
        offs = [None] * SEGS
        for s in reversed(range(SEGS)):
            offs[s] = tail
            tail = tail + run[s:s + 1, :]
        carry_ref[...] = jnp.broadcast_to(tail, carry_ref.shape)
        off = jnp.concatenate(offs, axis=0)
        for i in range(SEG_LEN):
            rows = slice(i * SEGS, (i + 1) * SEGS)
            w = jnp.exp2(d_ref[rows, :] - off)
            if masked:
                w = jnp.where(causal(i), w, 0.0)
            w_ref[rows, :] = w.astype(BF16)

    @pl.when(qi == 0)
    def _():
        def chunk_max(c, m):
            kc = k_ref[0, pl.ds(pl.multiple_of(c * KV_CHUNK, KV_CHUNK), KV_CHUNK), :].astype(F32)
            ss = jnp.sum(kc * kc, axis=1, keepdims=True)
            return jnp.maximum(m, jnp.max(ss, axis=0, keepdims=True))

        m = lax.fori_loop(0, n_chunks_total, chunk_max, jnp.zeros((1, 1), F32))
        kmax_ref[...] = jnp.broadcast_to(jnp.sqrt(m), kmax_ref.shape)

    q32 = qT.astype(F32)
    q_norm = jnp.sqrt(jnp.sum(q32 * q32, axis=0, keepdims=True))
    dead_at = q_norm * kmax_ref[0:1, 0:1] * SCORE_BOUND_SLACK + EXP2_UNDERFLOW

    def rest_is_zero():
        return jnp.min(carry_ref[0:1, :] - dead_at) >= 0.0

    acc_ref[...] = jnp.zeros_like(acc_ref)
    carry_ref[...] = jnp.zeros_like(carry_ref)
    w_ref[...] = jnp.zeros_like(w_ref)
    z_ref[0] = scores(j_top)
    visit(j_top, 0, True)

    n_pairs = j_top >> 1

    def more_pairs(c):
        t, done = c
        return jnp.logical_and(t < n_pairs, jnp.logical_not(done))

    def pair(c):
        t, _ = c
        j = j_top - 1 - 2 * t
        visit(j, 1, False)
        visit(j - 1, 0, False)
        return t + 1, rest_is_zero()

    t_end, done = lax.while_loop(more_pairs, pair, (jnp.int32(0), rest_is_zero()))
    odd_visit = jnp.logical_and((j_top & 1) == 1, jnp.logical_not(done))

    @pl.when(odd_visit)
    def _():
        visit(0, 1, False)

    add_values(jnp.where(odd_visit, 0, j_top - 2 * t_end))

    out = acc_ref[...]
    ms = jnp.mean(out * out, axis=0, keepdims=True)
    y = (out * lax.rsqrt(ms + EPS)) * g_ref[0]
    o_ref[0] = y.T


def _attention(qT, k_perm, vT_perm, g_att, *, past, tq):
    bh_total, _, t_q = qT.shape
    b = bh_total // N_HEADS
    t_k = k_perm.shape[1]
    assert KV_CHUNK % tq == 0 and past % KV_CHUNK == 0 and t_q % tq == 0
    assert t_k % KV_CHUNK == 0 and t_k >= -(-(past + t_q) // KV_CHUNK) * KV_CHUNK
    kernel = functools.partial(_attn_kernel, past=past, tq=tq, n_chunks_total=t_k // KV_CHUNK)
    return pl.pallas_call(
        kernel,
        out_shape=jax.ShapeDtypeStruct((b, t_q, ATT_WIDTH), F32),
        grid=(b * N_HEADS, t_q // tq),
        in_specs=[
            pl.BlockSpec((1, HEAD_DIM, tq), lambda bh, i: (bh, 0, i)),
            pl.BlockSpec((1, t_k, HEAD_DIM), lambda bh, i: (bh, 0, 0)),
            pl.BlockSpec((1, HEAD_DIM, t_k), lambda bh, i: (bh, 0, 0)),
            pl.BlockSpec((1, HEAD_DIM, 1), lambda bh, i: (bh % N_HEADS, 0, 0)),
        ],
        out_specs=pl.BlockSpec((1, tq, HEAD_DIM), lambda bh, i: (bh // N_HEADS, i, bh % N_HEADS)),
        scratch_shapes=[
            pltpu.VMEM((2, KV_CHUNK, tq), F32),
            pltpu.VMEM((KV_CHUNK, tq), F32),
            pltpu.VMEM((KV_CHUNK, tq), BF16),
            pltpu.VMEM((HEAD_DIM, tq), F32),
            pltpu.VMEM((SUBLANES, tq), F32),
            pltpu.VMEM((SUBLANES, LANES), F32),
        ],
        compiler_params=_params(("arbitrary", "arbitrary")),
        name="sb_attention",
    )(qT, k_perm, vT_perm, g_att.reshape(N_HEADS, HEAD_DIM, 1))


def _dwconv3_rows(cur, w_ref, full_ref, cols):
    n = cur.shape[0]
    full_ref[SUBLANES:SUBLANES + n, cols] = cur
    return (full_ref[SUBLANES - 2:SUBLANES - 2 + n, cols] * w_ref[0:1, cols]
            + full_ref[SUBLANES - 1:SUBLANES - 1 + n, cols] * w_ref[1:2, cols]
            + cur * w_ref[2:3, cols])


def _seq_geometry(tm, seq_len):
    sub_len = min(seq_len, tm)
    return sub_len, tm // sub_len, max(seq_len // tm, 1)


def _mix_kernel(att_ref, b_ref, c_ref, u_ref, cp_ref, up_ref, st_ref, wc_ref, gc_ref, woa_ref, wob_ref,
                x_ref, o_ref, nc_ref, full_ref, yc_ref, *, tm, seq_len):
    i = pl.program_id(0)
    sub_len, n_sub, tiles_per_seq = _seq_geometry(tm, seq_len)
    halo_rows = slice(0, SUBLANES)
    all_cols = slice(0, CONV_WIDTH)
    if tiles_per_seq > 1:
        @pl.when(i % tiles_per_seq == 0)
        def _():
            full_ref[halo_rows, :] = st_ref[0]

        @pl.when(i % tiles_per_seq != 0)
        def _():
            full_ref[halo_rows, :] = cp_ref[...] * up_ref[...]

    out = x_ref[...] + jnp.dot(att_ref[...].astype(BF16), woa_ref[...], preferred_element_type=F32)
    cu = c_ref[...] * u_ref[...]
    for s in range(n_sub):
        rows = slice(s * sub_len, (s + 1) * sub_len)
        if tiles_per_seq == 1:
            full_ref[halo_rows, :] = st_ref[s]
        cur = cu[rows, :]
        y = b_ref[rows, :] * _dwconv3_rows(cur, wc_ref, full_ref, all_cols)
        for g in range(N_GROUPS):
            cols = slice(g * GROUP, (g + 1) * GROUP)
            yc_ref[rows, cols] = _rms_rows(y[:, cols], gc_ref[:, cols]).astype(BF16)
        if tiles_per_seq == 1:
            nc_ref[i * n_sub + s] = cur[sub_len - (CONV_K - 1):, :]

    o_ref[...] = out + jnp.dot(yc_ref[...], wob_ref[...], preferred_element_type=F32)

    if tiles_per_seq > 1:
        @pl.when(i % tiles_per_seq == tiles_per_seq - 1)
        def _():
            last = SUBLANES + sub_len
            nc_ref[i // tiles_per_seq] = full_ref[last - (CONV_K - 1):last, :]


def _state_rows(state):
    return jnp.pad(state, ((0, 0), (SUBLANES - (CONV_K - 1), 0), (0, 0)))


def _mix(att2d, proj, state, w_conv, g_conv, w_o_bf16, x2d, *, tm, seq_len):
    m = x2d.shape[0]
    n_seq = m // seq_len
    sub_len, n_sub, tiles_per_seq = _seq_geometry(tm, seq_len)
    rows_per_block = tm // SUBLANES
    col0 = 3 * ATT_WIDTH // CONV_WIDTH
    prev_map = lambda c: (lambda i: (jnp.maximum(i * rows_per_block - 1, 0), c))
    kernel = functools.partial(_mix_kernel, tm=tm, seq_len=seq_len)
    return pl.pallas_call(
        kernel,
        out_shape=(jax.ShapeDtypeStruct((m, D_MODEL), F32),
                   jax.ShapeDtypeStruct((n_seq, CONV_K - 1, CONV_WIDTH), F32)),
        grid=(m // tm,),
        in_specs=[
            pl.BlockSpec((tm, ATT_WIDTH), lambda i: (i, 0)),
            pl.BlockSpec((tm, CONV_WIDTH), lambda i: (i, col0)),
            pl.BlockSpec((tm, CONV_WIDTH), lambda i: (i, col0 + 1)),
            pl.BlockSpec((tm, CONV_WIDTH), lambda i: (i, col0 + 2)),
            pl.BlockSpec((SUBLANES, CONV_WIDTH), prev_map(col0 + 1)),
            pl.BlockSpec((SUBLANES, CONV_WIDTH), prev_map(col0 + 2)),
            pl.BlockSpec((n_sub, SUBLANES, CONV_WIDTH), lambda i: (i // tiles_per_seq, 0, 0)),
            pl.BlockSpec((CONV_K, CONV_WIDTH), lambda i: (0, 0)),
            pl.BlockSpec((1, CONV_WIDTH), lambda i: (0, 0)),
            pl.BlockSpec((ATT_WIDTH, D_MODEL), lambda i: (0, 0)),
            pl.BlockSpec((CONV_WIDTH, D_MODEL), lambda i: (1, 0)),
            pl.BlockSpec((tm, D_MODEL), lambda i: (i, 0)),
        ],
        out_specs=(pl.BlockSpec((tm, D_MODEL), lambda i: (i, 0)),
                   pl.BlockSpec((n_seq, CONV_K - 1, CONV_WIDTH), lambda i: (0, 0, 0))),
        scratch_shapes=[pltpu.VMEM((SUBLANES + sub_len, CONV_WIDTH), F32),
                        pltpu.VMEM((tm, CONV_WIDTH), BF16)],
        compiler_params=_params(("arbitrary",)),
        name="mix_out_proj",
    )(att2d, proj, proj, proj, proj, proj, _state_rows(state), w_conv, g_conv.reshape(1, CONV_WIDTH),
      w_o_bf16, w_o_bf16, x2d)


def _ffn_kernel(x_ref, g_ref, wg_ref, wu_ref, wc_ref, st_ref, wd_ref, gf_ref, o_ref, nf_ref,
                h_ref, halo_ref, full_ref, act_ref, *, tm, seq_len, n_ff):
    i = pl.program_id(0)
    j = pl.program_id(1)
    sub_len, n_sub, tiles_per_seq = _seq_geometry(tm, seq_len)
    tf = wg_ref.shape[1]
    half = tf // 2

    @pl.when(j == 0)
    def _():
        x = x_ref[...]
        h_ref[...] = _rms_rows(x, g_ref[...]).astype(BF16)
        o_ref[...] = x

    halo_rows = slice(0, SUBLANES)
    if tiles_per_seq > 1:
        @pl.when(i % tiles_per_seq == 0)
        def _():
            halo_ref[j] = st_ref[0]

    h = h_ref[...]
    for c in range(2):
        cols = slice(c * half, (c + 1) * half)
        gate = jnp.dot(h, wg_ref[:, cols], preferred_element_type=F32)
        up = jnp.dot(h, wu_ref[:, cols], preferred_element_type=F32)
        for s in range(n_sub):
            rows = slice(s * sub_len, (s + 1) * sub_len)
            if tiles_per_seq == 1:
                full_ref[halo_rows, cols] = st_ref[s, :, cols]
            else:
                full_ref[halo_rows, cols] = halo_ref[j, :, cols]
            cur = gate[rows, :]
            gc = _dwconv3_rows(cur, wc_ref, full_ref, cols)
            act_ref[rows, cols] = (gc * jax.nn.sigmoid(gc) * up[rows, :]).astype(BF16)
            if tiles_per_seq == 1:
                nf_ref[i * n_sub + s, j, :, cols] = cur[sub_len - (CONV_K - 1):, :]
            else:
                halo_ref[j, :, cols] = cur[sub_len - SUBLANES:, :]

    o_ref[...] += jnp.dot(act_ref[...], wd_ref[...], preferred_element_type=F32)

    if tiles_per_seq > 1:
        @pl.when(i % tiles_per_seq == tiles_per_seq - 1)
        def _():
            nf_ref[i // tiles_per_seq, j] = halo_ref[j, SUBLANES - (CONV_K - 1):, :]

    @pl.when(j == n_ff - 1)
    def _():
        o_ref[...] = _rms_rows(o_ref[...], gf_ref[...])


def _ffn(x2d, g_ffn, w_gate_up_bf16, w_ffn_conv, state, w_down_bf16, g_final, *, tm, seq_len):
    m = x2d.shape[0]
    n_seq = m // seq_len
    tf = FF_TILE
    n_ff = D_FF // tf
    sub_len, n_sub, tiles_per_seq = _seq_geometry(tm, seq_len)
    kernel = functools.partial(_ffn_kernel, tm=tm, seq_len=seq_len, n_ff=n_ff)
    y, new_ffn = pl.pallas_call(
        kernel,
        out_shape=(jax.ShapeDtypeStruct((m, D_MODEL), F32),
                   jax.ShapeDtypeStruct((n_seq, n_ff, CONV_K - 1, tf), F32)),
        grid=(m // tm, n_ff),
        in_specs=[
            pl.BlockSpec((tm, D_MODEL), lambda i, j: (i, 0)),
            pl.BlockSpec((1, D_MODEL), lambda i, j: (0, 0)),
            pl.BlockSpec((D_MODEL, tf), lambda i, j: (0, j)),
            pl.BlockSpec((D_MODEL, tf), lambda i, j: (0, j + n_ff)),
            pl.BlockSpec((CONV_K, tf), lambda i, j: (0, j)),
            pl.BlockSpec((n_sub, SUBLANES, tf), lambda i, j: (i // tiles_per_seq, 0, j)),
            pl.BlockSpec((tf, D_MODEL), lambda i, j: (j, 0)),
            pl.BlockSpec((1, D_MODEL), lambda i, j: (0, 0)),
        ],
        out_specs=(pl.BlockSpec((tm, D_MODEL), lambda i, j: (i, 0)),
                   pl.BlockSpec((n_seq, n_ff, CONV_K - 1, tf), lambda i, j: (0, 0, 0, 0))),
        scratch_shapes=[pltpu.VMEM((tm, D_MODEL), BF16),
                        pltpu.VMEM((n_ff, SUBLANES, tf), F32),
                        pltpu.VMEM((SUBLANES + sub_len, tf), F32),
                        pltpu.VMEM((tm, tf), BF16)],
        compiler_params=_params(("arbitrary", "arbitrary")),
        name="conv_ffn",
    )(x2d, g_ffn.reshape(1, D_MODEL), w_gate_up_bf16, w_gate_up_bf16, w_ffn_conv, _state_rows(state),
      w_down_bf16, g_final.reshape(1, D_MODEL))
    return y, new_ffn.transpose(0, 2, 1, 3).reshape(n_seq, CONV_K - 1, D_FF)


def _tiles(batch, seq_len):
    rows = batch * seq_len
    if rows >= 4096:
        return dict(tm_proj=1024, tm_mix=256, tm_ffn=512, tq=KV_CHUNK)
    return dict(tm_proj=rows, tm_mix=rows, tm_ffn=rows, tq=LANES)


def _trunk(x, cache_k, cache_v, state_conv, state_ffn, weights):
    g_mix, w_in, w_conv, g_att, g_conv, w_o, g_ffn, w_gate_up, w_ffn_conv, w_down, g_final = weights
    b, t, _ = x.shape
    past = cache_k.shape[1]
    tiles = _tiles(b, t)
    tq = tiles["tq"]
    x2d = x.reshape(b * t, D_MODEL)

    proj = _in_proj(x2d, g_mix, w_in, tm=tiles["tm_proj"])
    proj3 = proj.reshape(b, t, IN_WIDTH)
    tq_pad = -(-t // tq) * tq
    tk_pad = -(-(past + tq_pad) // KV_CHUNK) * KV_CHUNK
    if past == 0 and tq_pad == t and tk_pad == t:
        q_h = _to_heads(proj3, 0, permute=False, transpose=True)
        k_h, k = _to_heads(proj3, 1, permute=True, transpose=False, keep_f32=True)
        v_h, v = _to_heads(proj3, 2, permute=True, transpose=True, keep_f32=True)
    else:
        k = proj3[:, :, ATT_WIDTH:2 * ATT_WIDTH]
        v = proj3[:, :, 2 * ATT_WIDTH:3 * ATT_WIDTH]

        def padded_keys(cache, new):
            allk = jnp.concatenate([cache.reshape(b, past, ATT_WIDTH), new], axis=1)
            return jnp.pad(allk, ((0, 0), (0, tk_pad - past - t), (0, 0)))

        q_pad = jnp.pad(proj3[:, :, :ATT_WIDTH], ((0, 0), (0, tq_pad - t), (0, 0)))
        q_h = _to_heads(q_pad, 0, permute=False, transpose=True)
        k_h = _to_heads(padded_keys(cache_k, k), 0, permute=True, transpose=False)
        v_h = _to_heads(padded_keys(cache_v, v), 0, permute=True, transpose=True)
    att = _attention(q_h, k_h, v_h, g_att, past=past, tq=tq)
    att2d = att[:, :t, :].reshape(b * t, ATT_WIDTH)

    x1, new_conv = _mix(att2d, proj, state_conv, w_conv, g_conv, w_o, x2d, tm=tiles["tm_mix"], seq_len=t)
    y, new_ffn = _ffn(x1, g_ffn, w_gate_up, w_ffn_conv, state_ffn, w_down, g_final, tm=tiles["tm_ffn"], seq_len=t)
    shape4 = (b, t, N_HEADS, HEAD_DIM)
    return y.reshape(b, t, D_MODEL), k.reshape(shape4), v.reshape(shape4), new_conv, new_ffn


def kernel(x_prompt, x_sample, cache_k, cache_v, state_conv, state_ffn_conv, g_mix, w_in, w_conv, g_att_out,
           g_conv_out, w_o, g_ffn, w_gate_up, w_ffn_conv, w_down, g_final):
    depth = g_mix.shape[0]
    assert depth == 1
    weights = (g_mix[0], w_in[0].astype(BF16), w_conv[0], g_att_out[0], g_conv_out[0], w_o[0].astype(BF16),
               g_ffn[0], w_gate_up[0].astype(BF16), w_ffn_conv[0], w_down[0].astype(BF16), g_final)
    bp = x_prompt.shape[0]
    dt = x_prompt.dtype
    empty = jnp.zeros((bp, 0, N_HEADS, HEAD_DIM), dt)
    y_p, k_p, v_p, c_p, f_p = _trunk(
        x_prompt, empty, empty, jnp.zeros((bp, CONV_K - 1, CONV_WIDTH), dt), jnp.zeros((bp, CONV_K - 1, D_FF), dt),
        weights)
    y_s, k_s, v_s, c_s, f_s = _trunk(x_sample, cache_k[0], cache_v[0], state_conv[0], state_ffn_conv[0], weights)
    return (y_p, y_s, k_p[None], v_p[None], c_p[None], f_p[None], k_s[None], v_s[None], c_s[None], f_s[None])
```

```python
import functools
import math

import jax
import jax.numpy as jnp
import numpy as np
from jax import lax
from jax.experimental import pallas as pl
from jax.experimental.pallas import tpu as pltpu

F32 = jnp.float32
BF16 = jnp.bfloat16

D_MODEL = 2048
N_HEADS = 8
HEAD_DIM = 128
ATT_WIDTH = N_HEADS * HEAD_DIM
N_GROUPS = 8
GROUP = 128
CONV_WIDTH = N_GROUPS * GROUP
IN_WIDTH = 3 * ATT_WIDTH + 3 * CONV_WIDTH
D_FF = 5632
CONV_K = 3
EPS = 1e-6
Q_SCALE = (HEAD_DIM ** -0.5) * math.log2(math.e)

SUBLANES = 8
LANES = 128
SEGS = 2 * SUBLANES
SEG_LEN = 32
KV_CHUNK = SEGS * SEG_LEN
VMEM_LIMIT_BYTES = 48 * 1024 * 1024

FF_TILE = 512
IN_PROJ_COL_TILE = 1024
assert ATT_WIDTH == IN_PROJ_COL_TILE and CONV_WIDTH == IN_PROJ_COL_TILE


def _params(sem, vmem_limit_bytes=VMEM_LIMIT_BYTES):
    return pltpu.CompilerParams(dimension_semantics=sem, vmem_limit_bytes=vmem_limit_bytes)


def _rms_rows(x, g):
    ms = jnp.mean(x * x, axis=-1, keepdims=True)
    return (x * lax.rsqrt(ms + EPS)) * g


def _key_row_permutation():
    p = np.zeros((KV_CHUNK, KV_CHUNK), np.float32)
    for i in range(SEG_LEN):
        for s in range(SEGS):
            p[SEGS * i + s, SEG_LEN * s + SEG_LEN - 1 - i] = 1.0
    return jnp.asarray(p, BF16)


def _in_proj_kernel(x_ref, g_ref, w_ref, o_ref, h_ref):
    j = pl.program_id(1)

    @pl.when(j == 0)
    def _():
        h_ref[...] = _rms_rows(x_ref[...], g_ref[...]).astype(BF16)

    acc = jnp.dot(h_ref[...], w_ref[...], preferred_element_type=F32)
    o_ref[...] = acc * jnp.where(j == 0, Q_SCALE, 1.0).astype(F32)


def _in_proj(x2d, g_mix, w_in_bf16, *, tm):
    m = x2d.shape[0]
    tn = IN_PROJ_COL_TILE
    return pl.pallas_call(
        _in_proj_kernel,
        out_shape=jax.ShapeDtypeStruct((m, IN_WIDTH), F32),
        grid=(m // tm, IN_WIDTH // tn),
        in_specs=[
            pl.BlockSpec((tm, D_MODEL), lambda i, j: (i, 0)),
            pl.BlockSpec((1, D_MODEL), lambda i, j: (0, 0)),
            pl.BlockSpec((D_MODEL, tn), lambda i, j: (0, j)),
        ],
        out_specs=pl.BlockSpec((tm, tn), lambda i, j: (i, j)),
        scratch_shapes=[pltpu.VMEM((tm, D_MODEL), BF16)],
        compiler_params=_params(("arbitrary", "arbitrary")),
        name="in_proj",
    )(x2d, g_mix.reshape(1, D_MODEL), w_in_bf16)


def _to_heads_kernel(p_ref, x_ref, o_ref, *copy_ref, permute, transpose):
    x32 = x_ref[0]
    for ref in copy_ref:
        ref[0] = x32
    if permute:
        x32 = jnp.dot(p_ref[...], x32.astype(BF16), preferred_element_type=F32)
    for h in range(N_HEADS):
        xh = x32[:, h * HEAD_DIM:(h + 1) * HEAD_DIM]
        o_ref[0, h] = (xh.T if transpose else xh).astype(BF16)


def _to_heads(src, col_block, *, permute, transpose, keep_f32=False):
    b, t, _ = src.shape
    rows = KV_CHUNK if permute else math.gcd(t, KV_CHUNK)
    assert t % rows == 0 and rows % LANES == 0
    if transpose:
        out_shape = [jax.ShapeDtypeStruct((b, N_HEADS, HEAD_DIM, t), BF16)]
        out_specs = [pl.BlockSpec((1, N_HEADS, HEAD_DIM, rows), lambda bi, i: (bi, 0, 0, i))]
    else:
        out_shape = [jax.ShapeDtypeStruct((b, N_HEADS, t, HEAD_DIM), BF16)]
        out_specs = [pl.BlockSpec((1, N_HEADS, rows, HEAD_DIM), lambda bi, i: (bi, 0, i, 0))]
    if keep_f32:
        out_shape.append(jax.ShapeDtypeStruct((b, t, ATT_WIDTH), F32))
        out_specs.append(pl.BlockSpec((1, rows, ATT_WIDTH), lambda bi, i: (bi, i, 0)))
    outs = pl.pallas_call(
        functools.partial(_to_heads_kernel, permute=permute, transpose=transpose),
        out_shape=tuple(out_shape),
        grid=(b, t // rows),
        in_specs=[
            pl.BlockSpec((KV_CHUNK, KV_CHUNK), lambda bi, i: (0, 0)),
            pl.BlockSpec((1, rows, ATT_WIDTH), lambda bi, i: (bi, i, col_block)),
        ],
        out_specs=tuple(out_specs),
        compiler_params=_params(("arbitrary", "arbitrary")),
        name="to_heads",
    )(_key_row_permutation(), src)
    heads = outs[0].reshape((b * N_HEADS,) + outs[0].shape[2:])
    return (heads, outs[1]) if keep_f32 else heads


EXP2_UNDERFLOW = 160.0
SCORE_BOUND_SLACK = 1.01


def _softplus2(z):
    e = jnp.exp2(-jnp.abs(z))
    return jnp.maximum(z, 0.0) + jnp.log(1.0 + e) * math.log2(math.e)


def _attn_kernel(qT_ref, k_ref, vT_ref, g_ref, o_ref, z_ref, d_ref, w_ref, acc_ref, carry_ref, kmax_ref,
                 *, past, tq, n_chunks_total):
    qi = pl.program_id(1)
    q_lo = past + qi * tq
    j_top = q_lo >> (KV_CHUNK.bit_length() - 1)

    qT = qT_ref[0]
    qpos = q_lo + lax.broadcasted_iota(jnp.int32, (SEGS, tq), 1)
    top_key = j_top * KV_CHUNK + SEG_LEN * lax.broadcasted_iota(jnp.int32, (SEGS, tq), 0) + (SEG_LEN - 1)

    def causal(i):
        return top_key - i < qpos

    def scores(j):
        base = pl.multiple_of(j * KV_CHUNK, KV_CHUNK)
        kc = k_ref[0, pl.ds(base, KV_CHUNK), :]
        return jnp.dot(kc, qT, preferred_element_type=F32)

    def add_values(j):
        base = pl.multiple_of(j * KV_CHUNK, KV_CHUNK)
        vc = vT_ref[0, :, pl.ds(base, KV_CHUNK)]
        acc_ref[...] += jnp.dot(vc, w_ref[...], preferred_element_type=F32)

    def visit(j, cur, masked):
        z_ref[1 - cur] = scores(jnp.maximum(j - 1, 0))
        add_values(jnp.minimum(j + 1, n_chunks_total - 1))
        run = jnp.zeros((SEGS, tq), F32)
        for i in range(SEG_LEN):
            rows = slice(i * SEGS, (i + 1) * SEGS)
            z = z_ref[cur, rows, :]
            sp = _softplus2(z)
            if masked:
                sp = jnp.where(causal(i), sp, 0.0)
            run = run + sp
            d_ref[rows, :] = z - run
        tail = carry_ref[0:1, :]
        offs = [None] * SEGS
        for s in reversed(range(SEGS)):
            offs[s] = tail
            tail = tail + run[s:s + 1, :]
        carry_ref[...] = jnp.broadcast_to(tail, carry_ref.shape)
        off = jnp.concatenate(offs, axis=0)
        for i in range(SEG_LEN):
            rows = slice(i * SEGS, (i + 1) * SEGS)
            w = jnp.exp2(d_ref[rows, :] - off)
            if masked:
                w = jnp.where(causal(i), w, 0.0)
            w_ref[rows, :] = w.astype(BF16)

    @pl.when(qi == 0)
    def _():
        def chunk_max(c, m):
            kc = k_ref[0, pl.ds(pl.multiple_of(c * KV_CHUNK, KV_CHUNK), KV_CHUNK), :].astype(F32)
            ss = jnp.sum(kc * kc, axis=1, keepdims=True)
            return jnp.maximum(m, jnp.max(ss, axis=0, keepdims=True))

        m = lax.fori_loop(0, n_chunks_total, chunk_max, jnp.zeros((1, 1), F32))
        kmax_ref[...] = jnp.broadcast_to(jnp.sqrt(m), kmax_ref.shape)

    q32 = qT.astype(F32)
    q_norm = jnp.sqrt(jnp.sum(q32 * q32, axis=0, keepdims=True))
    dead_at = q_norm * kmax_ref[0:1, 0:1] * SCORE_BOUND_SLACK + EXP2_UNDERFLOW

    def rest_is_zero():
        return jnp.min(carry_ref[0:1, :] - dead_at) >= 0.0

    acc_ref[...] = jnp.zeros_like(acc_ref)
    carry_ref[...] = jnp.zeros_like(carry_ref)
    w_ref[...] = jnp.zeros_like(w_ref)
    z_ref[0] = scores(j_top)
    visit(j_top, 0, True)

    first = jnp.logical_and(j_top >= 1, jnp.logical_not(rest_is_zero()))

    @pl.when(first)
    def _():
        visit(j_top - 1, 1, False)

    n_pairs = jnp.maximum(j_top - 1, 0) >> 1

    def more_pairs(c):
        t, done = c
        return jnp.logical_and(t < n_pairs, jnp.logical_not(done))

    def pair(c):
        t, _ = c
        j = j_top - 2 - 2 * t
        visit(j, 0, False)
        visit(j - 1, 1, False)
        return t + 1, rest_is_zero()

    stop = jnp.logical_or(jnp.logical_not(first), rest_is_zero())
    t_end, done = lax.while_loop(more_pairs, pair, (jnp.int32(0), stop))
    odd_visit = jnp.logical_and(((j_top - 1) & 1) == 1, jnp.logical_not(done))

    @pl.when(odd_visit)
    def _():
        visit(0, 0, False)

    j_last = jnp.where(odd_visit, 0, jnp.where(first, j_top - 1 - 2 * t_end, j_top))
    add_values(j_last)

    out = acc_ref[...]
    ms = jnp.mean(out * out, axis=0, keepdims=True)
    y = (out * lax.rsqrt(ms + EPS)) * g_ref[0]
    o_ref[0] = y.T


def _attention(qT, k_perm, vT_perm, g_att, *, past, tq):
    bh_total, _, t_q = qT.shape
    b = bh_total // N_HEADS
    t_k = k_perm.shape[1]
    assert KV_CHUNK % tq == 0 and past % KV_CHUNK == 0 and t_q % tq == 0
    assert t_k % KV_CHUNK == 0 and t_k >= -(-(past + t_q) // KV_CHUNK) * KV_CHUNK
    kernel = functools.partial(_attn_kernel, past=past, tq=tq, n_chunks_total=t_k // KV_CHUNK)
    return pl.pallas_call(
        kernel,
        out_shape=jax.ShapeDtypeStruct((b, t_q, ATT_WIDTH), F32),
        grid=(b * N_HEADS, t_q // tq),
        in_specs=[
            pl.BlockSpec((1, HEAD_DIM, tq), lambda bh, i: (bh, 0, i)),
            pl.BlockSpec((1, t_k, HEAD_DIM), lambda bh, i: (bh, 0, 0)),
            pl.BlockSpec((1, HEAD_DIM, t_k), lambda bh, i: (bh, 0, 0)),
            pl.BlockSpec((1, HEAD_DIM, 1), lambda bh, i: (bh % N_HEADS, 0, 0)),
        ],
        out_specs=pl.BlockSpec((1, tq, HEAD_DIM), lambda bh, i: (bh // N_HEADS, i, bh % N_HEADS)),
        scratch_shapes=[
            pltpu.VMEM((2, KV_CHUNK, tq), F32),
            pltpu.VMEM((KV_CHUNK, tq), F32),
            pltpu.VMEM((KV_CHUNK, tq), BF16),
            pltpu.VMEM((HEAD_DIM, tq), F32),
            pltpu.VMEM((SUBLANES, tq), F32),
            pltpu.VMEM((SUBLANES, LANES), F32),
        ],
        compiler_params=_params(("arbitrary", "arbitrary")),
        name="sb_attention",
    )(qT, k_perm, vT_perm, g_att.reshape(N_HEADS, HEAD_DIM, 1))


def _dwconv3_rows(cur, w_ref, full_ref, cols):
    n = cur.shape[0]
    full_ref[SUBLANES:SUBLANES + n, cols] = cur
    return (full_ref[SUBLANES - 2:SUBLANES - 2 + n, cols] * w_ref[0:1, cols]
            + full_ref[SUBLANES - 1:SUBLANES - 1 + n, cols] * w_ref[1:2, cols]
            + cur * w_ref[2:3, cols])


def _seq_geometry(tm, seq_len):
    sub_len = min(seq_len, tm)
    return sub_len, tm // sub_len, max(seq_len // tm, 1)


def _mix_kernel(att_ref, b_ref, c_ref, u_ref, cp_ref, up_ref, st_ref, wc_ref, gc_ref, woa_ref, wob_ref,
                x_ref, o_ref, nc_ref, full_ref, yc_ref, *, tm, seq_len):
    i = pl.program_id(0)
    sub_len, n_sub, tiles_per_seq = _seq_geometry(tm, seq_len)
    halo_rows = slice(0, SUBLANES)
    all_cols = slice(0, CONV_WIDTH)
    if tiles_per_seq > 1:
        @pl.when(i % tiles_per_seq == 0)
        def _():
            full_ref[halo_rows, :] = st_ref[0]

        @pl.when(i % tiles_per_seq != 0)
        def _():
            full_ref[halo_rows, :] = cp_ref[...] * up_ref[...]

    out = x_ref[...] + jnp.dot(att_ref[...].astype(BF16), woa_ref[...], preferred_element_type=F32)
    cu = c_ref[...] * u_ref[...]
    for s in range(n_sub):
        rows = slice(s * sub_len, (s + 1) * sub_len)
        if tiles_per_seq == 1:
            full_ref[halo_rows, :] = st_ref[s]
        cur = cu[rows, :]
        y = b_ref[rows, :] * _dwconv3_rows(cur, wc_ref, full_ref, all_cols)
        for g in range(N_GROUPS):
            cols = slice(g * GROUP, (g + 1) * GROUP)
            yc_ref[rows, cols] = _rms_rows(y[:, cols], gc_ref[:, cols]).astype(BF16)
        if tiles_per_seq == 1:
            nc_ref[i * n_sub + s] = cur[sub_len - (CONV_K - 1):, :]

    o_ref[...] = out + jnp.dot(yc_ref[...], wob_ref[...], preferred_element_type=F32)

    if tiles_per_seq > 1:
        @pl.when(i % tiles_per_seq == tiles_per_seq - 1)
        def _():
            last = SUBLANES + sub_len
            nc_ref[i // tiles_per_seq] = full_ref[last - (CONV_K - 1):last, :]


def _state_rows(state):
    return jnp.pad(state, ((0, 0), (SUBLANES - (CONV_K - 1), 0), (0, 0)))


def _mix(att2d, proj, state, w_conv, g_conv, w_o_bf16, x2d, *, tm, seq_len):
    m = x2d.shape[0]
    n_seq = m // seq_len
    sub_len, n_sub, tiles_per_seq = _seq_geometry(tm, seq_len)
    rows_per_block = tm // SUBLANES
    col0 = 3 * ATT_WIDTH // CONV_WIDTH
    prev_map = lambda c: (lambda i: (jnp.maximum(i * rows_per_block - 1, 0), c))
    kernel = functools.partial(_mix_kernel, tm=tm, seq_len=seq_len)
    return pl.pallas_call(
        kernel,
        out_shape=(jax.ShapeDtypeStruct((m, D_MODEL), F32),
                   jax.ShapeDtypeStruct((n_seq, CONV_K - 1, CONV_WIDTH), F32)),
        grid=(m // tm,),
        in_specs=[
            pl.BlockSpec((tm, ATT_WIDTH), lambda i: (i, 0)),
            pl.BlockSpec((tm, CONV_WIDTH), lambda i: (i, col0)),
            pl.BlockSpec((tm, CONV_WIDTH), lambda i: (i, col0 + 1)),
            pl.BlockSpec((tm, CONV_WIDTH), lambda i: (i, col0 + 2)),
            pl.BlockSpec((SUBLANES, CONV_WIDTH), prev_map(col0 + 1)),
            pl.BlockSpec((SUBLANES, CONV_WIDTH), prev_map(col0 + 2)),
            pl.BlockSpec((n_sub, SUBLANES, CONV_WIDTH), lambda i: (i // tiles_per_seq, 0, 0)),
            pl.BlockSpec((CONV_K, CONV_WIDTH), lambda i: (0, 0)),
            pl.BlockSpec((1, CONV_WIDTH), lambda i: (0, 0)),
            pl.BlockSpec((ATT_WIDTH, D_MODEL), lambda i: (0, 0)),
            pl.BlockSpec((CONV_WIDTH, D_MODEL), lambda i: (1, 0)),
            pl.BlockSpec((tm, D_MODEL), lambda i: (i, 0)),
        ],
        out_specs=(pl.BlockSpec((tm, D_MODEL), lambda i: (i, 0)),
                   pl.BlockSpec((n_seq, CONV_K - 1, CONV_WIDTH), lambda i: (0, 0, 0))),
        scratch_shapes=[pltpu.VMEM((SUBLANES + sub_len, CONV_WIDTH), F32),
                        pltpu.VMEM((tm, CONV_WIDTH), BF16)],
        compiler_params=_params(("arbitrary",)),
        name="mix_out_proj",
    )(att2d, proj, proj, proj, proj, proj, _state_rows(state), w_conv, g_conv.reshape(1, CONV_WIDTH),
      w_o_bf16, w_o_bf16, x2d)


def _ffn_kernel(x_ref, g_ref, wg_ref, wu_ref, wc_ref, st_ref, wd_ref, gf_ref, o_ref, nf_ref,
                h_ref, halo_ref, full_ref, act_ref, *, tm, seq_len, n_ff):
    i = pl.program_id(0)
    j = pl.program_id(1)
    sub_len, n_sub, tiles_per_seq = _seq_geometry(tm, seq_len)
    tf = wg_ref.shape[1]
    half = tf // 2

    @pl.when(j == 0)
    def _():
        x = x_ref[...]
        h_ref[...] = _rms_rows(x, g_ref[...]).astype(BF16)
        o_ref[...] = x

    halo_rows = slice(0, SUBLANES)
    if tiles_per_seq > 1:
        @pl.when(i % tiles_per_seq == 0)
        def _():
            halo_ref[j] = st_ref[0]

    h = h_ref[...]
    for c in range(2):
        cols = slice(c * half, (c + 1) * half)
        gate = jnp.dot(h, wg_ref[:, cols], preferred_element_type=F32)
        up = jnp.dot(h, wu_ref[:, cols], preferred_element_type=F32)
        for s in range(n_sub):
            rows = slice(s * sub_len, (s + 1) * sub_len)
            if tiles_per_seq == 1:
                full_ref[halo_rows, cols] = st_ref[s, :, cols]
            else:
                full_ref[halo_rows, cols] = halo_ref[j, :, cols]
            cur = gate[rows, :]
            gc = _dwconv3_rows(cur, wc_ref, full_ref, cols)
            act_ref[rows, cols] = (gc * jax.nn.sigmoid(gc) * up[rows, :]).astype(BF16)
            if tiles_per_seq == 1:
                nf_ref[i * n_sub + s, j, :, cols] = cur[sub_len - (CONV_K - 1):, :]
            else:
                halo_ref[j, :, cols] = cur[sub_len - SUBLANES:, :]

    o_ref[...] += jnp.dot(act_ref[...], wd_ref[...], preferred_element_type=F32)

    if tiles_per_seq > 1:
        @pl.when(i % tiles_per_seq == tiles_per_seq - 1)
        def _():
            nf_ref[i // tiles_per_seq, j] = halo_ref[j, SUBLANES - (CONV_K - 1):, :]

    @pl.when(j == n_ff - 1)
    def _():
        o_ref[...] = _rms_rows(o_ref[...], gf_ref[...])


def _ffn(x2d, g_ffn, w_gate_up_bf16, w_ffn_conv, state, w_down_bf16, g_final, *, tm, seq_len):
    m = x2d.shape[0]
    n_seq = m // seq_len
    tf = FF_TILE
    n_ff = D_FF // tf
    sub_len, n_sub, tiles_per_seq = _seq_geometry(tm, seq_len)
    kernel = functools.partial(_ffn_kernel, tm=tm, seq_len=seq_len, n_ff=n_ff)
    f32_b, bf16_b = 4, 2
    vmem_bytes = (3 * tm * D_MODEL * f32_b + 2 * 3 * D_MODEL * tf * bf16_b + tm * D_MODEL * bf16_b
                  + tm * tf * bf16_b + (SUBLANES + sub_len) * tf * f32_b + 4 * tm * tf * f32_b)
    y, new_ffn = pl.pallas_call(
        kernel,
        out_shape=(jax.ShapeDtypeStruct((m, D_MODEL), F32),
                   jax.ShapeDtypeStruct((n_seq, n_ff, CONV_K - 1, tf), F32)),
        grid=(m // tm, n_ff),
        in_specs=[
            pl.BlockSpec((tm, D_MODEL), lambda i, j: (i, 0), pipeline_mode=pl.Buffered(1)),
            pl.BlockSpec((1, D_MODEL), lambda i, j: (0, 0)),
            pl.BlockSpec((D_MODEL, tf), lambda i, j: (0, j)),
            pl.BlockSpec((D_MODEL, tf), lambda i, j: (0, j + n_ff)),
            pl.BlockSpec((CONV_K, tf), lambda i, j: (0, j)),
            pl.BlockSpec((n_sub, SUBLANES, tf), lambda i, j: (i // tiles_per_seq, 0, j)),
            pl.BlockSpec((tf, D_MODEL), lambda i, j: (j, 0)),
            pl.BlockSpec((1, D_MODEL), lambda i, j: (0, 0)),
        ],
        out_specs=(pl.BlockSpec((tm, D_MODEL), lambda i, j: (i, 0)),
                   pl.BlockSpec((n_seq, n_ff, CONV_K - 1, tf), lambda i, j: (0, 0, 0, 0))),
        scratch_shapes=[pltpu.VMEM((tm, D_MODEL), BF16),
                        pltpu.VMEM((n_ff, SUBLANES, tf), F32),
                        pltpu.VMEM((SUBLANES + sub_len, tf), F32),
                        pltpu.VMEM((tm, tf), BF16)],
        compiler_params=_params(("arbitrary", "arbitrary"), vmem_bytes),
        name="conv_ffn",
    )(x2d, g_ffn.reshape(1, D_MODEL), w_gate_up_bf16, w_gate_up_bf16, w_ffn_conv, _state_rows(state),
      w_down_bf16, g_final.reshape(1, D_MODEL))
    return y, new_ffn.transpose(0, 2, 1, 3).reshape(n_seq, CONV_K - 1, D_FF)


def _tiles(batch, seq_len):
    rows = batch * seq_len
    if rows >= 4096:
        return dict(tm_proj=1024, tm_mix=256, tm_ffn=1024, tq=KV_CHUNK)
    return dict(tm_proj=rows, tm_mix=rows, tm_ffn=rows, tq=LANES)


def _trunk(x, cache_k, cache_v, state_conv, state_ffn, weights):
    g_mix, w_in, w_conv, g_att, g_conv, w_o, g_ffn, w_gate_up, w_ffn_conv, w_down, g_final = weights
    b, t, _ = x.shape
    past = cache_k.shape[1]
    tiles = _tiles(b, t)
    tq = tiles["tq"]
    x2d = x.reshape(b * t, D_MODEL)

    proj = _in_proj(x2d, g_mix, w_in, tm=tiles["tm_proj"])
    proj3 = proj.reshape(b, t, IN_WIDTH)
    tq_pad = -(-t // tq) * tq
    tk_pad = -(-(past + tq_pad) // KV_CHUNK) * KV_CHUNK
    if past == 0 and tq_pad == t and tk_pad == t:
        q_h = _to_heads(proj3, 0, permute=False, transpose=True)
        k_h, k = _to_heads(proj3, 1, permute=True, transpose=False, keep_f32=True)
        v_h, v = _to_heads(proj3, 2, permute=True, transpose=True, keep_f32=True)
    else:
        k = proj3[:, :, ATT_WIDTH:2 * ATT_WIDTH]
        v = proj3[:, :, 2 * ATT_WIDTH:3 * ATT_WIDTH]

        def padded_keys(cache, new):
            allk = jnp.concatenate([cache.reshape(b, past, ATT_WIDTH), new], axis=1)
            return jnp.pad(allk, ((0, 0), (0, tk_pad - past - t), (0, 0)))

        q_pad = jnp.pad(proj3[:, :, :ATT_WIDTH], ((0, 0), (0, tq_pad - t), (0, 0)))
        q_h = _to_heads(q_pad, 0, permute=False, transpose=True)
        k_h = _to_heads(padded_keys(cache_k, k), 0, permute=True, transpose=False)
        v_h = _to_heads(padded_keys(cache_v, v), 0, permute=True, transpose=True)
    att = _attention(q_h, k_h, v_h, g_att, past=past, tq=tq)
    att2d = att[:, :t, :].reshape(b * t, ATT_WIDTH)

    x1, new_conv = _mix(att2d, proj, state_conv, w_conv, g_conv, w_o, x2d, tm=tiles["tm_mix"], seq_len=t)
    y, new_ffn = _ffn(x1, g_ffn, w_gate_up, w_ffn_conv, state_ffn, w_down, g_final, tm=tiles["tm_ffn"], seq_len=t)
    shape4 = (b, t, N_HEADS, HEAD_DIM)
    return y.reshape(b, t, D_MODEL), k.reshape(shape4), v.reshape(shape4), new_conv, new_ffn


def kernel(x_prompt, x_sample, cache_k, cache_v, state_conv, state_ffn_conv, g_mix, w_in, w_conv, g_att_out,
           g_conv_out, w_o, g_ffn, w_gate_up, w_ffn_conv, w_down, g_final):
    depth = g_mix.shape[0]
    assert depth == 1
    weights = (g_mix[0], w_in[0].astype(BF16), w_conv[0], g_att_out[0], g_conv_out[0], w_o[0].astype(BF16),
               g_ffn[0], w_gate_up[0].astype(BF16), w_ffn_conv[0], w_down[0].astype(BF16), g_final)
    bp = x_prompt.shape[0]
    dt = x_prompt.dtype
    empty = jnp.zeros((bp, 0, N_HEADS, HEAD_DIM), dt)
    y_p, k_p, v_p, c_p, f_p = _trunk(
        x_prompt, empty, empty, jnp.zeros((bp, CONV_K - 1, CONV_WIDTH), dt), jnp.zeros((bp, CONV_K - 1, D_FF), dt),
        weights)
    y_s, k_s, v_s, c_s, f_s = _trunk(x_sample, cache_k[0], cache_v[0], state_conv[0], state_ffn_conv[0], weights)
    return (y_p, y_s, k_p[None], v_p[None], c_p[None], f_p[None], k_s[None], v_s[None], c_s[None], f_s[None])
```

```python
import functools
import math

import jax
import jax.numpy as jnp
import numpy as np
from jax import lax
from jax.experimental import pallas as pl
from jax.experimental.pallas import tpu as pltpu

F32 = jnp.float32
BF16 = jnp.bfloat16

D_MODEL = 2048
N_HEADS = 8
HEAD_DIM = 128
ATT_WIDTH = N_HEADS * HEAD_DIM
N_GROUPS = 8
GROUP = 128
CONV_WIDTH = N_GROUPS * GROUP
IN_WIDTH = 3 * ATT_WIDTH + 3 * CONV_WIDTH
D_FF = 5632
CONV_K = 3
EPS = 1e-6
Q_SCALE = (HEAD_DIM ** -0.5) * math.log2(math.e)

SUBLANES = 8
LANES = 128
SEGS = 2 * SUBLANES
SEG_LEN = 32
KV_CHUNK = SEGS * SEG_LEN
VMEM_LIMIT_BYTES = 48 * 1024 * 1024

FF_TILE = 512
IN_PROJ_COL_TILE = 1024
assert ATT_WIDTH == IN_PROJ_COL_TILE and CONV_WIDTH == IN_PROJ_COL_TILE


def _params(sem, vmem_limit_bytes=VMEM_LIMIT_BYTES):
    return pltpu.CompilerParams(dimension_semantics=sem, vmem_limit_bytes=vmem_limit_bytes)


def _rms_rows(x, g):
    ms = jnp.mean(x * x, axis=-1, keepdims=True)
    return (x * lax.rsqrt(ms + EPS)) * g


def _key_row_permutation():
    p = np.zeros((KV_CHUNK, KV_CHUNK), np.float32)
    for i in range(SEG_LEN):
        for s in range(SEGS):
            p[SEGS * i + s, SEG_LEN * s + SEG_LEN - 1 - i] = 1.0
    return jnp.asarray(p, BF16)


def _in_proj_kernel(x_ref, g_ref, w_ref, o_ref, h_ref):
    j = pl.program_id(1)

    @pl.when(j == 0)
    def _():
        h_ref[...] = _rms_rows(x_ref[...], g_ref[...]).astype(BF16)

    acc = jnp.dot(h_ref[...], w_ref[...], preferred_element_type=F32)
    o_ref[...] = acc * jnp.where(j == 0, Q_SCALE, 1.0).astype(F32)


def _in_proj(x2d, g_mix, w_in_bf16, *, tm):
    m = x2d.shape[0]
    tn = IN_PROJ_COL_TILE
    return pl.pallas_call(
        _in_proj_kernel,
        out_shape=jax.ShapeDtypeStruct((m, IN_WIDTH), F32),
        grid=(m // tm, IN_WIDTH // tn),
        in_specs=[
            pl.BlockSpec((tm, D_MODEL), lambda i, j: (i, 0)),
            pl.BlockSpec((1, D_MODEL), lambda i, j: (0, 0)),
            pl.BlockSpec((D_MODEL, tn), lambda i, j: (0, j)),
        ],
        out_specs=pl.BlockSpec((tm, tn), lambda i, j: (i, j)),
        scratch_shapes=[pltpu.VMEM((tm, D_MODEL), BF16)],
        compiler_params=_params(("arbitrary", "arbitrary")),
        name="in_proj",
    )(x2d, g_mix.reshape(1, D_MODEL), w_in_bf16)


def _store_heads(x32, p_ref, o_ref, *, permute, transpose):
    if permute:
        x32 = jnp.dot(p_ref[...], x32.astype(BF16), preferred_element_type=F32)
    for h in range(N_HEADS):
        xh = x32[:, h * HEAD_DIM:(h + 1) * HEAD_DIM]
        o_ref[0, h] = (xh.T if transpose else xh).astype(BF16)


def _to_heads_kernel(p_ref, x_ref, o_ref, *, permute, transpose):
    _store_heads(x_ref[0], p_ref, o_ref, permute=permute, transpose=transpose)


def _qkv_layouts_kernel(p_ref, x_ref, qT_ref, k_ref, vT_ref, k32_ref, v32_ref):
    q32 = x_ref[0, :, 0:ATT_WIDTH]
    k32 = x_ref[0, :, ATT_WIDTH:2 * ATT_WIDTH]
    v32 = x_ref[0, :, 2 * ATT_WIDTH:3 * ATT_WIDTH]
    k32_ref[0] = k32
    v32_ref[0] = v32
    _store_heads(q32, p_ref, qT_ref, permute=False, transpose=True)
    _store_heads(k32, p_ref, k_ref, permute=True, transpose=False)
    _store_heads(v32, p_ref, vT_ref, permute=True, transpose=True)


def _qkv_layouts(proj3):
    b, t, _ = proj3.shape
    rows = KV_CHUNK
    assert t % rows == 0
    heads_t = jax.ShapeDtypeStruct((b, N_HEADS, HEAD_DIM, t), BF16)
    heads = jax.ShapeDtypeStruct((b, N_HEADS, t, HEAD_DIM), BF16)
    plain = jax.ShapeDtypeStruct((b, t, ATT_WIDTH), F32)
    spec_t = pl.BlockSpec((1, N_HEADS, HEAD_DIM, rows), lambda bi, i: (bi, 0, 0, i))
    spec_h = pl.BlockSpec((1, N_HEADS, rows, HEAD_DIM), lambda bi, i: (bi, 0, i, 0))
    spec_p = pl.BlockSpec((1, rows, ATT_WIDTH), lambda bi, i: (bi, i, 0))
    qT, k_h, vT, k32, v32 = pl.pallas_call(
        _qkv_layouts_kernel,
        out_shape=(heads_t, heads, heads_t, plain, plain),
        grid=(b, t // rows),
        in_specs=[
            pl.BlockSpec((KV_CHUNK, KV_CHUNK), lambda bi, i: (0, 0)),
            pl.BlockSpec((1, rows, 3 * ATT_WIDTH), lambda bi, i: (bi, i, 0)),
        ],
        out_specs=(spec_t, spec_h, spec_t, spec_p, spec_p),
        compiler_params=_params(("arbitrary", "arbitrary")),
        name="qkv_layouts",
    )(_key_row_permutation(), proj3)
    merge = lambda a: a.reshape((b * N_HEADS,) + a.shape[2:])
    return merge(qT), merge(k_h), merge(vT), k32, v32


def _to_heads(src, *, permute, transpose):
    b, t, _ = src.shape
    rows = KV_CHUNK if permute else math.gcd(t, KV_CHUNK)
    assert t % rows == 0 and rows % LANES == 0
    if transpose:
        out_shape = jax.ShapeDtypeStruct((b, N_HEADS, HEAD_DIM, t), BF16)
        out_spec = pl.BlockSpec((1, N_HEADS, HEAD_DIM, rows), lambda bi, i: (bi, 0, 0, i))
    else:
        out_shape = jax.ShapeDtypeStruct((b, N_HEADS, t, HEAD_DIM), BF16)
        out_spec = pl.BlockSpec((1, N_HEADS, rows, HEAD_DIM), lambda bi, i: (bi, 0, i, 0))
    out = pl.pallas_call(
        functools.partial(_to_heads_kernel, permute=permute, transpose=transpose),
        out_shape=out_shape,
        grid=(b, t // rows),
        in_specs=[
            pl.BlockSpec((KV_CHUNK, KV_CHUNK), lambda bi, i: (0, 0)),
            pl.BlockSpec((1, rows, ATT_WIDTH), lambda bi, i: (bi, i, 0)),
        ],
        out_specs=out_spec,
        compiler_params=_params(("arbitrary", "arbitrary")),
        name="to_heads",
    )(_key_row_permutation(), src)
    return out.reshape((b * N_HEADS,) + out.shape[2:])


EXP2_UNDERFLOW = 160.0
SCORE_BOUND_SLACK = 1.01


def _softplus2(z):
    e = jnp.exp2(-jnp.abs(z))
    return jnp.maximum(z, 0.0) + jnp.log(1.0 + e) * math.log2(math.e)


def _attn_kernel(qT_ref, k_ref, vT_ref, g_ref, o_ref, z_ref, d_ref, w_ref, acc_ref, carry_ref, kmax_ref,
                 *, past, tq, n_chunks_total):
    qi = pl.program_id(1)
    q_lo = past + qi * tq
    j_top = q_lo >> (KV_CHUNK.bit_length() - 1)

    qT = qT_ref[0]
    qpos = q_lo + lax.broadcasted_iota(jnp.int32, (SEGS, tq), 1)
    top_key = j_top * KV_CHUNK + SEG_LEN * lax.broadcasted_iota(jnp.int32, (SEGS, tq), 0) + (SEG_LEN - 1)

    def causal(i):
        return top_key - i < qpos

    def scores(j):
        base = pl.multiple_of(j * KV_CHUNK, KV_CHUNK)
        kc = k_ref[0, pl.ds(base, KV_CHUNK), :]
        return jnp.dot(kc, qT, preferred_element_type=F32)

    def add_values(j):
        base = pl.multiple_of(j * KV_CHUNK, KV_CHUNK)
        vc = vT_ref[0, :, pl.ds(base, KV_CHUNK)]
        acc_ref[...] += jnp.dot(vc, w_ref[...], preferred_element_type=F32)

    def visit(j, cur, masked):
        z_ref[1 - cur] = scores(jnp.maximum(j - 1, 0))
        if not masked:
            add_values(j + 1)
        run = jnp.zeros((SEGS, tq), F32)
        for i in range(SEG_LEN):
            rows = slice(i * SEGS, (i + 1) * SEGS)
            z = z_ref[cur, rows, :]
            sp = _softplus2(z)
            if masked:
                sp = jnp.where(causal(i), sp, 0.0)
            run = run + sp
            d_ref[rows, :] = z - run
        tail = carry_ref[0:1, :]
        offs = [None] * SEGS
        for s in reversed(range(SEGS)):
            offs[s] = tail
            tail = tail + run[s:s + 1, :]
        carry_ref[...] = jnp.broadcast_to(tail, carry_ref.shape)
        off = jnp.concatenate(offs, axis=0)
        for i in range(SEG_LEN):
            rows = slice(i * SEGS, (i + 1) * SEGS)
            w = jnp.exp2(d_ref[rows, :] - off)
            if masked:
                w = jnp.where(causal(i), w, 0.0)
            w_ref[rows, :] = w.astype(BF16)

    @pl.when(qi == 0)
    def _():
        def chunk_max(c, m):
            kc = k_ref[0, pl.ds(pl.multiple_of(c * KV_CHUNK, KV_CHUNK), KV_CHUNK), :].astype(F32)
            ss = jnp.sum(kc * kc, axis=1, keepdims=True)
            return jnp.maximum(m, jnp.max(ss, axis=0, keepdims=True))

        m = lax.fori_loop(0, n_chunks_total, chunk_max, jnp.zeros((1, 1), F32))
        kmax_ref[...] = jnp.broadcast_to(jnp.sqrt(m), kmax_ref.shape)

    q32 = qT.astype(F32)
    q_norm = jnp.sqrt(jnp.sum(q32 * q32, axis=0, keepdims=True))
    dead_at = q_norm * kmax_ref[0:1, 0:1] * SCORE_BOUND_SLACK + EXP2_UNDERFLOW

    def rest_is_zero():
        return jnp.min(carry_ref[0:1, :] - dead_at) >= 0.0

    acc_ref[...] = jnp.zeros_like(acc_ref)
    carry_ref[...] = jnp.zeros_like(carry_ref)
    z_ref[0] = scores(j_top)
    visit(j_top, 0, True)

    first = jnp.logical_and(j_top >= 1, jnp.logical_not(rest_is_zero()))

    @pl.when(first)
    def _():
        visit(j_top - 1, 1, False)

    n_pairs = jnp.maximum(j_top - 1, 0) >> 1

    def more_pairs(c):
        t, done = c
        return jnp.logical_and(t < n_pairs, jnp.logical_not(done))

    def pair(c):
        t, _ = c
        j = j_top - 2 - 2 * t
        visit(j, 0, False)
        visit(j - 1, 1, False)
        return t + 1, rest_is_zero()

    stop = jnp.logical_or(jnp.logical_not(first), rest_is_zero())
    t_end, done = lax.while_loop(more_pairs, pair, (jnp.int32(0), stop))
    odd_visit = jnp.logical_and(((j_top - 1) & 1) == 1, jnp.logical_not(done))

    @pl.when(odd_visit)
    def _():
        visit(0, 0, False)

    j_last = jnp.where(odd_visit, 0, jnp.where(first, j_top - 1 - 2 * t_end, j_top))
    add_values(j_last)

    out = acc_ref[...]
    ms = jnp.mean(out * out, axis=0, keepdims=True)
    y = (out * lax.rsqrt(ms + EPS)) * g_ref[0]
    o_ref[0] = y.T


def _attention(qT, k_perm, vT_perm, g_att, *, past, tq):
    bh_total, _, t_q = qT.shape
    b = bh_total // N_HEADS
    t_k = k_perm.shape[1]
    assert KV_CHUNK % tq == 0 and past % KV_CHUNK == 0 and t_q % tq == 0
    assert t_k % KV_CHUNK == 0 and t_k >= -(-(past + t_q) // KV_CHUNK) * KV_CHUNK
    kernel = functools.partial(_attn_kernel, past=past, tq=tq, n_chunks_total=t_k // KV_CHUNK)
    return pl.pallas_call(
        kernel,
        out_shape=jax.ShapeDtypeStruct((b, t_q, ATT_WIDTH), F32),
        grid=(b * N_HEADS, t_q // tq),
        in_specs=[
            pl.BlockSpec((1, HEAD_DIM, tq), lambda bh, i: (bh, 0, i)),
            pl.BlockSpec((1, t_k, HEAD_DIM), lambda bh, i: (bh, 0, 0)),
            pl.BlockSpec((1, HEAD_DIM, t_k), lambda bh, i: (bh, 0, 0)),
            pl.BlockSpec((1, HEAD_DIM, 1), lambda bh, i: (bh % N_HEADS, 0, 0)),
        ],
        out_specs=pl.BlockSpec((1, tq, HEAD_DIM), lambda bh, i: (bh // N_HEADS, i, bh % N_HEADS)),
        scratch_shapes=[
            pltpu.VMEM((2, KV_CHUNK, tq), F32),
            pltpu.VMEM((KV_CHUNK, tq), F32),
            pltpu.VMEM((KV_CHUNK, tq), BF16),
            pltpu.VMEM((HEAD_DIM, tq), F32),
            pltpu.VMEM((SUBLANES, tq), F32),
            pltpu.VMEM((SUBLANES, LANES), F32),
        ],
        compiler_params=_params(("arbitrary", "arbitrary")),
        name="sb_attention",
    )(qT, k_perm, vT_perm, g_att.reshape(N_HEADS, HEAD_DIM, 1))


def _dwconv3_rows(cur, w_ref, full_ref, cols):
    n = cur.shape[0]
    full_ref[SUBLANES:SUBLANES + n, cols] = cur
    return (full_ref[SUBLANES - 2:SUBLANES - 2 + n, cols] * w_ref[0:1, cols]
            + full_ref[SUBLANES - 1:SUBLANES - 1 + n, cols] * w_ref[1:2, cols]
            + cur * w_ref[2:3, cols])


def _seq_geometry(tm, seq_len):
    sub_len = min(seq_len, tm)
    return sub_len, tm // sub_len, max(seq_len // tm, 1)


def _mix_kernel(att_ref, b_ref, c_ref, u_ref, cp_ref, up_ref, st_ref, wc_ref, gc_ref, woa_ref, wob_ref,
                x_ref, o_ref, nc_ref, full_ref, yc_ref, *, tm, seq_len):
    i = pl.program_id(0)
    sub_len, n_sub, tiles_per_seq = _seq_geometry(tm, seq_len)
    halo_rows = slice(0, SUBLANES)
    all_cols = slice(0, CONV_WIDTH)
    if tiles_per_seq > 1:
        @pl.when(i % tiles_per_seq == 0)
        def _():
            full_ref[halo_rows, :] = st_ref[0]

        @pl.when(i % tiles_per_seq != 0)
        def _():
            full_ref[halo_rows, :] = cp_ref[...] * up_ref[...]

    out = x_ref[...] + jnp.dot(att_ref[...].astype(BF16), woa_ref[...], preferred_element_type=F32)
    cu = c_ref[...] * u_ref[...]
    for s in range(n_sub):
        rows = slice(s * sub_len, (s + 1) * sub_len)
        if tiles_per_seq == 1:
            full_ref[halo_rows, :] = st_ref[s]
        cur = cu[rows, :]
        y = b_ref[rows, :] * _dwconv3_rows(cur, wc_ref, full_ref, all_cols)
        for g in range(N_GROUPS):
            cols = slice(g * GROUP, (g + 1) * GROUP)
            yc_ref[rows, cols] = _rms_rows(y[:, cols], gc_ref[:, cols]).astype(BF16)
        if tiles_per_seq == 1:
            nc_ref[i * n_sub + s] = cur[sub_len - (CONV_K - 1):, :]

    o_ref[...] = out + jnp.dot(yc_ref[...], wob_ref[...], preferred_element_type=F32)

    if tiles_per_seq > 1:
        @pl.when(i % tiles_per_seq == tiles_per_seq - 1)
        def _():
            last = SUBLANES + sub_len
            nc_ref[i // tiles_per_seq] = full_ref[last - (CONV_K - 1):last, :]


def _state_rows(state):
    return jnp.pad(state, ((0, 0), (SUBLANES - (CONV_K - 1), 0), (0, 0)))


def _mix(att2d, proj, state, w_conv, g_conv, w_o_bf16, x2d, *, tm, seq_len):
    m = x2d.shape[0]
    n_seq = m // seq_len
    sub_len, n_sub, tiles_per_seq = _seq_geometry(tm, seq_len)
    rows_per_block = tm // SUBLANES
    col0 = 3 * ATT_WIDTH // CONV_WIDTH
    prev_map = lambda c: (lambda i: (jnp.maximum(i * rows_per_block - 1, 0), c))
    kernel = functools.partial(_mix_kernel, tm=tm, seq_len=seq_len)
    return pl.pallas_call(
        kernel,
        out_shape=(jax.ShapeDtypeStruct((m, D_MODEL), F32),
                   jax.ShapeDtypeStruct((n_seq, CONV_K - 1, CONV_WIDTH), F32)),
        grid=(m // tm,),
        in_specs=[
            pl.BlockSpec((tm, ATT_WIDTH), lambda i: (i, 0)),
            pl.BlockSpec((tm, CONV_WIDTH), lambda i: (i, col0)),
            pl.BlockSpec((tm, CONV_WIDTH), lambda i: (i, col0 + 1)),
            pl.BlockSpec((tm, CONV_WIDTH), lambda i: (i, col0 + 2)),
            pl.BlockSpec((SUBLANES, CONV_WIDTH), prev_map(col0 + 1)),
            pl.BlockSpec((SUBLANES, CONV_WIDTH), prev_map(col0 + 2)),
            pl.BlockSpec((n_sub, SUBLANES, CONV_WIDTH), lambda i: (i // tiles_per_seq, 0, 0)),
            pl.BlockSpec((CONV_K, CONV_WIDTH), lambda i: (0, 0)),
            pl.BlockSpec((1, CONV_WIDTH), lambda i: (0, 0)),
            pl.BlockSpec((ATT_WIDTH, D_MODEL), lambda i: (0, 0)),
            pl.BlockSpec((CONV_WIDTH, D_MODEL), lambda i: (1, 0)),
            pl.BlockSpec((tm, D_MODEL), lambda i: (i, 0)),
        ],
        out_specs=(pl.BlockSpec((tm, D_MODEL), lambda i: (i, 0)),
                   pl.BlockSpec((n_seq, CONV_K - 1, CONV_WIDTH), lambda i: (0, 0, 0))),
        scratch_shapes=[pltpu.VMEM((SUBLANES + sub_len, CONV_WIDTH), F32),
                        pltpu.VMEM((tm, CONV_WIDTH), BF16)],
        compiler_params=_params(("arbitrary",)),
        name="mix_out_proj",
    )(att2d, proj, proj, proj, proj, proj, _state_rows(state), w_conv, g_conv.reshape(1, CONV_WIDTH),
      w_o_bf16, w_o_bf16, x2d)


def _ffn_kernel(x_ref, g_ref, wg_ref, wu_ref, wc_ref, st_ref, wd_ref, gf_ref, o_ref, nf_ref,
                h_ref, halo_ref, full_ref, act_ref, *, tm, seq_len, n_ff):
    i = pl.program_id(0)
    j = pl.program_id(1)
    sub_len, n_sub, tiles_per_seq = _seq_geometry(tm, seq_len)
    tf = wg_ref.shape[1]
    half = tf // 2

    @pl.when(j == 0)
    def _():
        x = x_ref[...]
        h_ref[...] = _rms_rows(x, g_ref[...]).astype(BF16)
        o_ref[...] = x

    halo_rows = slice(0, SUBLANES)
    if tiles_per_seq > 1:
        @pl.when(i % tiles_per_seq == 0)
        def _():
            halo_ref[j] = st_ref[0]

    h = h_ref[...]
    for c in range(2):
        cols = slice(c * half, (c + 1) * half)
        gate = jnp.dot(h, wg_ref[:, cols], preferred_element_type=F32)
        up = jnp.dot(h, wu_ref[:, cols], preferred_element_type=F32)
        for s in range(n_sub):
            rows = slice(s * sub_len, (s + 1) * sub_len)
            if tiles_per_seq == 1:
                full_ref[halo_rows, cols] = st_ref[s, :, cols]
            else:
                full_ref[halo_rows, cols] = halo_ref[j, :, cols]
            cur = gate[rows, :]
            gc = _dwconv3_rows(cur, wc_ref, full_ref, cols)
            act_ref[rows, cols] = (gc * jax.nn.sigmoid(gc) * up[rows, :]).astype(BF16)
            if tiles_per_seq == 1:
                nf_ref[i * n_sub + s, j, :, cols] = cur[sub_len - (CONV_K - 1):, :]
            else:
                halo_ref[j, :, cols] = cur[sub_len - SUBLANES:, :]

    o_ref[...] += jnp.dot(act_ref[...], wd_ref[...], preferred_element_type=F32)

    if tiles_per_seq > 1:
        @pl.when(i % tiles_per_seq == tiles_per_seq - 1)
        def _():
            nf_ref[i // tiles_per_seq, j] = halo_ref[j, SUBLANES - (CONV_K - 1):, :]

    @pl.when(j == n_ff - 1)
    def _():
        o_ref[...] = _rms_rows(o_ref[...], gf_ref[...])


def _ffn(x2d, g_ffn, w_gate_up_bf16, w_ffn_conv, state, w_down_bf16, g_final, *, tm, seq_len):
    m = x2d.shape[0]
    n_seq = m // seq_len
    tf = FF_TILE
    n_ff = D_FF // tf
    sub_len, n_sub, tiles_per_seq = _seq_geometry(tm, seq_len)
    kernel = functools.partial(_ffn_kernel, tm=tm, seq_len=seq_len, n_ff=n_ff)
    f32_b, bf16_b = 4, 2
    vmem_bytes = (4 * tm * D_MODEL * f32_b + 2 * 3 * D_MODEL * tf * bf16_b + tm * D_MODEL * bf16_b
                  + tm * tf * bf16_b + (SUBLANES + sub_len) * tf * f32_b + 4 * tm * tf * f32_b)
    y, new_ffn = pl.pallas_call(
        kernel,
        out_shape=(jax.ShapeDtypeStruct((m, D_MODEL), F32),
                   jax.ShapeDtypeStruct((n_seq, n_ff, CONV_K - 1, tf), F32)),
        grid=(m // tm, n_ff),
        in_specs=[
            pl.BlockSpec((tm, D_MODEL), lambda i, j: (i, 0)),
            pl.BlockSpec((1, D_MODEL), lambda i, j: (0, 0)),
            pl.BlockSpec((D_MODEL, tf), lambda i, j: (0, j)),
            pl.BlockSpec((D_MODEL, tf), lambda i, j: (0, j + n_ff)),
            pl.BlockSpec((CONV_K, tf), lambda i, j: (0, j)),
            pl.BlockSpec((n_sub, SUBLANES, tf), lambda i, j: (i // tiles_per_seq, 0, j)),
            pl.BlockSpec((tf, D_MODEL), lambda i, j: (j, 0)),
            pl.BlockSpec((1, D_MODEL), lambda i, j: (0, 0)),
        ],
        out_specs=(pl.BlockSpec((tm, D_MODEL), lambda i, j: (i, 0)),
                   pl.BlockSpec((n_seq, n_ff, CONV_K - 1, tf), lambda i, j: (0, 0, 0, 0))),
        scratch_shapes=[pltpu.VMEM((tm, D_MODEL), BF16),
                        pltpu.VMEM((n_ff, SUBLANES, tf), F32),
                        pltpu.VMEM((SUBLANES + sub_len, tf), F32),
                        pltpu.VMEM((tm, tf), BF16)],
        compiler_params=_params(("arbitrary", "arbitrary"), vmem_bytes),
        name="conv_ffn",
    )(x2d, g_ffn.reshape(1, D_MODEL), w_gate_up_bf16, w_gate_up_bf16, w_ffn_conv, _state_rows(state),
      w_down_bf16, g_final.reshape(1, D_MODEL))
    return y, new_ffn.transpose(0, 2, 1, 3).reshape(n_seq, CONV_K - 1, D_FF)


def _tiles(batch, seq_len):
    rows = batch * seq_len
    if rows >= 4096:
        return dict(tm_proj=1024, tm_mix=256, tm_ffn=512, tq=KV_CHUNK)
    return dict(tm_proj=rows, tm_mix=rows, tm_ffn=rows, tq=LANES)


def _trunk(x, cache_k, cache_v, state_conv, state_ffn, weights):
    g_mix, w_in, w_conv, g_att, g_conv, w_o, g_ffn, w_gate_up, w_ffn_conv, w_down, g_final = weights
    b, t, _ = x.shape
    past = cache_k.shape[1]
    tiles = _tiles(b, t)
    tq = tiles["tq"]
    x2d = x.reshape(b * t, D_MODEL)

    proj = _in_proj(x2d, g_mix, w_in, tm=tiles["tm_proj"])
    proj3 = proj.reshape(b, t, IN_WIDTH)
    tq_pad = -(-t // tq) * tq
    tk_pad = -(-(past + tq_pad) // KV_CHUNK) * KV_CHUNK
    if past == 0 and tq_pad == t and tk_pad == t:
        q_h, k_h, v_h, k, v = _qkv_layouts(proj3)
    else:
        k = proj3[:, :, ATT_WIDTH:2 * ATT_WIDTH]
        v = proj3[:, :, 2 * ATT_WIDTH:3 * ATT_WIDTH]

        def padded_keys(cache, new):
            allk = jnp.concatenate([cache.reshape(b, past, ATT_WIDTH), new], axis=1)
            return jnp.pad(allk, ((0, 0), (0, tk_pad - past - t), (0, 0)))

        q_pad = jnp.pad(proj3[:, :, :ATT_WIDTH], ((0, 0), (0, tq_pad - t), (0, 0)))
        q_h = _to_heads(q_pad, permute=False, transpose=True)
        k_h = _to_heads(padded_keys(cache_k, k), permute=True, transpose=False)
        v_h = _to_heads(padded_keys(cache_v, v), permute=True, transpose=True)
    att = _attention(q_h, k_h, v_h, g_att, past=past, tq=tq)
    att2d = att[:, :t, :].reshape(b * t, ATT_WIDTH)

    x1, new_conv = _mix(att2d, proj, state_conv, w_conv, g_conv, w_o, x2d, tm=tiles["tm_mix"], seq_len=t)
    y, new_ffn = _ffn(x1, g_ffn, w_gate_up, w_ffn_conv, state_ffn, w_down, g_final, tm=tiles["tm_ffn"], seq_len=t)
    shape4 = (b, t, N_HEADS, HEAD_DIM)
    return y.reshape(b, t, D_MODEL), k.reshape(shape4), v.reshape(shape4), new_conv, new_ffn


def kernel(x_prompt, x_sample, cache_k, cache_v, state_conv, state_ffn_conv, g_mix, w_in, w_conv, g_att_out,
           g_conv_out, w_o, g_ffn, w_gate_up, w_ffn_conv, w_down, g_final):
    depth = g_mix.shape[0]
    assert depth == 1
    weights = (g_mix[0], w_in[0].astype(BF16), w_conv[0], g_att_out[0], g_conv_out[0], w_o[0].astype(BF16),
               g_ffn[0], w_gate_up[0].astype(BF16), w_ffn_conv[0], w_down[0].astype(BF16), g_final)
    bp = x_prompt.shape[0]
    dt = x_prompt.dtype
    empty = jnp.zeros((bp, 0, N_HEADS, HEAD_DIM), dt)
    y_p, k_p, v_p, c_p, f_p = _trunk(
        x_prompt, empty, empty, jnp.zeros((bp, CONV_K - 1, CONV_WIDTH), dt), jnp.zeros((bp, CONV_K - 1, D_FF), dt),
        weights)
    y_s, k_s, v_s, c_s, f_s = _trunk(x_sample, cache_k[0], cache_v[0], state_conv[0], state_ffn_conv[0], weights)
    return (y_p, y_s, k_p[None], v_p[None], c_p[None], f_p[None], k_s[None], v_s[None], c_s[None], f_s[None])
```

```python
import functools
import math

import jax
import jax.numpy as jnp
import numpy as np
from jax import lax
from jax.experimental import pallas as pl
from jax.experimental.pallas import tpu as pltpu

F32 = jnp.float32
BF16 = jnp.bfloat16

D_MODEL = 2048
N_HEADS = 8
HEAD_DIM = 128
ATT_WIDTH = N_HEADS * HEAD_DIM
N_GROUPS = 8
GROUP = 128
CONV_WIDTH = N_GROUPS * GROUP
IN_WIDTH = 3 * ATT_WIDTH + 3 * CONV_WIDTH
D_FF = 5632
CONV_K = 3
EPS = 1e-6
Q_SCALE = (HEAD_DIM ** -0.5) * math.log2(math.e)

SUBLANES = 8
LANES = 128
SEGS = 2 * SUBLANES
SEG_LEN = 32
KV_CHUNK = SEGS * SEG_LEN
VMEM_LIMIT_BYTES = 48 * 1024 * 1024

FF_TILE = 512
IN_PROJ_COL_TILE = 1024
assert ATT_WIDTH == IN_PROJ_COL_TILE and CONV_WIDTH == IN_PROJ_COL_TILE


def _params(sem, vmem_limit_bytes=VMEM_LIMIT_BYTES):
    return pltpu.CompilerParams(dimension_semantics=sem, vmem_limit_bytes=vmem_limit_bytes)


def _rms_rows(x, g):
    ms = jnp.mean(x * x, axis=-1, keepdims=True)
    return (x * lax.rsqrt(ms + EPS)) * g


def _key_row_permutation():
    p = np.zeros((KV_CHUNK, KV_CHUNK), np.float32)
    for i in range(SEG_LEN):
        for s in range(SEGS):
            p[SEGS * i + s, SEG_LEN * s + SEG_LEN - 1 - i] = 1.0
    return jnp.asarray(p, BF16)


def _in_proj_kernel(x_ref, g_ref, w_ref, o_ref, h_ref):
    j = pl.program_id(1)

    @pl.when(j == 0)
    def _():
        h_ref[...] = _rms_rows(x_ref[...], g_ref[...]).astype(BF16)

    acc = jnp.dot(h_ref[...], w_ref[...], preferred_element_type=F32)
    o_ref[...] = acc * jnp.where(j == 0, Q_SCALE, 1.0).astype(F32)


def _in_proj(x2d, g_mix, w_in_bf16, *, tm):
    m = x2d.shape[0]
    tn = IN_PROJ_COL_TILE
    return pl.pallas_call(
        _in_proj_kernel,
        out_shape=jax.ShapeDtypeStruct((m, IN_WIDTH), F32),
        grid=(m // tm, IN_WIDTH // tn),
        in_specs=[
            pl.BlockSpec((tm, D_MODEL), lambda i, j: (i, 0)),
            pl.BlockSpec((1, D_MODEL), lambda i, j: (0, 0)),
            pl.BlockSpec((D_MODEL, tn), lambda i, j: (0, j)),
        ],
        out_specs=pl.BlockSpec((tm, tn), lambda i, j: (i, j)),
        scratch_shapes=[pltpu.VMEM((tm, D_MODEL), BF16)],
        compiler_params=_params(("arbitrary", "arbitrary")),
        name="in_proj",
    )(x2d, g_mix.reshape(1, D_MODEL), w_in_bf16)


def _store_heads(x32, p_ref, o_ref, *, permute, transpose):
    if permute:
        x32 = jnp.dot(p_ref[...], x32.astype(BF16), preferred_element_type=F32)
    for h in range(N_HEADS):
        xh = x32[:, h * HEAD_DIM:(h + 1) * HEAD_DIM]
        o_ref[0, h] = (xh.T if transpose else xh).astype(BF16)


def _to_heads_kernel(p_ref, x_ref, tail_ref, o_ref, *, permute, transpose, n_src_blocks):
    x32 = jnp.where(pl.program_id(1) < n_src_blocks, x_ref[0], tail_ref[0])
    _store_heads(x32, p_ref, o_ref, permute=permute, transpose=transpose)


def _qkv_layouts_kernel(p_ref, x_ref, qT_ref, k_ref, vT_ref, k32_ref, v32_ref):
    q32 = x_ref[0, :, 0:ATT_WIDTH]
    k32 = x_ref[0, :, ATT_WIDTH:2 * ATT_WIDTH]
    v32 = x_ref[0, :, 2 * ATT_WIDTH:3 * ATT_WIDTH]
    k32_ref[0] = k32.reshape(k32.shape[0], N_HEADS, HEAD_DIM)
    v32_ref[0] = v32.reshape(v32.shape[0], N_HEADS, HEAD_DIM)
    _store_heads(q32, p_ref, qT_ref, permute=False, transpose=True)
    _store_heads(k32, p_ref, k_ref, permute=True, transpose=False)
    _store_heads(v32, p_ref, vT_ref, permute=True, transpose=True)


def _qkv_layouts(proj3):
    b, t, _ = proj3.shape
    rows = KV_CHUNK
    assert t % rows == 0
    heads_t = jax.ShapeDtypeStruct((b, N_HEADS, HEAD_DIM, t), BF16)
    heads = jax.ShapeDtypeStruct((b, N_HEADS, t, HEAD_DIM), BF16)
    plain = jax.ShapeDtypeStruct((b, t, N_HEADS, HEAD_DIM), F32)
    spec_t = pl.BlockSpec((1, N_HEADS, HEAD_DIM, rows), lambda bi, i: (bi, 0, 0, i))
    spec_h = pl.BlockSpec((1, N_HEADS, rows, HEAD_DIM), lambda bi, i: (bi, 0, i, 0))
    spec_p = pl.BlockSpec((1, rows, N_HEADS, HEAD_DIM), lambda bi, i: (bi, i, 0, 0))
    qT, k_h, vT, k32, v32 = pl.pallas_call(
        _qkv_layouts_kernel,
        out_shape=(heads_t, heads, heads_t, plain, plain),
        grid=(b, t // rows),
        in_specs=[
            pl.BlockSpec((KV_CHUNK, KV_CHUNK), lambda bi, i: (0, 0)),
            pl.BlockSpec((1, rows, 3 * ATT_WIDTH), lambda bi, i: (bi, i, 0)),
        ],
        out_specs=(spec_t, spec_h, spec_t, spec_p, spec_p),
        compiler_params=_params(("arbitrary", "arbitrary")),
        name="qkv_layouts",
    )(_key_row_permutation(), proj3)
    merge = lambda a: a.reshape((b * N_HEADS,) + a.shape[2:])
    return merge(qT), merge(k_h), merge(vT), k32, v32


def _to_heads(src, tail, *, permute, transpose):
    b, t_src, _ = src.shape
    t = t_src + tail.shape[1]
    rows = KV_CHUNK if permute else math.gcd(math.gcd(t_src, tail.shape[1]), KV_CHUNK)
    assert t_src % rows == 0 and tail.shape[1] % rows == 0 and rows % LANES == 0
    n_src = t_src // rows
    if n_src == 0:
        src = tail
    if transpose:
        out_shape = jax.ShapeDtypeStruct((b, N_HEADS, HEAD_DIM, t), BF16)
        out_spec = pl.BlockSpec((1, N_HEADS, HEAD_DIM, rows), lambda bi, i: (bi, 0, 0, i))
    else:
        out_shape = jax.ShapeDtypeStruct((b, N_HEADS, t, HEAD_DIM), BF16)
        out_spec = pl.BlockSpec((1, N_HEADS, rows, HEAD_DIM), lambda bi, i: (bi, 0, i, 0))
    out = pl.pallas_call(
        functools.partial(_to_heads_kernel, permute=permute, transpose=transpose, n_src_blocks=n_src),
        out_shape=out_shape,
        grid=(b, t // rows),
        in_specs=[
            pl.BlockSpec((KV_CHUNK, KV_CHUNK), lambda bi, i: (0, 0)),
            pl.BlockSpec((1, rows, ATT_WIDTH), lambda bi, i: (bi, jnp.minimum(i, max(n_src - 1, 0)), 0)),
            pl.BlockSpec((1, rows, ATT_WIDTH), lambda bi, i: (bi, jnp.maximum(i - n_src, 0), 0)),
        ],
        out_specs=out_spec,
        compiler_params=_params(("arbitrary", "arbitrary")),
        name="to_heads",
    )(_key_row_permutation(), src, tail)
    return out.reshape((b * N_HEADS,) + out.shape[2:])


EXP2_UNDERFLOW = 160.0
SCORE_BOUND_SLACK = 1.01


def _softplus2(z):
    e = jnp.exp2(-jnp.abs(z))
    return jnp.maximum(z, 0.0) + jnp.log(1.0 + e) * math.log2(math.e)


def _attn_kernel(qT_ref, k_ref, vT_ref, g_ref, o_ref, z_ref, d_ref, w_ref, acc_ref, carry_ref, kmax_ref,
                 *, past, tq, n_chunks_total):
    qi = pl.program_id(1)
    q_lo = past + qi * tq
    j_top = q_lo >> (KV_CHUNK.bit_length() - 1)

    qT = qT_ref[0]
    qpos = q_lo + lax.broadcasted_iota(jnp.int32, (SEGS, tq), 1)
    top_key = j_top * KV_CHUNK + SEG_LEN * lax.broadcasted_iota(jnp.int32, (SEGS, tq), 0) + (SEG_LEN - 1)

    def causal(i):
        return top_key - i < qpos

    def scores(j):
        base = pl.multiple_of(j * KV_CHUNK, KV_CHUNK)
        kc = k_ref[0, pl.ds(base, KV_CHUNK), :]
        return jnp.dot(kc, qT, preferred_element_type=F32)

    def add_values(j):
        base = pl.multiple_of(j * KV_CHUNK, KV_CHUNK)
        vc = vT_ref[0, :, pl.ds(base, KV_CHUNK)]
        acc_ref[...] += jnp.dot(vc, w_ref[...], preferred_element_type=F32)

    def visit(j, cur, masked):
        z_ref[1 - cur] = scores(jnp.maximum(j - 1, 0))
        if not masked:
            add_values(j + 1)
        run = jnp.zeros((SEGS, tq), F32)
        for i in range(SEG_LEN):
            rows = slice(i * SEGS, (i + 1) * SEGS)
            z = z_ref[cur, rows, :]
            sp = _softplus2(z)
            if masked:
                sp = jnp.where(causal(i), sp, 0.0)
            run = run + sp
            d_ref[rows, :] = z - run
        tail = carry_ref[0:1, :]
        offs = [None] * SEGS
        for s in reversed(range(SEGS)):
            offs[s] = tail
            tail = tail + run[s:s + 1, :]
        carry_ref[...] = jnp.broadcast_to(tail, carry_ref.shape)
        off = jnp.concatenate(offs, axis=0)
        for i in range(SEG_LEN):
            rows = slice(i * SEGS, (i + 1) * SEGS)
            w = jnp.exp2(d_ref[rows, :] - off)
            if masked:
                w = jnp.where(causal(i), w, 0.0)
            w_ref[rows, :] = w.astype(BF16)

    @pl.when(qi == 0)
    def _():
        def chunk_max(c, m):
            kc = k_ref[0, pl.ds(pl.multiple_of(c * KV_CHUNK, KV_CHUNK), KV_CHUNK), :].astype(F32)
            ss = jnp.sum(kc * kc, axis=1, keepdims=True)
            return jnp.maximum(m, jnp.max(ss, axis=0, keepdims=True))

        m = lax.fori_loop(0, n_chunks_total, chunk_max, jnp.zeros((1, 1), F32))
        kmax_ref[...] = jnp.broadcast_to(jnp.sqrt(m), kmax_ref.shape)

    q32 = qT.astype(F32)
    q_norm = jnp.sqrt(jnp.sum(q32 * q32, axis=0, keepdims=True))
    dead_at = q_norm * kmax_ref[0:1, 0:1] * SCORE_BOUND_SLACK + EXP2_UNDERFLOW

    def rest_is_zero():
        return jnp.min(carry_ref[0:1, :] - dead_at) >= 0.0

    acc_ref[...] = jnp.zeros_like(acc_ref)
    carry_ref[...] = jnp.zeros_like(carry_ref)
    z_ref[0] = scores(j_top)
    visit(j_top, 0, True)

    first = jnp.logical_and(j_top >= 1, jnp.logical_not(rest_is_zero()))

    @pl.when(first)
    def _():
        visit(j_top - 1, 1, False)

    n_pairs = jnp.maximum(j_top - 1, 0) >> 1

    def more_pairs(c):
        t, done = c
        return jnp.logical_and(t < n_pairs, jnp.logical_not(done))

    def pair(c):
        t, _ = c
        j = j_top - 2 - 2 * t
        visit(j, 0, False)
        visit(j - 1, 1, False)
        return t + 1, rest_is_zero()

    stop = jnp.logical_or(jnp.logical_not(first), rest_is_zero())
    t_end, done = lax.while_loop(more_pairs, pair, (jnp.int32(0), stop))
    odd_visit = jnp.logical_and(((j_top - 1) & 1) == 1, jnp.logical_not(done))

    @pl.when(odd_visit)
    def _():
        visit(0, 0, False)

    j_last = jnp.where(odd_visit, 0, jnp.where(first, j_top - 1 - 2 * t_end, j_top))
    add_values(j_last)

    out = acc_ref[...]
    ms = jnp.mean(out * out, axis=0, keepdims=True)
    y = (out * lax.rsqrt(ms + EPS)) * g_ref[0]
    o_ref[0] = y.T


def _attention(qT, k_perm, vT_perm, g_att, *, past, tq):
    bh_total, _, t_q = qT.shape
    b = bh_total // N_HEADS
    t_k = k_perm.shape[1]
    assert KV_CHUNK % tq == 0 and past % KV_CHUNK == 0 and t_q % tq == 0
    assert t_k % KV_CHUNK == 0 and t_k >= -(-(past + t_q) // KV_CHUNK) * KV_CHUNK
    kernel = functools.partial(_attn_kernel, past=past, tq=tq, n_chunks_total=t_k // KV_CHUNK)
    return pl.pallas_call(
        kernel,
        out_shape=jax.ShapeDtypeStruct((b, t_q, ATT_WIDTH), F32),
        grid=(b * N_HEADS, t_q // tq),
        in_specs=[
            pl.BlockSpec((1, HEAD_DIM, tq), lambda bh, i: (bh, 0, i)),
            pl.BlockSpec((1, t_k, HEAD_DIM), lambda bh, i: (bh, 0, 0)),
            pl.BlockSpec((1, HEAD_DIM, t_k), lambda bh, i: (bh, 0, 0)),
            pl.BlockSpec((1, HEAD_DIM, 1), lambda bh, i: (bh % N_HEADS, 0, 0)),
        ],
        out_specs=pl.BlockSpec((1, tq, HEAD_DIM), lambda bh, i: (bh // N_HEADS, i, bh % N_HEADS)),
        scratch_shapes=[
            pltpu.VMEM((2, KV_CHUNK, tq), F32),
            pltpu.VMEM((KV_CHUNK, tq), F32),
            pltpu.VMEM((KV_CHUNK, tq), BF16),
            pltpu.VMEM((HEAD_DIM, tq), F32),
            pltpu.VMEM((SUBLANES, tq), F32),
            pltpu.VMEM((SUBLANES, LANES), F32),
        ],
        compiler_params=_params(("arbitrary", "arbitrary")),
        name="sb_attention",
    )(qT, k_perm, vT_perm, g_att.reshape(N_HEADS, HEAD_DIM, 1))


def _dwconv3_rows(cur, w_ref, full_ref, cols):
    n = cur.shape[0]
    full_ref[SUBLANES:SUBLANES + n, cols] = cur
    return (full_ref[SUBLANES - 2:SUBLANES - 2 + n, cols] * w_ref[0:1, cols]
            + full_ref[SUBLANES - 1:SUBLANES - 1 + n, cols] * w_ref[1:2, cols]
            + cur * w_ref[2:3, cols])


def _seq_geometry(tm, seq_len):
    sub_len = min(seq_len, tm)
    return sub_len, tm // sub_len, max(seq_len // tm, 1)


def _mix_kernel(att_ref, b_ref, c_ref, u_ref, cp_ref, up_ref, st_ref, wc_ref, gc_ref, woa_ref, wob_ref,
                x_ref, o_ref, nc_ref, full_ref, yc_ref, *, tm, seq_len):
    i = pl.program_id(0)
    sub_len, n_sub, tiles_per_seq = _seq_geometry(tm, seq_len)
    halo_rows = slice(0, SUBLANES)
    all_cols = slice(0, CONV_WIDTH)
    if tiles_per_seq > 1:
        @pl.when(i % tiles_per_seq == 0)
        def _():
            full_ref[halo_rows, :] = st_ref[0]

        @pl.when(i % tiles_per_seq != 0)
        def _():
            full_ref[halo_rows, :] = cp_ref[...] * up_ref[...]

    out = x_ref[...] + jnp.dot(att_ref[...].astype(BF16), woa_ref[...], preferred_element_type=F32)
    cu = c_ref[...] * u_ref[...]
    for s in range(n_sub):
        rows = slice(s * sub_len, (s + 1) * sub_len)
        if tiles_per_seq == 1:
            full_ref[halo_rows, :] = st_ref[s]
        cur = cu[rows, :]
        y = b_ref[rows, :] * _dwconv3_rows(cur, wc_ref, full_ref, all_cols)
        for g in range(N_GROUPS):
            cols = slice(g * GROUP, (g + 1) * GROUP)
            yc_ref[rows, cols] = _rms_rows(y[:, cols], gc_ref[:, cols]).astype(BF16)
        if tiles_per_seq == 1:
            nc_ref[i * n_sub + s] = cur[sub_len - (CONV_K - 1):, :]

    o_ref[...] = out + jnp.dot(yc_ref[...], wob_ref[...], preferred_element_type=F32)

    if tiles_per_seq > 1:
        @pl.when(i % tiles_per_seq == tiles_per_seq - 1)
        def _():
            last = SUBLANES + sub_len
            nc_ref[i // tiles_per_seq] = full_ref[last - (CONV_K - 1):last, :]


def _state_rows(state):
    return jnp.pad(state, ((0, 0), (SUBLANES - (CONV_K - 1), 0), (0, 0)))


def _mix(att2d, proj, state, w_conv, g_conv, w_o_bf16, x2d, *, tm, seq_len):
    m = x2d.shape[0]
    n_seq = m // seq_len
    sub_len, n_sub, tiles_per_seq = _seq_geometry(tm, seq_len)
    rows_per_block = tm // SUBLANES
    col0 = 3 * ATT_WIDTH // CONV_WIDTH
    prev_map = lambda c: (lambda i: (jnp.maximum(i * rows_per_block - 1, 0), c))
    kernel = functools.partial(_mix_kernel, tm=tm, seq_len=seq_len)
    return pl.pallas_call(
        kernel,
        out_shape=(jax.ShapeDtypeStruct((m, D_MODEL), F32),
                   jax.ShapeDtypeStruct((n_seq, CONV_K - 1, CONV_WIDTH), F32)),
        grid=(m // tm,),
        in_specs=[
            pl.BlockSpec((tm, ATT_WIDTH), lambda i: (i, 0)),
            pl.BlockSpec((tm, CONV_WIDTH), lambda i: (i, col0)),
            pl.BlockSpec((tm, CONV_WIDTH), lambda i: (i, col0 + 1)),
            pl.BlockSpec((tm, CONV_WIDTH), lambda i: (i, col0 + 2)),
            pl.BlockSpec((SUBLANES, CONV_WIDTH), prev_map(col0 + 1)),
            pl.BlockSpec((SUBLANES, CONV_WIDTH), prev_map(col0 + 2)),
            pl.BlockSpec((n_sub, SUBLANES, CONV_WIDTH), lambda i: (i // tiles_per_seq, 0, 0)),
            pl.BlockSpec((CONV_K, CONV_WIDTH), lambda i: (0, 0)),
            pl.BlockSpec((1, CONV_WIDTH), lambda i: (0, 0)),
            pl.BlockSpec((ATT_WIDTH, D_MODEL), lambda i: (0, 0)),
            pl.BlockSpec((CONV_WIDTH, D_MODEL), lambda i: (1, 0)),
            pl.BlockSpec((tm, D_MODEL), lambda i: (i, 0)),
        ],
        out_specs=(pl.BlockSpec((tm, D_MODEL), lambda i: (i, 0)),
                   pl.BlockSpec((n_seq, CONV_K - 1, CONV_WIDTH), lambda i: (0, 0, 0))),
        scratch_shapes=[pltpu.VMEM((SUBLANES + sub_len, CONV_WIDTH), F32),
                        pltpu.VMEM((tm, CONV_WIDTH), BF16)],
        compiler_params=_params(("arbitrary",)),
        name="mix_out_proj",
    )(att2d, proj, proj, proj, proj, proj, _state_rows(state), w_conv, g_conv.reshape(1, CONV_WIDTH),
      w_o_bf16, w_o_bf16, x2d)


def _ffn_kernel(x_ref, g_ref, wg_ref, wu_ref, wc_ref, st_ref, wd_ref, gf_ref, o_ref, nf_ref,
                h_ref, halo_ref, full_ref, act_ref, *, tm, seq_len, n_ff):
    i = pl.program_id(0)
    j = pl.program_id(1)
    sub_len, n_sub, tiles_per_seq = _seq_geometry(tm, seq_len)
    tf = wg_ref.shape[1]
    half = tf // 2

    @pl.when(j == 0)
    def _():
        x = x_ref[...]
        h_ref[...] = _rms_rows(x, g_ref[...]).astype(BF16)
        o_ref[...] = x

    halo_rows = slice(0, SUBLANES)
    if tiles_per_seq > 1:
        @pl.when(i % tiles_per_seq == 0)
        def _():
            halo_ref[j] = st_ref[0]

    h = h_ref[...]
    for c in range(2):
        cols = slice(c * half, (c + 1) * half)
        gate = jnp.dot(h, wg_ref[:, cols], preferred_element_type=F32)
        up = jnp.dot(h, wu_ref[:, cols], preferred_element_type=F32)
        for s in range(n_sub):
            rows = slice(s * sub_len, (s + 1) * sub_len)
            if tiles_per_seq == 1:
                full_ref[halo_rows, cols] = st_ref[s, :, cols]
            else:
                full_ref[halo_rows, cols] = halo_ref[j, :, cols]
            cur = gate[rows, :]
            gc = _dwconv3_rows(cur, wc_ref, full_ref, cols)
            act_ref[rows, cols] = (gc * jax.nn.sigmoid(gc) * up[rows, :]).astype(BF16)
            if tiles_per_seq == 1:
                nf_ref[i * n_sub + s, j, :, cols] = cur[sub_len - (CONV_K - 1):, :]
            else:
                halo_ref[j, :, cols] = cur[sub_len - SUBLANES:, :]

    o_ref[...] += jnp.dot(act_ref[...], wd_ref[...], preferred_element_type=F32)

    if tiles_per_seq > 1:
        @pl.when(i % tiles_per_seq == tiles_per_seq - 1)
        def _():
            nf_ref[i // tiles_per_seq, j] = halo_ref[j, SUBLANES - (CONV_K - 1):, :]

    @pl.when(j == n_ff - 1)
    def _():
        o_ref[...] = _rms_rows(o_ref[...], gf_ref[...])


def _ffn(x2d, g_ffn, w_gate_up_bf16, w_ffn_conv, state, w_down_bf16, g_final, *, tm, seq_len):
    m = x2d.shape[0]
    n_seq = m // seq_len
    tf = FF_TILE
    n_ff = D_FF // tf
    sub_len, n_sub, tiles_per_seq = _seq_geometry(tm, seq_len)
    kernel = functools.partial(_ffn_kernel, tm=tm, seq_len=seq_len, n_ff=n_ff)
    f32_b, bf16_b = 4, 2
    vmem_bytes = (4 * tm * D_MODEL * f32_b + 2 * 3 * D_MODEL * tf * bf16_b + tm * D_MODEL * bf16_b
                  + tm * tf * bf16_b + (SUBLANES + sub_len) * tf * f32_b + 4 * tm * tf * f32_b)
    y, new_ffn = pl.pallas_call(
        kernel,
        out_shape=(jax.ShapeDtypeStruct((m, D_MODEL), F32),
                   jax.ShapeDtypeStruct((n_seq, n_ff, CONV_K - 1, tf), F32)),
        grid=(m // tm, n_ff),
        in_specs=[
            pl.BlockSpec((tm, D_MODEL), lambda i, j: (i, 0)),
            pl.BlockSpec((1, D_MODEL), lambda i, j: (0, 0)),
            pl.BlockSpec((D_MODEL, tf), lambda i, j: (0, j)),
            pl.BlockSpec((D_MODEL, tf), lambda i, j: (0, j + n_ff)),
            pl.BlockSpec((CONV_K, tf), lambda i, j: (0, j)),
            pl.BlockSpec((n_sub, SUBLANES, tf), lambda i, j: (i // tiles_per_seq, 0, j)),
            pl.BlockSpec((tf, D_MODEL), lambda i, j: (j, 0)),
            pl.BlockSpec((1, D_MODEL), lambda i, j: (0, 0)),
        ],
        out_specs=(pl.BlockSpec((tm, D_MODEL), lambda i, j: (i, 0)),
                   pl.BlockSpec((n_seq, n_ff, CONV_K - 1, tf), lambda i, j: (0, 0, 0, 0))),
        scratch_shapes=[pltpu.VMEM((tm, D_MODEL), BF16),
                        pltpu.VMEM((n_ff, SUBLANES, tf), F32),
                        pltpu.VMEM((SUBLANES + sub_len, tf), F32),
                        pltpu.VMEM((tm, tf), BF16)],
        compiler_params=_params(("arbitrary", "arbitrary"), vmem_bytes),
        name="conv_ffn",
    )(x2d, g_ffn.reshape(1, D_MODEL), w_gate_up_bf16, w_gate_up_bf16, w_ffn_conv, _state_rows(state),
      w_down_bf16, g_final.reshape(1, D_MODEL))
    return y, new_ffn.transpose(0, 2, 1, 3).reshape(n_seq, CONV_K - 1, D_FF)


def _tiles(batch, seq_len):
    rows = batch * seq_len
    if rows >= 4096:
        return dict(tm_proj=1024, tm_mix=256, tm_ffn=512, tq=KV_CHUNK)
    return dict(tm_proj=rows, tm_mix=rows, tm_ffn=rows, tq=LANES)


def _trunk(x, cache_k, cache_v, state_conv, state_ffn, weights):
    g_mix, w_in, w_conv, g_att, g_conv, w_o, g_ffn, w_gate_up, w_ffn_conv, w_down, g_final = weights
    b, t, _ = x.shape
    past = cache_k.shape[1]
    tiles = _tiles(b, t)
    tq = tiles["tq"]
    x2d = x.reshape(b * t, D_MODEL)

    proj = _in_proj(x2d, g_mix, w_in, tm=tiles["tm_proj"])
    proj3 = proj.reshape(b, t, IN_WIDTH)
    tq_pad = -(-t // tq) * tq
    tk_pad = -(-(past + tq_pad) // KV_CHUNK) * KV_CHUNK
    if past == 0 and tq_pad == t and tk_pad == t:
        q_h, k_h, v_h, k, v = _qkv_layouts(proj3)
    else:
        k = proj3[:, :, ATT_WIDTH:2 * ATT_WIDTH]
        v = proj3[:, :, 2 * ATT_WIDTH:3 * ATT_WIDTH]

        def key_heads(cache, new, transpose):
            tail = jnp.pad(new, ((0, 0), (0, tk_pad - past - t), (0, 0)))
            return _to_heads(cache.reshape(b, past, ATT_WIDTH), tail, permute=True, transpose=transpose)

        q_pad = jnp.pad(proj3[:, :, :ATT_WIDTH], ((0, 0), (0, tq_pad - t), (0, 0)))
        q_h = _to_heads(jnp.zeros((b, 0, ATT_WIDTH), F32), q_pad, permute=False, transpose=True)
        k_h = key_heads(cache_k, k, False)
        v_h = key_heads(cache_v, v, True)
    att = _attention(q_h, k_h, v_h, g_att, past=past, tq=tq)
    att2d = att[:, :t, :].reshape(b * t, ATT_WIDTH)

    x1, new_conv = _mix(att2d, proj, state_conv, w_conv, g_conv, w_o, x2d, tm=tiles["tm_mix"], seq_len=t)
    y, new_ffn = _ffn(x1, g_ffn, w_gate_up, w_ffn_conv, state_ffn, w_down, g_final, tm=tiles["tm_ffn"], seq_len=t)
    shape4 = (b, t, N_HEADS, HEAD_DIM)
    return y.reshape(b, t, D_MODEL), k.reshape(shape4), v.reshape(shape4), new_conv, new_ffn


def kernel(x_prompt, x_sample, cache_k, cache_v, state_conv, state_ffn_conv, g_mix, w_in, w_conv, g_att_out,
           g_conv_out, w_o, g_ffn, w_gate_up, w_ffn_conv, w_down, g_final):
    depth = g_mix.shape[0]
    assert depth == 1
    weights = (g_mix[0], w_in[0].astype(BF16), w_conv[0], g_att_out[0], g_conv_out[0], w_o[0].astype(BF16),
               g_ffn[0], w_gate_up[0].astype(BF16), w_ffn_conv[0], w_down[0].astype(BF16), g_final)
    bp = x_prompt.shape[0]
    dt = x_prompt.dtype
    empty = jnp.zeros((bp, 0, N_HEADS, HEAD_DIM), dt)
    y_p, k_p, v_p, c_p, f_p = _trunk(
        x_prompt, empty, empty, jnp.zeros((bp, CONV_K - 1, CONV_WIDTH), dt), jnp.zeros((bp, CONV_K - 1, D_FF), dt),
        weights)
    y_s, k_s, v_s, c_s, f_s = _trunk(x_sample, cache_k[0], cache_v[0], state_conv[0], state_ffn_conv[0], weights)
    return (y_p, y_s, k_p[None], v_p[None], c_p[None], f_p[None], k_s[None], v_s[None], c_s[None], f_s[None])
```

```python
import functools
import math

import jax
import jax.numpy as jnp
import numpy as np
from jax import lax
from jax.experimental import pallas as pl
from jax.experimental.pallas import tpu as pltpu

F32 = jnp.float32
BF16 = jnp.bfloat16

D_MODEL = 2048
N_HEADS = 8
HEAD_DIM = 128
ATT_WIDTH = N_HEADS * HEAD_DIM
N_GROUPS = 8
GROUP = 128
CONV_WIDTH = N_GROUPS * GROUP
IN_WIDTH = 3 * ATT_WIDTH + 3 * CONV_WIDTH
D_FF = 5632
CONV_K = 3
EPS = 1e-6
Q_SCALE = (HEAD_DIM ** -0.5) * math.log2(math.e)

SUBLANES = 8
LANES = 128
SEGS = 2 * SUBLANES
SEG_LEN = 32
KV_CHUNK = SEGS * SEG_LEN
VMEM_LIMIT_BYTES = 48 * 1024 * 1024

FF_TILE = 512
IN_PROJ_COL_TILE = 1024
assert ATT_WIDTH == IN_PROJ_COL_TILE and CONV_WIDTH == IN_PROJ_COL_TILE


def _params(sem, vmem_limit_bytes=VMEM_LIMIT_BYTES):
    return pltpu.CompilerParams(dimension_semantics=sem, vmem_limit_bytes=vmem_limit_bytes)


def _rms_rows(x, g):
    ms = jnp.mean(x * x, axis=-1, keepdims=True)
    return (x * lax.rsqrt(ms + EPS)) * g


def _key_row_permutation():
    p = np.zeros((KV_CHUNK, KV_CHUNK), np.float32)
    for i in range(SEG_LEN):
        for s in range(SEGS):
            p[SEGS * i + s, SEG_LEN * s + SEG_LEN - 1 - i] = 1.0
    return jnp.asarray(p, BF16)


def _in_proj_kernel(x_ref, g_ref, w_ref, o_ref, h_ref):
    j = pl.program_id(1)

    @pl.when(j == 0)
    def _():
        h_ref[...] = _rms_rows(x_ref[...], g_ref[...]).astype(BF16)

    acc = jnp.dot(h_ref[...], w_ref[...], preferred_element_type=F32)
    o_ref[...] = acc * jnp.where(j == 0, Q_SCALE, 1.0).astype(F32)


def _in_proj(x2d, g_mix, w_in_bf16, *, tm):
    m = x2d.shape[0]
    tn = IN_PROJ_COL_TILE
    return pl.pallas_call(
        _in_proj_kernel,
        out_shape=jax.ShapeDtypeStruct((m, IN_WIDTH), F32),
        grid=(m // tm, IN_WIDTH // tn),
        in_specs=[
            pl.BlockSpec((tm, D_MODEL), lambda i, j: (i, 0)),
            pl.BlockSpec((1, D_MODEL), lambda i, j: (0, 0)),
            pl.BlockSpec((D_MODEL, tn), lambda i, j: (0, j)),
        ],
        out_specs=pl.BlockSpec((tm, tn), lambda i, j: (i, j)),
        scratch_shapes=[pltpu.VMEM((tm, D_MODEL), BF16)],
        compiler_params=_params(("arbitrary", "arbitrary")),
        name="in_proj",
    )(x2d, g_mix.reshape(1, D_MODEL), w_in_bf16)


def _store_heads(x32, p_ref, o_ref, *, permute, transpose):
    if permute:
        x32 = jnp.dot(p_ref[...], x32.astype(BF16), preferred_element_type=F32)
    for h in range(N_HEADS):
        xh = x32[:, h * HEAD_DIM:(h + 1) * HEAD_DIM]
        o_ref[0, h] = (xh.T if transpose else xh).astype(BF16)


def _to_heads_kernel(p_ref, x_ref, tail_ref, o_ref, *, permute, transpose, n_src_blocks):
    src32 = jnp.concatenate([x_ref[0, :, h, :] for h in range(N_HEADS)], axis=1)
    x32 = jnp.where(pl.program_id(1) < n_src_blocks, src32, tail_ref[0])
    _store_heads(x32, p_ref, o_ref, permute=permute, transpose=transpose)


def _qkv_layouts_kernel(p_ref, x_ref, qT_ref, k_ref, vT_ref, k32_ref, v32_ref):
    q32 = x_ref[0, :, 0:ATT_WIDTH]
    k32 = x_ref[0, :, ATT_WIDTH:2 * ATT_WIDTH]
    v32 = x_ref[0, :, 2 * ATT_WIDTH:3 * ATT_WIDTH]
    k32_ref[0] = k32.reshape(k32.shape[0], N_HEADS, HEAD_DIM)
    v32_ref[0] = v32.reshape(v32.shape[0], N_HEADS, HEAD_DIM)
    _store_heads(q32, p_ref, qT_ref, permute=False, transpose=True)
    _store_heads(k32, p_ref, k_ref, permute=True, transpose=False)
    _store_heads(v32, p_ref, vT_ref, permute=True, transpose=True)


def _qkv_layouts(proj3):
    b, t, _ = proj3.shape
    rows = KV_CHUNK
    assert t % rows == 0
    heads_t = jax.ShapeDtypeStruct((b, N_HEADS, HEAD_DIM, t), BF16)
    heads = jax.ShapeDtypeStruct((b, N_HEADS, t, HEAD_DIM), BF16)
    plain = jax.ShapeDtypeStruct((b, t, N_HEADS, HEAD_DIM), F32)
    spec_t = pl.BlockSpec((1, N_HEADS, HEAD_DIM, rows), lambda bi, i: (bi, 0, 0, i))
    spec_h = pl.BlockSpec((1, N_HEADS, rows, HEAD_DIM), lambda bi, i: (bi, 0, i, 0))
    spec_p = pl.BlockSpec((1, rows, N_HEADS, HEAD_DIM), lambda bi, i: (bi, i, 0, 0))
    qT, k_h, vT, k32, v32 = pl.pallas_call(
        _qkv_layouts_kernel,
        out_shape=(heads_t, heads, heads_t, plain, plain),
        grid=(b, t // rows),
        in_specs=[
            pl.BlockSpec((KV_CHUNK, KV_CHUNK), lambda bi, i: (0, 0)),
            pl.BlockSpec((1, rows, 3 * ATT_WIDTH), lambda bi, i: (bi, i, 0)),
        ],
        out_specs=(spec_t, spec_h, spec_t, spec_p, spec_p),
        compiler_params=_params(("arbitrary", "arbitrary")),
        name="qkv_layouts",
    )(_key_row_permutation(), proj3)
    merge = lambda a: a.reshape((b * N_HEADS,) + a.shape[2:])
    return merge(qT), merge(k_h), merge(vT), k32, v32


def _to_heads(src, tail, *, permute, transpose):
    b, t_src = src.shape[:2]
    t = t_src + tail.shape[1]
    rows = KV_CHUNK if permute else math.gcd(math.gcd(t_src, tail.shape[1]), KV_CHUNK)
    assert t_src % rows == 0 and tail.shape[1] % rows == 0 and rows % LANES == 0
    n_src = t_src // rows
    if n_src == 0:
        src = tail.reshape(b, tail.shape[1], N_HEADS, HEAD_DIM)
    if transpose:
        out_shape = jax.ShapeDtypeStruct((b, N_HEADS, HEAD_DIM, t), BF16)
        out_spec = pl.BlockSpec((1, N_HEADS, HEAD_DIM, rows), lambda bi, i: (bi, 0, 0, i))
    else:
        out_shape = jax.ShapeDtypeStruct((b, N_HEADS, t, HEAD_DIM), BF16)
        out_spec = pl.BlockSpec((1, N_HEADS, rows, HEAD_DIM), lambda bi, i: (bi, 0, i, 0))
    out = pl.pallas_call(
        functools.partial(_to_heads_kernel, permute=permute, transpose=transpose, n_src_blocks=n_src),
        out_shape=out_shape,
        grid=(b, t // rows),
        in_specs=[
            pl.BlockSpec((KV_CHUNK, KV_CHUNK), lambda bi, i: (0, 0)),
            pl.BlockSpec((1, rows, N_HEADS, HEAD_DIM), lambda bi, i: (bi, jnp.minimum(i, max(n_src - 1, 0)), 0, 0)),
            pl.BlockSpec((1, rows, ATT_WIDTH), lambda bi, i: (bi, jnp.maximum(i - n_src, 0), 0)),
        ],
        out_specs=out_spec,
        compiler_params=_params(("arbitrary", "arbitrary")),
        name="to_heads",
    )(_key_row_permutation(), src, tail)
    return out.reshape((b * N_HEADS,) + out.shape[2:])


EXP2_UNDERFLOW = 160.0
SCORE_BOUND_SLACK = 1.01


def _softplus2(z):
    e = jnp.exp2(-jnp.abs(z))
    return jnp.maximum(z, 0.0) + jnp.log(1.0 + e) * math.log2(math.e)


def _attn_kernel(qT_ref, k_ref, vT_ref, g_ref, o_ref, z_ref, d_ref, w_ref, acc_ref, carry_ref, kmax_ref,
                 *, past, tq, n_chunks_total):
    qi = pl.program_id(1)
    q_lo = past + qi * tq
    j_top = q_lo >> (KV_CHUNK.bit_length() - 1)

    qT = qT_ref[0]
    qpos = q_lo + lax.broadcasted_iota(jnp.int32, (SEGS, tq), 1)
    top_key = j_top * KV_CHUNK + SEG_LEN * lax.broadcasted_iota(jnp.int32, (SEGS, tq), 0) + (SEG_LEN - 1)

    def causal(i):
        return top_key - i < qpos

    def scores(j):
        base = pl.multiple_of(j * KV_CHUNK, KV_CHUNK)
        kc = k_ref[0, pl.ds(base, KV_CHUNK), :]
        return jnp.dot(kc, qT, preferred_element_type=F32)

    def add_values(j):
        base = pl.multiple_of(j * KV_CHUNK, KV_CHUNK)
        vc = vT_ref[0, :, pl.ds(base, KV_CHUNK)]
        acc_ref[...] += jnp.dot(vc, w_ref[...], preferred_element_type=F32)

    def visit(j, cur, masked):
        z_ref[1 - cur] = scores(jnp.maximum(j - 1, 0))
        if not masked:
            add_values(j + 1)
        run = jnp.zeros((SEGS, tq), F32)
        for i in range(SEG_LEN):
            rows = slice(i * SEGS, (i + 1) * SEGS)
            z = z_ref[cur, rows, :]
            sp = _softplus2(z)
            if masked:
                sp = jnp.where(causal(i), sp, 0.0)
            run = run + sp
            d_ref[rows, :] = z - run
        tail = carry_ref[0:1, :]
        offs = [None] * SEGS
        for s in reversed(range(SEGS)):
            offs[s] = tail
            tail = tail + run[s:s + 1, :]
        carry_ref[...] = jnp.broadcast_to(tail, carry_ref.shape)
        off = jnp.concatenate(offs, axis=0)
        for i in range(SEG_LEN):
            rows = slice(i * SEGS, (i + 1) * SEGS)
            w = jnp.exp2(d_ref[rows, :] - off)
            if masked:
                w = jnp.where(causal(i), w, 0.0)
            w_ref[rows, :] = w.astype(BF16)

    @pl.when(qi == 0)
    def _():
        def chunk_max(c, m):
            kc = k_ref[0, pl.ds(pl.multiple_of(c * KV_CHUNK, KV_CHUNK), KV_CHUNK), :].astype(F32)
            ss = jnp.sum(kc * kc, axis=1, keepdims=True)
            return jnp.maximum(m, jnp.max(ss, axis=0, keepdims=True))

        m = lax.fori_loop(0, n_chunks_total, chunk_max, jnp.zeros((1, 1), F32))
        kmax_ref[...] = jnp.broadcast_to(jnp.sqrt(m), kmax_ref.shape)

    q32 = qT.astype(F32)
    q_norm = jnp.sqrt(jnp.sum(q32 * q32, axis=0, keepdims=True))
    dead_at = q_norm * kmax_ref[0:1, 0:1] * SCORE_BOUND_SLACK + EXP2_UNDERFLOW

    def rest_is_zero():
        return jnp.min(carry_ref[0:1, :] - dead_at) >= 0.0

    acc_ref[...] = jnp.zeros_like(acc_ref)
    carry_ref[...] = jnp.zeros_like(carry_ref)
    z_ref[0] = scores(j_top)

    if tq == KV_CHUNK:
        first = j_top >= 1

        @pl.when(first)
        def _():
            visit(j_top, 0, True)
            visit(j_top - 1, 1, False)

        @pl.when(jnp.logical_not(first))
        def _():
            visit(j_top, 0, True)
    else:
        visit(j_top, 0, True)
        first = jnp.logical_and(j_top >= 1, jnp.logical_not(rest_is_zero()))

        @pl.when(first)
        def _():
            visit(j_top - 1, 1, False)

    n_pairs = jnp.maximum(j_top - 1, 0) >> 1

    def more_pairs(c):
        t, done = c
        return jnp.logical_and(t < n_pairs, jnp.logical_not(done))

    def pair(c):
        t, _ = c
        j = j_top - 2 - 2 * t
        visit(j, 0, False)
        visit(j - 1, 1, False)
        return t + 1, rest_is_zero()

    stop = jnp.logical_or(jnp.logical_not(first), rest_is_zero())
    t_end, done = lax.while_loop(more_pairs, pair, (jnp.int32(0), stop))
    odd_visit = jnp.logical_and(((j_top - 1) & 1) == 1, jnp.logical_not(done))

    @pl.when(odd_visit)
    def _():
        visit(0, 0, False)

    j_last = jnp.where(odd_visit, 0, jnp.where(first, j_top - 1 - 2 * t_end, j_top))
    add_values(j_last)

    out = acc_ref[...]
    ms = jnp.mean(out * out, axis=0, keepdims=True)
    y = (out * lax.rsqrt(ms + EPS)) * g_ref[0]
    o_ref[0] = y.T


def _attention(qT, k_perm, vT_perm, g_att, *, past, tq):
    bh_total, _, t_q = qT.shape
    b = bh_total // N_HEADS
    t_k = k_perm.shape[1]
    assert KV_CHUNK % tq == 0 and past % KV_CHUNK == 0 and t_q % tq == 0
    assert t_k % KV_CHUNK == 0 and t_k >= -(-(past + t_q) // KV_CHUNK) * KV_CHUNK
    kernel = functools.partial(_attn_kernel, past=past, tq=tq, n_chunks_total=t_k // KV_CHUNK)
    return pl.pallas_call(
        kernel,
        out_shape=jax.ShapeDtypeStruct((b, t_q, ATT_WIDTH), F32),
        grid=(b * N_HEADS, t_q // tq),
        in_specs=[
            pl.BlockSpec((1, HEAD_DIM, tq), lambda bh, i: (bh, 0, i)),
            pl.BlockSpec((1, t_k, HEAD_DIM), lambda bh, i: (bh, 0, 0)),
            pl.BlockSpec((1, HEAD_DIM, t_k), lambda bh, i: (bh, 0, 0)),
            pl.BlockSpec((1, HEAD_DIM, 1), lambda bh, i: (bh % N_HEADS, 0, 0)),
        ],
        out_specs=pl.BlockSpec((1, tq, HEAD_DIM), lambda bh, i: (bh // N_HEADS, i, bh % N_HEADS)),
        scratch_shapes=[
            pltpu.VMEM((2, KV_CHUNK, tq), F32),
            pltpu.VMEM((KV_CHUNK, tq), F32),
            pltpu.VMEM((KV_CHUNK, tq), BF16),
            pltpu.VMEM((HEAD_DIM, tq), F32),
            pltpu.VMEM((SUBLANES, tq), F32),
            pltpu.VMEM((SUBLANES, LANES), F32),
        ],
        compiler_params=_params(("arbitrary", "arbitrary")),
        name="sb_attention",
    )(qT, k_perm, vT_perm, g_att.reshape(N_HEADS, HEAD_DIM, 1))


def _dwconv3_rows(cur, w_ref, full_ref, cols):
    n = cur.shape[0]
    full_ref[SUBLANES:SUBLANES + n, cols] = cur
    return (full_ref[SUBLANES - 2:SUBLANES - 2 + n, cols] * w_ref[0:1, cols]
            + full_ref[SUBLANES - 1:SUBLANES - 1 + n, cols] * w_ref[1:2, cols]
            + cur * w_ref[2:3, cols])


def _seq_geometry(tm, seq_len):
    sub_len = min(seq_len, tm)
    return sub_len, tm // sub_len, max(seq_len // tm, 1)


def _mix_kernel(att_ref, b_ref, c_ref, u_ref, cp_ref, up_ref, st_ref, wc_ref, gc_ref, woa_ref, wob_ref,
                x_ref, o_ref, nc_ref, full_ref, yc_ref, *, tm, seq_len):
    i = pl.program_id(0)
    sub_len, n_sub, tiles_per_seq = _seq_geometry(tm, seq_len)
    halo_rows = slice(0, SUBLANES)
    all_cols = slice(0, CONV_WIDTH)
    if tiles_per_seq > 1:
        @pl.when(i % tiles_per_seq == 0)
        def _():
            full_ref[halo_rows, :] = st_ref[0]

        @pl.when(i % tiles_per_seq != 0)
        def _():
            full_ref[halo_rows, :] = cp_ref[...] * up_ref[...]

    out = x_ref[...] + jnp.dot(att_ref[...].astype(BF16), woa_ref[...], preferred_element_type=F32)
    cu = c_ref[...] * u_ref[...]
    for s in range(n_sub):
        rows = slice(s * sub_len, (s + 1) * sub_len)
        if tiles_per_seq == 1:
            full_ref[halo_rows, :] = st_ref[s]
        cur = cu[rows, :]
        y = b_ref[rows, :] * _dwconv3_rows(cur, wc_ref, full_ref, all_cols)
        for g in range(N_GROUPS):
            cols = slice(g * GROUP, (g + 1) * GROUP)
            yc_ref[rows, cols] = _rms_rows(y[:, cols], gc_ref[:, cols]).astype(BF16)
        if tiles_per_seq == 1:
            nc_ref[i * n_sub + s] = cur[sub_len - (CONV_K - 1):, :]

    o_ref[...] = out + jnp.dot(yc_ref[...], wob_ref[...], preferred_element_type=F32)

    if tiles_per_seq > 1:
        @pl.when(i % tiles_per_seq == tiles_per_seq - 1)
        def _():
            last = SUBLANES + sub_len
            nc_ref[i // tiles_per_seq] = full_ref[last - (CONV_K - 1):last, :]


def _state_rows(state):
    return jnp.pad(state, ((0, 0), (SUBLANES - (CONV_K - 1), 0), (0, 0)))


def _mix(att2d, proj, state, w_conv, g_conv, w_o_bf16, x2d, *, tm, seq_len):
    m = x2d.shape[0]
    n_seq = m // seq_len
    sub_len, n_sub, tiles_per_seq = _seq_geometry(tm, seq_len)
    rows_per_block = tm // SUBLANES
    col0 = 3 * ATT_WIDTH // CONV_WIDTH
    prev_map = lambda c: (lambda i: (jnp.maximum(i * rows_per_block - 1, 0), c))
    kernel = functools.partial(_mix_kernel, tm=tm, seq_len=seq_len)
    return pl.pallas_call(
        kernel,
        out_shape=(jax.ShapeDtypeStruct((m, D_MODEL), F32),
                   jax.ShapeDtypeStruct((n_seq, CONV_K - 1, CONV_WIDTH), F32)),
        grid=(m // tm,),
        in_specs=[
            pl.BlockSpec((tm, ATT_WIDTH), lambda i: (i, 0)),
            pl.BlockSpec((tm, CONV_WIDTH), lambda i: (i, col0)),
            pl.BlockSpec((tm, CONV_WIDTH), lambda i: (i, col0 + 1)),
            pl.BlockSpec((tm, CONV_WIDTH), lambda i: (i, col0 + 2)),
            pl.BlockSpec((SUBLANES, CONV_WIDTH), prev_map(col0 + 1)),
            pl.BlockSpec((SUBLANES, CONV_WIDTH), prev_map(col0 + 2)),
            pl.BlockSpec((n_sub, SUBLANES, CONV_WIDTH), lambda i: (i // tiles_per_seq, 0, 0)),
            pl.BlockSpec((CONV_K, CONV_WIDTH), lambda i: (0, 0)),
            pl.BlockSpec((1, CONV_WIDTH), lambda i: (0, 0)),
            pl.BlockSpec((ATT_WIDTH, D_MODEL), lambda i: (0, 0)),
            pl.BlockSpec((CONV_WIDTH, D_MODEL), lambda i: (1, 0)),
            pl.BlockSpec((tm, D_MODEL), lambda i: (i, 0)),
        ],
        out_specs=(pl.BlockSpec((tm, D_MODEL), lambda i: (i, 0)),
                   pl.BlockSpec((n_seq, CONV_K - 1, CONV_WIDTH), lambda i: (0, 0, 0))),
        scratch_shapes=[pltpu.VMEM((SUBLANES + sub_len, CONV_WIDTH), F32),
                        pltpu.VMEM((tm, CONV_WIDTH), BF16)],
        compiler_params=_params(("arbitrary",)),
        name="mix_out_proj",
    )(att2d, proj, proj, proj, proj, proj, _state_rows(state), w_conv, g_conv.reshape(1, CONV_WIDTH),
      w_o_bf16, w_o_bf16, x2d)


def _ffn_kernel(x_ref, g_ref, wg_ref, wu_ref, wc_ref, st_ref, wd_ref, gf_ref, o_ref, nf_ref,
                h_ref, halo_ref, full_ref, act_ref, *, tm, seq_len, n_ff):
    i = pl.program_id(0)
    j = pl.program_id(1)
    sub_len, n_sub, tiles_per_seq = _seq_geometry(tm, seq_len)
    tf = wg_ref.shape[1]
    half = tf // 2

    @pl.when(j == 0)
    def _():
        x = x_ref[...]
        h_ref[...] = _rms_rows(x, g_ref[...]).astype(BF16)
        o_ref[...] = x

    halo_rows = slice(0, SUBLANES)
    if tiles_per_seq > 1:
        @pl.when(i % tiles_per_seq == 0)
        def _():
            halo_ref[j] = st_ref[0]

    h = h_ref[...]
    for c in range(2):
        cols = slice(c * half, (c + 1) * half)
        gate = jnp.dot(h, wg_ref[:, cols], preferred_element_type=F32)
        up = jnp.dot(h, wu_ref[:, cols], preferred_element_type=F32)
        for s in range(n_sub):
            rows = slice(s * sub_len, (s + 1) * sub_len)
            if tiles_per_seq == 1:
                full_ref[halo_rows, cols] = st_ref[s, :, cols]
            else:
                full_ref[halo_rows, cols] = halo_ref[j, :, cols]
            cur = gate[rows, :]
            gc = _dwconv3_rows(cur, wc_ref, full_ref, cols)
            act_ref[rows, cols] = (gc * jax.nn.sigmoid(gc) * up[rows, :]).astype(BF16)
            if tiles_per_seq == 1:
                nf_ref[i * n_sub + s, j, :, cols] = cur[sub_len - (CONV_K - 1):, :]
            else:
                halo_ref[j, :, cols] = cur[sub_len - SUBLANES:, :]

    o_ref[...] += jnp.dot(act_ref[...], wd_ref[...], preferred_element_type=F32)

    if tiles_per_seq > 1:
        @pl.when(i % tiles_per_seq == tiles_per_seq - 1)
        def _():
            nf_ref[i // tiles_per_seq, j] = halo_ref[j, SUBLANES - (CONV_K - 1):, :]

    @pl.when(j == n_ff - 1)
    def _():
        o_ref[...] = _rms_rows(o_ref[...], gf_ref[...])


def _ffn(x2d, g_ffn, w_gate_up_bf16, w_ffn_conv, state, w_down_bf16, g_final, *, tm, seq_len):
    m = x2d.shape[0]
    n_seq = m // seq_len
    tf = FF_TILE
    n_ff = D_FF // tf
    sub_len, n_sub, tiles_per_seq = _seq_geometry(tm, seq_len)
    kernel = functools.partial(_ffn_kernel, tm=tm, seq_len=seq_len, n_ff=n_ff)
    f32_b, bf16_b = 4, 2
    vmem_bytes = (4 * tm * D_MODEL * f32_b + 2 * 3 * D_MODEL * tf * bf16_b + tm * D_MODEL * bf16_b
                  + tm * tf * bf16_b + (SUBLANES + sub_len) * tf * f32_b + 4 * tm * tf * f32_b)
    y, new_ffn = pl.pallas_call(
        kernel,
        out_shape=(jax.ShapeDtypeStruct((m, D_MODEL), F32),
                   jax.ShapeDtypeStruct((n_seq, n_ff, CONV_K - 1, tf), F32)),
        grid=(m // tm, n_ff),
        in_specs=[
            pl.BlockSpec((tm, D_MODEL), lambda i, j: (i, 0)),
            pl.BlockSpec((1, D_MODEL), lambda i, j: (0, 0)),
            pl.BlockSpec((D_MODEL, tf), lambda i, j: (0, j)),
            pl.BlockSpec((D_MODEL, tf), lambda i, j: (0, j + n_ff)),
            pl.BlockSpec((CONV_K, tf), lambda i, j: (0, j)),
            pl.BlockSpec((n_sub, SUBLANES, tf), lambda i, j: (i // tiles_per_seq, 0, j)),
            pl.BlockSpec((tf, D_MODEL), lambda i, j: (j, 0)),
            pl.BlockSpec((1, D_MODEL), lambda i, j: (0, 0)),
        ],
        out_specs=(pl.BlockSpec((tm, D_MODEL), lambda i, j: (i, 0)),
                   pl.BlockSpec((n_seq, n_ff, CONV_K - 1, tf), lambda i, j: (0, 0, 0, 0))),
        scratch_shapes=[pltpu.VMEM((tm, D_MODEL), BF16),
                        pltpu.VMEM((n_ff, SUBLANES, tf), F32),
                        pltpu.VMEM((SUBLANES + sub_len, tf), F32),
                        pltpu.VMEM((tm, tf), BF16)],
        compiler_params=_params(("arbitrary", "arbitrary"), vmem_bytes),
        name="conv_ffn",
    )(x2d, g_ffn.reshape(1, D_MODEL), w_gate_up_bf16, w_gate_up_bf16, w_ffn_conv, _state_rows(state),
      w_down_bf16, g_final.reshape(1, D_MODEL))
    return y, new_ffn.transpose(0, 2, 1, 3).reshape(n_seq, CONV_K - 1, D_FF)


def _tiles(batch, seq_len):
    rows = batch * seq_len
    if rows >= 4096:
        return dict(tm_proj=1024, tm_mix=256, tm_ffn=512, tq=KV_CHUNK)
    return dict(tm_proj=rows, tm_mix=rows, tm_ffn=rows, tq=LANES)


def _trunk(x, cache_k, cache_v, state_conv, state_ffn, weights):
    g_mix, w_in, w_conv, g_att, g_conv, w_o, g_ffn, w_gate_up, w_ffn_conv, w_down, g_final = weights
    b, t, _ = x.shape
    past = cache_k.shape[1]
    tiles = _tiles(b, t)
    tq = tiles["tq"]
    x2d = x.reshape(b * t, D_MODEL)

    proj = _in_proj(x2d, g_mix, w_in, tm=tiles["tm_proj"])
    proj3 = proj.reshape(b, t, IN_WIDTH)
    tq_pad = -(-t // tq) * tq
    tk_pad = -(-(past + tq_pad) // KV_CHUNK) * KV_CHUNK
    if past == 0 and tq_pad == t and tk_pad == t:
        q_h, k_h, v_h, k, v = _qkv_layouts(proj3)
    else:
        k = proj3[:, :, ATT_WIDTH:2 * ATT_WIDTH]
        v = proj3[:, :, 2 * ATT_WIDTH:3 * ATT_WIDTH]

        def key_heads(cache, new, transpose):
            tail = jnp.pad(new, ((0, 0), (0, tk_pad - past - t), (0, 0)))
            return _to_heads(cache, tail, permute=True, transpose=transpose)

        q_pad = jnp.pad(proj3[:, :, :ATT_WIDTH], ((0, 0), (0, tq_pad - t), (0, 0)))
        q_h = _to_heads(jnp.zeros((b, 0, N_HEADS, HEAD_DIM), F32), q_pad, permute=False, transpose=True)
        k_h = key_heads(cache_k, k, False)
        v_h = key_heads(cache_v, v, True)
    att = _attention(q_h, k_h, v_h, g_att, past=past, tq=tq)
    att2d = att[:, :t, :].reshape(b * t, ATT_WIDTH)

    x1, new_conv = _mix(att2d, proj, state_conv, w_conv, g_conv, w_o, x2d, tm=tiles["tm_mix"], seq_len=t)
    y, new_ffn = _ffn(x1, g_ffn, w_gate_up, w_ffn_conv, state_ffn, w_down, g_final, tm=tiles["tm_ffn"], seq_len=t)
    shape4 = (b, t, N_HEADS, HEAD_DIM)
    return y.reshape(b, t, D_MODEL), k.reshape(shape4), v.reshape(shape4), new_conv, new_ffn


def kernel(x_prompt, x_sample, cache_k, cache_v, state_conv, state_ffn_conv, g_mix, w_in, w_conv, g_att_out,
           g_conv_out, w_o, g_ffn, w_gate_up, w_ffn_conv, w_down, g_final):
    depth = g_mix.shape[0]
    assert depth == 1
    weights = (g_mix[0], w_in[0].astype(BF16), w_conv[0], g_att_out[0], g_conv_out[0], w_o[0].astype(BF16),
               g_ffn[0], w_gate_up[0].astype(BF16), w_ffn_conv[0], w_down[0].astype(BF16), g_final)
    bp = x_prompt.shape[0]
    dt = x_prompt.dtype
    empty = jnp.zeros((bp, 0, N_HEADS, HEAD_DIM), dt)
    y_p, k_p, v_p, c_p, f_p = _trunk(
        x_prompt, empty, empty, jnp.zeros((bp, CONV_K - 1, CONV_WIDTH), dt), jnp.zeros((bp, CONV_K - 1, D_FF), dt),
        weights)
    y_s, k_s, v_s, c_s, f_s = _trunk(x_sample, cache_k[0], cache_v[0], state_conv[0], state_ffn_conv[0], weights)
    return (y_p, y_s, k_p[None], v_p[None], c_p[None], f_p[None], k_s[None], v_s[None], c_s[None], f_s[None])
```

```python
import functools
import math

import jax
import jax.numpy as jnp
import numpy as np
from jax import lax
from jax.experimental import pallas as pl
from jax.experimental.pallas import tpu as pltpu

F32 = jnp.float32
BF16 = jnp.bfloat16

D_MODEL = 2048
N_HEADS = 8
HEAD_DIM = 128
ATT_WIDTH = N_HEADS * HEAD_DIM
N_GROUPS = 8
GROUP = 128
CONV_WIDTH = N_GROUPS * GROUP
IN_WIDTH = 3 * ATT_WIDTH + 3 * CONV_WIDTH
D_FF = 5632
CONV_K = 3
EPS = 1e-6
Q_SCALE = (HEAD_DIM ** -0.5) * math.log2(math.e)

SUBLANES = 8
LANES = 128
SEGS = 2 * SUBLANES
SEG_LEN = 32
KV_CHUNK = SEGS * SEG_LEN
VMEM_LIMIT_BYTES = 48 * 1024 * 1024

FF_TILE = 512
IN_PROJ_COL_TILE = 1024
assert ATT_WIDTH == IN_PROJ_COL_TILE and CONV_WIDTH == IN_PROJ_COL_TILE


def _params(sem, vmem_limit_bytes=VMEM_LIMIT_BYTES):
    return pltpu.CompilerParams(dimension_semantics=sem, vmem_limit_bytes=vmem_limit_bytes)


def _rms_rows(x, g):
    ms = jnp.mean(x * x, axis=-1, keepdims=True)
    return (x * lax.rsqrt(ms + EPS)) * g


def _key_row_permutation():
    p = np.zeros((KV_CHUNK, KV_CHUNK), np.float32)
    for i in range(SEG_LEN):
        for s in range(SEGS):
            p[SEGS * i + s, SEG_LEN * s + SEG_LEN - 1 - i] = 1.0
    return jnp.asarray(p, BF16)


def _in_proj_kernel(x_ref, g_ref, w_ref, o_ref, h_ref):
    j = pl.program_id(1)

    @pl.when(j == 0)
    def _():
        h_ref[...] = _rms_rows(x_ref[...], g_ref[...]).astype(BF16)

    acc = jnp.dot(h_ref[...], w_ref[...], preferred_element_type=F32)
    o_ref[...] = acc * jnp.where(j == 0, Q_SCALE, 1.0).astype(F32)


def _in_proj(x2d, g_mix, w_in_bf16, *, tm):
    m = x2d.shape[0]
    tn = IN_PROJ_COL_TILE
    return pl.pallas_call(
        _in_proj_kernel,
        out_shape=jax.ShapeDtypeStruct((m, IN_WIDTH), F32),
        grid=(m // tm, IN_WIDTH // tn),
        in_specs=[
            pl.BlockSpec((tm, D_MODEL), lambda i, j: (i, 0)),
            pl.BlockSpec((1, D_MODEL), lambda i, j: (0, 0)),
            pl.BlockSpec((D_MODEL, tn), lambda i, j: (0, j)),
        ],
        out_specs=pl.BlockSpec((tm, tn), lambda i, j: (i, j)),
        scratch_shapes=[pltpu.VMEM((tm, D_MODEL), BF16)],
        compiler_params=_params(("arbitrary", "arbitrary")),
        name="in_proj",
    )(x2d, g_mix.reshape(1, D_MODEL), w_in_bf16)


def _store_heads(x32, p_ref, o_ref, *, permute, transpose):
    if permute:
        x32 = jnp.dot(p_ref[...], x32.astype(BF16), preferred_element_type=F32)
    for h in range(N_HEADS):
        xh = x32[:, h * HEAD_DIM:(h + 1) * HEAD_DIM]
        o_ref[0, h] = (xh.T if transpose else xh).astype(BF16)


def _to_heads_kernel(p_ref, x_ref, tail_ref, o_ref, *, permute, transpose, n_src_blocks):
    src32 = jnp.concatenate([x_ref[0, :, h, :] for h in range(N_HEADS)], axis=1)
    x32 = jnp.where(pl.program_id(1) < n_src_blocks, src32, tail_ref[0])
    _store_heads(x32, p_ref, o_ref, permute=permute, transpose=transpose)


def _qkv_layouts_kernel(p_ref, x_ref, qT_ref, k_ref, vT_ref, k32_ref, v32_ref):
    q32 = x_ref[0, :, 0:ATT_WIDTH]
    k32 = x_ref[0, :, ATT_WIDTH:2 * ATT_WIDTH]
    v32 = x_ref[0, :, 2 * ATT_WIDTH:3 * ATT_WIDTH]
    k32_ref[0] = k32.reshape(k32.shape[0], N_HEADS, HEAD_DIM)
    v32_ref[0] = v32.reshape(v32.shape[0], N_HEADS, HEAD_DIM)
    _store_heads(q32, p_ref, qT_ref, permute=False, transpose=True)
    _store_heads(k32, p_ref, k_ref, permute=True, transpose=False)
    _store_heads(v32, p_ref, vT_ref, permute=True, transpose=True)


def _qkv_layouts(proj3):
    b, t, _ = proj3.shape
    rows = KV_CHUNK
    assert t % rows == 0
    heads_t = jax.ShapeDtypeStruct((b, N_HEADS, HEAD_DIM, t), BF16)
    heads = jax.ShapeDtypeStruct((b, N_HEADS, t, HEAD_DIM), BF16)
    plain = jax.ShapeDtypeStruct((b, t, N_HEADS, HEAD_DIM), F32)
    spec_t = pl.BlockSpec((1, N_HEADS, HEAD_DIM, rows), lambda bi, i: (bi, 0, 0, i))
    spec_h = pl.BlockSpec((1, N_HEADS, rows, HEAD_DIM), lambda bi, i: (bi, 0, i, 0))
    spec_p = pl.BlockSpec((1, rows, N_HEADS, HEAD_DIM), lambda bi, i: (bi, i, 0, 0))
    qT, k_h, vT, k32, v32 = pl.pallas_call(
        _qkv_layouts_kernel,
        out_shape=(heads_t, heads, heads_t, plain, plain),
        grid=(b, t // rows),
        in_specs=[
            pl.BlockSpec((KV_CHUNK, KV_CHUNK), lambda bi, i: (0, 0)),
            pl.BlockSpec((1, rows, 3 * ATT_WIDTH), lambda bi, i: (bi, i, 0)),
        ],
        out_specs=(spec_t, spec_h, spec_t, spec_p, spec_p),
        compiler_params=_params(("arbitrary", "arbitrary")),
        name="qkv_layouts",
    )(_key_row_permutation(), proj3)
    merge = lambda a: a.reshape((b * N_HEADS,) + a.shape[2:])
    return merge(qT), merge(k_h), merge(vT), k32, v32


def _to_heads(src, tail, *, permute, transpose):
    b, t_src = src.shape[:2]
    t = t_src + tail.shape[1]
    rows = KV_CHUNK if permute else math.gcd(math.gcd(t_src, tail.shape[1]), KV_CHUNK)
    assert t_src % rows == 0 and tail.shape[1] % rows == 0 and rows % LANES == 0
    n_src = t_src // rows
    if n_src == 0:
        src = tail.reshape(b, tail.shape[1], N_HEADS, HEAD_DIM)
    if transpose:
        out_shape = jax.ShapeDtypeStruct((b, N_HEADS, HEAD_DIM, t), BF16)
        out_spec = pl.BlockSpec((1, N_HEADS, HEAD_DIM, rows), lambda bi, i: (bi, 0, 0, i))
    else:
        out_shape = jax.ShapeDtypeStruct((b, N_HEADS, t, HEAD_DIM), BF16)
        out_spec = pl.BlockSpec((1, N_HEADS, rows, HEAD_DIM), lambda bi, i: (bi, 0, i, 0))
    out = pl.pallas_call(
        functools.partial(_to_heads_kernel, permute=permute, transpose=transpose, n_src_blocks=n_src),
        out_shape=out_shape,
        grid=(b, t // rows),
        in_specs=[
            pl.BlockSpec((KV_CHUNK, KV_CHUNK), lambda bi, i: (0, 0)),
            pl.BlockSpec((1, rows, N_HEADS, HEAD_DIM), lambda bi, i: (bi, jnp.minimum(i, max(n_src - 1, 0)), 0, 0)),
            pl.BlockSpec((1, rows, ATT_WIDTH), lambda bi, i: (bi, jnp.maximum(i - n_src, 0), 0)),
        ],
        out_specs=out_spec,
        compiler_params=_params(("arbitrary", "arbitrary")),
        name="to_heads",
    )(_key_row_permutation(), src, tail)
    return out.reshape((b * N_HEADS,) + out.shape[2:])


EXP2_UNDERFLOW = 160.0
SCORE_BOUND_SLACK = 1.01


def _softplus2(z):
    e = jnp.exp2(-jnp.abs(z))
    return jnp.maximum(z, 0.0) + jnp.log(1.0 + e) * math.log2(math.e)


def _attn_kernel(qT_ref, k_ref, vT_ref, g_ref, o_ref, z_ref, d_ref, w_ref, acc_ref, carry_ref, kmax_ref,
                 *, past, tq, n_chunks_total):
    qi = pl.program_id(1)
    q_lo = past + qi * tq
    j_top = q_lo >> (KV_CHUNK.bit_length() - 1)

    qT = qT_ref[0]
    qpos = q_lo + lax.broadcasted_iota(jnp.int32, (SEGS, tq), 1)
    top_key = j_top * KV_CHUNK + SEG_LEN * lax.broadcasted_iota(jnp.int32, (SEGS, tq), 0) + (SEG_LEN - 1)

    def causal(i):
        return top_key - i < qpos

    def scores(j):
        base = pl.multiple_of(j * KV_CHUNK, KV_CHUNK)
        kc = k_ref[0, pl.ds(base, KV_CHUNK), :]
        return jnp.dot(kc, qT, preferred_element_type=F32)

    def add_values(j):
        base = pl.multiple_of(j * KV_CHUNK, KV_CHUNK)
        vc = vT_ref[0, :, pl.ds(base, KV_CHUNK)]
        acc_ref[...] += jnp.dot(vc, w_ref[...], preferred_element_type=F32)

    def visit(j, cur, masked):
        z_ref[1 - cur] = scores(jnp.maximum(j - 1, 0))
        if not masked:
            add_values(j + 1)
        run = jnp.zeros((SEGS, tq), F32)
        for i in range(SEG_LEN):
            rows = slice(i * SEGS, (i + 1) * SEGS)
            z = z_ref[cur, rows, :]
            sp = _softplus2(z)
            if masked:
                sp = jnp.where(causal(i), sp, 0.0)
            run = run + sp
            d_ref[rows, :] = z - run
        tail = carry_ref[0:1, :]
        offs = [None] * SEGS
        for s in reversed(range(SEGS)):
            offs[s] = tail
            tail = tail + run[s:s + 1, :]
        carry_ref[...] = jnp.broadcast_to(tail, carry_ref.shape)
        off = jnp.concatenate(offs, axis=0)
        for i in range(SEG_LEN):
            rows = slice(i * SEGS, (i + 1) * SEGS)
            w = jnp.exp2(d_ref[rows, :] - off)
            if masked:
                w = jnp.where(causal(i), w, 0.0)
            w_ref[rows, :] = w.astype(BF16)

    @pl.when(qi == 0)
    def _():
        def chunk_max(c, m):
            kc = k_ref[0, pl.ds(pl.multiple_of(c * KV_CHUNK, KV_CHUNK), KV_CHUNK), :].astype(F32)
            ss = jnp.sum(kc * kc, axis=1, keepdims=True)
            return jnp.maximum(m, jnp.max(ss, axis=0, keepdims=True))

        m = lax.fori_loop(0, n_chunks_total, chunk_max, jnp.zeros((1, 1), F32))
        kmax_ref[...] = jnp.broadcast_to(jnp.sqrt(m), kmax_ref.shape)

    q32 = qT.astype(F32)
    q_norm = jnp.sqrt(jnp.sum(q32 * q32, axis=0, keepdims=True))
    dead_at = q_norm * kmax_ref[0:1, 0:1] * SCORE_BOUND_SLACK + EXP2_UNDERFLOW

    def rest_is_zero():
        return jnp.min(carry_ref[0:1, :] - dead_at) >= 0.0

    acc_ref[...] = jnp.zeros_like(acc_ref)
    carry_ref[...] = jnp.zeros_like(carry_ref)
    z_ref[0] = scores(j_top)

    visit(j_top, 0, True)
    first = jnp.logical_and(j_top >= 1, jnp.logical_not(rest_is_zero()))

    @pl.when(first)
    def _():
        visit(j_top - 1, 1, False)

    n_pairs = jnp.maximum(j_top - 1, 0) >> 1

    def more_pairs(c):
        t, done = c
        return jnp.logical_and(t < n_pairs, jnp.logical_not(done))

    def pair(c):
        t, _ = c
        j = j_top - 2 - 2 * t
        visit(j, 0, False)
        visit(j - 1, 1, False)
        return t + 1, rest_is_zero()

    stop = jnp.logical_or(jnp.logical_not(first), rest_is_zero())
    t_end, done = lax.while_loop(more_pairs, pair, (jnp.int32(0), stop))
    odd_visit = jnp.logical_and(((j_top - 1) & 1) == 1, jnp.logical_not(done))

    @pl.when(odd_visit)
    def _():
        visit(0, 0, False)

    j_last = jnp.where(odd_visit, 0, jnp.where(first, j_top - 1 - 2 * t_end, j_top))
    add_values(j_last)

    out = acc_ref[...]
    ms = jnp.mean(out * out, axis=0, keepdims=True)
    y = (out * lax.rsqrt(ms + EPS)) * g_ref[0]
    o_ref[0] = y.T


def _attention(qT, k_perm, vT_perm, g_att, *, past, tq):
    bh_total, _, t_q = qT.shape
    b = bh_total // N_HEADS
    t_k = k_perm.shape[1]
    assert KV_CHUNK % tq == 0 and past % KV_CHUNK == 0 and t_q % tq == 0
    assert t_k % KV_CHUNK == 0 and t_k >= -(-(past + t_q) // KV_CHUNK) * KV_CHUNK
    kernel = functools.partial(_attn_kernel, past=past, tq=tq, n_chunks_total=t_k // KV_CHUNK)
    return pl.pallas_call(
        kernel,
        out_shape=jax.ShapeDtypeStruct((b, t_q, ATT_WIDTH), F32),
        grid=(b * N_HEADS, t_q // tq),
        in_specs=[
            pl.BlockSpec((1, HEAD_DIM, tq), lambda bh, i: (bh, 0, i)),
            pl.BlockSpec((1, t_k, HEAD_DIM), lambda bh, i: (bh, 0, 0)),
            pl.BlockSpec((1, HEAD_DIM, t_k), lambda bh, i: (bh, 0, 0)),
            pl.BlockSpec((1, HEAD_DIM, 1), lambda bh, i: (bh % N_HEADS, 0, 0)),
        ],
        out_specs=pl.BlockSpec((1, tq, HEAD_DIM), lambda bh, i: (bh // N_HEADS, i, bh % N_HEADS)),
        scratch_shapes=[
            pltpu.VMEM((2, KV_CHUNK, tq), F32),
            pltpu.VMEM((KV_CHUNK, tq), F32),
            pltpu.VMEM((KV_CHUNK, tq), BF16),
            pltpu.VMEM((HEAD_DIM, tq), F32),
            pltpu.VMEM((SUBLANES, tq), F32),
            pltpu.VMEM((SUBLANES, LANES), F32),
        ],
        compiler_params=_params(("arbitrary", "arbitrary")),
        name="sb_attention",
    )(qT, k_perm, vT_perm, g_att.reshape(N_HEADS, HEAD_DIM, 1))


def _dwconv3_rows(cur, w_ref, full_ref, cols):
    n = cur.shape[0]
    full_ref[SUBLANES:SUBLANES + n, cols] = cur
    return (full_ref[SUBLANES - 2:SUBLANES - 2 + n, cols] * w_ref[0:1, cols]
            + full_ref[SUBLANES - 1:SUBLANES - 1 + n, cols] * w_ref[1:2, cols]
            + cur * w_ref[2:3, cols])


def _seq_geometry(tm, seq_len):
    sub_len = min(seq_len, tm)
    return sub_len, tm // sub_len, max(seq_len // tm, 1)


def _mix_kernel(att_ref, b_ref, c_ref, u_ref, cp_ref, up_ref, st_ref, wc_ref, gc_ref, woa_ref, wob_ref,
                x_ref, o_ref, nc_ref, full_ref, yc_ref, *, tm, seq_len):
    i = pl.program_id(0)
    sub_len, n_sub, tiles_per_seq = _seq_geometry(tm, seq_len)
    halo_rows = slice(0, SUBLANES)
    all_cols = slice(0, CONV_WIDTH)
    if tiles_per_seq > 1:
        @pl.when(i % tiles_per_seq == 0)
        def _():
            full_ref[halo_rows, :] = st_ref[0]

        @pl.when(i % tiles_per_seq != 0)
        def _():
            full_ref[halo_rows, :] = cp_ref[...] * up_ref[...]

    out = x_ref[...] + jnp.dot(att_ref[...].astype(BF16), woa_ref[...], preferred_element_type=F32)
    cu = c_ref[...] * u_ref[...]
    for s in range(n_sub):
        rows = slice(s * sub_len, (s + 1) * sub_len)
        if tiles_per_seq == 1:
            full_ref[halo_rows, :] = st_ref[s]
        cur = cu[rows, :]
        y = b_ref[rows, :] * _dwconv3_rows(cur, wc_ref, full_ref, all_cols)
        for g in range(N_GROUPS):
            cols = slice(g * GROUP, (g + 1) * GROUP)
            yc_ref[rows, cols] = _rms_rows(y[:, cols], gc_ref[:, cols]).astype(BF16)
        if tiles_per_seq == 1:
            nc_ref[i * n_sub + s] = cur[sub_len - (CONV_K - 1):, :]

    o_ref[...] = out + jnp.dot(yc_ref[...], wob_ref[...], preferred_element_type=F32)

    if tiles_per_seq > 1:
        @pl.when(i % tiles_per_seq == tiles_per_seq - 1)
        def _():
            last = SUBLANES + sub_len
            nc_ref[i // tiles_per_seq] = full_ref[last - (CONV_K - 1):last, :]


def _state_rows(state):
    return jnp.pad(state, ((0, 0), (SUBLANES - (CONV_K - 1), 0), (0, 0)))


def _mix(att2d, proj, state, w_conv, g_conv, w_o_bf16, x2d, *, tm, seq_len):
    m = x2d.shape[0]
    n_seq = m // seq_len
    sub_len, n_sub, tiles_per_seq = _seq_geometry(tm, seq_len)
    rows_per_block = tm // SUBLANES
    col0 = 3 * ATT_WIDTH // CONV_WIDTH
    prev_map = lambda c: (lambda i: (jnp.maximum(i * rows_per_block - 1, 0), c))
    kernel = functools.partial(_mix_kernel, tm=tm, seq_len=seq_len)
    return pl.pallas_call(
        kernel,
        out_shape=(jax.ShapeDtypeStruct((m, D_MODEL), F32),
                   jax.ShapeDtypeStruct((n_seq, CONV_K - 1, CONV_WIDTH), F32)),
        grid=(m // tm,),
        in_specs=[
            pl.BlockSpec((tm, ATT_WIDTH), lambda i: (i, 0)),
            pl.BlockSpec((tm, CONV_WIDTH), lambda i: (i, col0)),
            pl.BlockSpec((tm, CONV_WIDTH), lambda i: (i, col0 + 1)),
            pl.BlockSpec((tm, CONV_WIDTH), lambda i: (i, col0 + 2)),
            pl.BlockSpec((SUBLANES, CONV_WIDTH), prev_map(col0 + 1)),
            pl.BlockSpec((SUBLANES, CONV_WIDTH), prev_map(col0 + 2)),
            pl.BlockSpec((n_sub, SUBLANES, CONV_WIDTH), lambda i: (i // tiles_per_seq, 0, 0)),
            pl.BlockSpec((CONV_K, CONV_WIDTH), lambda i: (0, 0)),
            pl.BlockSpec((1, CONV_WIDTH), lambda i: (0, 0)),
            pl.BlockSpec((ATT_WIDTH, D_MODEL), lambda i: (0, 0)),
            pl.BlockSpec((CONV_WIDTH, D_MODEL), lambda i: (1, 0)),
            pl.BlockSpec((tm, D_MODEL), lambda i: (i, 0)),
        ],
        out_specs=(pl.BlockSpec((tm, D_MODEL), lambda i: (i, 0)),
                   pl.BlockSpec((n_seq, CONV_K - 1, CONV_WIDTH), lambda i: (0, 0, 0))),
        scratch_shapes=[pltpu.VMEM((SUBLANES + sub_len, CONV_WIDTH), F32),
                        pltpu.VMEM((tm, CONV_WIDTH), BF16)],
        compiler_params=_params(("arbitrary",)),
        name="mix_out_proj",
    )(att2d, proj, proj, proj, proj, proj, _state_rows(state), w_conv, g_conv.reshape(1, CONV_WIDTH),
      w_o_bf16, w_o_bf16, x2d)


def _ffn_kernel(x_ref, g_ref, wg_ref, wu_ref, wc_ref, st_ref, wd_ref, gf_ref, o_ref, nf_ref,
                h_ref, halo_ref, full_ref, act_ref, *, tm, seq_len, n_ff):
    i = pl.program_id(0)
    j = pl.program_id(1)
    sub_len, n_sub, tiles_per_seq = _seq_geometry(tm, seq_len)
    tf = wg_ref.shape[1]
    half = tf // 2

    @pl.when(j == 0)
    def _():
        x = x_ref[...]
        h_ref[...] = _rms_rows(x, g_ref[...]).astype(BF16)
        o_ref[...] = x

    halo_rows = slice(0, SUBLANES)
    if tiles_per_seq > 1:
        @pl.when(i % tiles_per_seq == 0)
        def _():
            halo_ref[j] = st_ref[0]

    h = h_ref[...]
    for c in range(2):
        cols = slice(c * half, (c + 1) * half)
        gate = jnp.dot(h, wg_ref[:, cols], preferred_element_type=F32)
        up = jnp.dot(h, wu_ref[:, cols], preferred_element_type=F32)
        for s in range(n_sub):
            rows = slice(s * sub_len, (s + 1) * sub_len)
            if tiles_per_seq == 1:
                full_ref[halo_rows, cols] = st_ref[s, :, cols]
            else:
                full_ref[halo_rows, cols] = halo_ref[j, :, cols]
            cur = gate[rows, :]
            gc = _dwconv3_rows(cur, wc_ref, full_ref, cols)
            act_ref[rows, cols] = (gc * jax.nn.sigmoid(gc) * up[rows, :]).astype(BF16)
            if tiles_per_seq == 1:
                nf_ref[i * n_sub + s, j, :, cols] = cur[sub_len - (CONV_K - 1):, :]
            else:
                halo_ref[j, :, cols] = cur[sub_len - SUBLANES:, :]

    o_ref[...] += jnp.dot(act_ref[...], wd_ref[...], preferred_element_type=F32)

    if tiles_per_seq > 1:
        @pl.when(i % tiles_per_seq == tiles_per_seq - 1)
        def _():
            nf_ref[i // tiles_per_seq, j] = halo_ref[j, SUBLANES - (CONV_K - 1):, :]

    @pl.when(j == n_ff - 1)
    def _():
        o_ref[...] = _rms_rows(o_ref[...], gf_ref[...])


def _ffn(x2d, g_ffn, w_gate_up_bf16, w_ffn_conv, state, w_down_bf16, g_final, *, tm, seq_len):
    m = x2d.shape[0]
    n_seq = m // seq_len
    tf = FF_TILE
    n_ff = D_FF // tf
    sub_len, n_sub, tiles_per_seq = _seq_geometry(tm, seq_len)
    kernel = functools.partial(_ffn_kernel, tm=tm, seq_len=seq_len, n_ff=n_ff)
    f32_b, bf16_b = 4, 2
    vmem_bytes = (4 * tm * D_MODEL * f32_b + 2 * 3 * D_MODEL * tf * bf16_b + tm * D_MODEL * bf16_b
                  + tm * tf * bf16_b + (SUBLANES + sub_len) * tf * f32_b + 4 * tm * tf * f32_b)
    y, new_ffn = pl.pallas_call(
        kernel,
        out_shape=(jax.ShapeDtypeStruct((m, D_MODEL), F32),
                   jax.ShapeDtypeStruct((n_seq, n_ff, CONV_K - 1, tf), F32)),
        grid=(m // tm, n_ff),
        in_specs=[
            pl.BlockSpec((tm, D_MODEL), lambda i, j: (i, 0)),
            pl.BlockSpec((1, D_MODEL), lambda i, j: (0, 0)),
            pl.BlockSpec((D_MODEL, tf), lambda i, j: (0, j)),
            pl.BlockSpec((D_MODEL, tf), lambda i, j: (0, j + n_ff)),
            pl.BlockSpec((CONV_K, tf), lambda i, j: (0, j)),
            pl.BlockSpec((n_sub, SUBLANES, tf), lambda i, j: (i // tiles_per_seq, 0, j)),
            pl.BlockSpec((tf, D_MODEL), lambda i, j: (j, 0)),
            pl.BlockSpec((1, D_MODEL), lambda i, j: (0, 0)),
        ],
        out_specs=(pl.BlockSpec((tm, D_MODEL), lambda i, j: (i, 0)),
                   pl.BlockSpec((n_seq, n_ff, CONV_K - 1, tf), lambda i, j: (0, 0, 0, 0))),
        scratch_shapes=[pltpu.VMEM((tm, D_MODEL), BF16),
                        pltpu.VMEM((n_ff, SUBLANES, tf), F32),
                        pltpu.VMEM((SUBLANES + sub_len, tf), F32),
                        pltpu.VMEM((tm, tf), BF16)],
        compiler_params=_params(("arbitrary", "arbitrary"), vmem_bytes),
        name="conv_ffn",
    )(x2d, g_ffn.reshape(1, D_MODEL), w_gate_up_bf16, w_gate_up_bf16, w_ffn_conv, _state_rows(state),
      w_down_bf16, g_final.reshape(1, D_MODEL))
    return y, new_ffn.transpose(0, 2, 1, 3).reshape(n_seq, CONV_K - 1, D_FF)


def _tiles(batch, seq_len):
    rows = batch * seq_len
    if rows >= 4096:
        return dict(tm_proj=1024, tm_mix=256, tm_ffn=512, tq=KV_CHUNK)
    return dict(tm_proj=rows, tm_mix=rows, tm_ffn=rows, tq=LANES)


def _trunk(x, cache_k, cache_v, state_conv, state_ffn, weights):
    g_mix, w_in, w_conv, g_att, g_conv, w_o, g_ffn, w_gate_up, w_ffn_conv, w_down, g_final = weights
    b, t, _ = x.shape
    past = cache_k.shape[1]
    tiles = _tiles(b, t)
    tq = tiles["tq"]
    x2d = x.reshape(b * t, D_MODEL)

    proj = _in_proj(x2d, g_mix, w_in, tm=tiles["tm_proj"])
    proj3 = proj.reshape(b, t, IN_WIDTH)
    tq_pad = -(-t // tq) * tq
    tk_pad = -(-(past + tq_pad) // KV_CHUNK) * KV_CHUNK
    if past == 0 and tq_pad == t and tk_pad == t:
        q_h, k_h, v_h, k, v = _qkv_layouts(proj3)
    else:
        k = proj3[:, :, ATT_WIDTH:2 * ATT_WIDTH]
        v = proj3[:, :, 2 * ATT_WIDTH:3 * ATT_WIDTH]

        def key_heads(cache, new, transpose):
            tail = jnp.pad(new, ((0, 0), (0, tk_pad - past - t), (0, 0)))
            return _to_heads(cache, tail, permute=True, transpose=transpose)

        q_pad = jnp.pad(proj3[:, :, :ATT_WIDTH], ((0, 0), (0, tq_pad - t), (0, 0)))
        q_h = _to_heads(jnp.zeros((b, 0, N_HEADS, HEAD_DIM), F32), q_pad, permute=False, transpose=True)
        k_h = key_heads(cache_k, k, False)
        v_h = key_heads(cache_v, v, True)
    att = _attention(q_h, k_h, v_h, g_att, past=past, tq=tq)
    att2d = att[:, :t, :].reshape(b * t, ATT_WIDTH)

    x1, new_conv = _mix(att2d, proj, state_conv, w_conv, g_conv, w_o, x2d, tm=tiles["tm_mix"], seq_len=t)
    y, new_ffn = _ffn(x1, g_ffn, w_gate_up, w_ffn_conv, state_ffn, w_down, g_final, tm=tiles["tm_ffn"], seq_len=t)
    shape4 = (b, t, N_HEADS, HEAD_DIM)
    return y.reshape(b, t, D_MODEL), k.reshape(shape4), v.reshape(shape4), new_conv, new_ffn


def kernel(x_prompt, x_sample, cache_k, cache_v, state_conv, state_ffn_conv, g_mix, w_in, w_conv, g_att_out,
           g_conv_out, w_o, g_ffn, w_gate_up, w_ffn_conv, w_down, g_final):
    depth = g_mix.shape[0]
    assert depth == 1
    weights = (g_mix[0], w_in[0].astype(BF16), w_conv[0], g_att_out[0], g_conv_out[0], w_o[0].astype(BF16),
               g_ffn[0], w_gate_up[0].astype(BF16), w_ffn_conv[0], w_down[0].astype(BF16), g_final)
    bp = x_prompt.shape[0]
    dt = x_prompt.dtype
    empty = jnp.zeros((bp, 0, N_HEADS, HEAD_DIM), dt)
    y_p, k_p, v_p, c_p, f_p = _trunk(
        x_prompt, empty, empty, jnp.zeros((bp, CONV_K - 1, CONV_WIDTH), dt), jnp.zeros((bp, CONV_K - 1, D_FF), dt),
        weights)
    y_s, k_s, v_s, c_s, f_s = _trunk(x_sample, cache_k[0], cache_v[0], state_conv[0], state_ffn_conv[0], weights)
    return (y_p, y_s, k_p[None], v_p[None], c_p[None], f_p[None], k_s[None], v_s[None], c_s[None], f_s[None])
```

```python
import functools
import math

import jax
import jax.numpy as jnp
import numpy as np
from jax import lax
from jax.experimental import pallas as pl
from jax.experimental.pallas import tpu as pltpu

F32 = jnp.float32
BF16 = jnp.bfloat16

D_MODEL = 2048
N_HEADS = 8
HEAD_DIM = 128
ATT_WIDTH = N_HEADS * HEAD_DIM
N_GROUPS = 8
GROUP = 128
CONV_WIDTH = N_GROUPS * GROUP
IN_WIDTH = 3 * ATT_WIDTH + 3 * CONV_WIDTH
D_FF = 5632
CONV_K = 3
EPS = 1e-6
Q_SCALE = (HEAD_DIM ** -0.5) * math.log2(math.e)

SUBLANES = 8
LANES = 128
SEGS = 2 * SUBLANES
SEG_LEN = 32
KV_CHUNK = SEGS * SEG_LEN
VMEM_LIMIT_BYTES = 48 * 1024 * 1024

FF_TILE = 512
IN_PROJ_COL_TILE = 1024
assert ATT_WIDTH == IN_PROJ_COL_TILE and CONV_WIDTH == IN_PROJ_COL_TILE


def _params(sem, vmem_limit_bytes=VMEM_LIMIT_BYTES):
    return pltpu.CompilerParams(dimension_semantics=sem, vmem_limit_bytes=vmem_limit_bytes)


def _rms_rows(x, g):
    ms = jnp.mean(x * x, axis=-1, keepdims=True)
    return (x * lax.rsqrt(ms + EPS)) * g


def _key_row_permutation():
    p = np.zeros((KV_CHUNK, KV_CHUNK), np.float32)
    for i in range(SEG_LEN):
        for s in range(SEGS):
            p[SEGS * i + s, SEG_LEN * s + SEG_LEN - 1 - i] = 1.0
    return jnp.asarray(p, BF16)


def _in_proj_kernel(x_ref, g_ref, w_ref, o_ref, h_ref):
    j = pl.program_id(1)

    @pl.when(j == 0)
    def _():
        h_ref[...] = _rms_rows(x_ref[...], g_ref[...]).astype(BF16)

    acc = jnp.dot(h_ref[...], w_ref[...], preferred_element_type=F32)
    o_ref[...] = acc * jnp.where(j == 0, Q_SCALE, 1.0).astype(F32)


def _in_proj(x2d, g_mix, w_in_bf16, *, tm):
    m = x2d.shape[0]
    tn = IN_PROJ_COL_TILE
    return pl.pallas_call(
        _in_proj_kernel,
        out_shape=jax.ShapeDtypeStruct((m, IN_WIDTH), F32),
        grid=(m // tm, IN_WIDTH // tn),
        in_specs=[
            pl.BlockSpec((tm, D_MODEL), lambda i, j: (i, 0)),
            pl.BlockSpec((1, D_MODEL), lambda i, j: (0, 0)),
            pl.BlockSpec((D_MODEL, tn), lambda i, j: (0, j)),
        ],
        out_specs=pl.BlockSpec((tm, tn), lambda i, j: (i, j)),
        scratch_shapes=[pltpu.VMEM((tm, D_MODEL), BF16)],
        compiler_params=_params(("arbitrary", "arbitrary")),
        name="in_proj",
    )(x2d, g_mix.reshape(1, D_MODEL), w_in_bf16)


def _store_heads(x32, p_ref, o_ref, *, permute, transpose):
    if permute:
        x32 = jnp.dot(p_ref[...], x32.astype(BF16), preferred_element_type=F32)
    for h in range(N_HEADS):
        xh = x32[:, h * HEAD_DIM:(h + 1) * HEAD_DIM]
        o_ref[0, h] = (xh.T if transpose else xh).astype(BF16)


def _max_key_norm2(k32):
    kb = k32.astype(BF16).astype(F32)
    out = []
    for h in range(N_HEADS):
        kh = kb[:, h * HEAD_DIM:(h + 1) * HEAD_DIM]
        ss = jnp.sum(kh * kh, axis=1, keepdims=True)
        out.append(jnp.broadcast_to(jnp.max(ss, axis=0, keepdims=True), (1, LANES)))
    return jnp.concatenate(out, axis=0)


def _to_heads_kernel(p_ref, x_ref, tail_ref, o_ref, *kn_ref, permute, transpose, n_src_blocks):
    src32 = jnp.concatenate([x_ref[0, :, h, :] for h in range(N_HEADS)], axis=1)
    x32 = jnp.where(pl.program_id(1) < n_src_blocks, src32, tail_ref[0])
    for ref in kn_ref:
        ref[0, 0] = _max_key_norm2(x32)
    _store_heads(x32, p_ref, o_ref, permute=permute, transpose=transpose)


def _qkv_layouts_kernel(p_ref, x_ref, qT_ref, k_ref, vT_ref, k32_ref, v32_ref, kn_ref):
    q32 = x_ref[0, :, 0:ATT_WIDTH]
    k32 = x_ref[0, :, ATT_WIDTH:2 * ATT_WIDTH]
    v32 = x_ref[0, :, 2 * ATT_WIDTH:3 * ATT_WIDTH]
    kn_ref[0, 0] = _max_key_norm2(k32)
    k32_ref[0] = k32.reshape(k32.shape[0], N_HEADS, HEAD_DIM)
    v32_ref[0] = v32.reshape(v32.shape[0], N_HEADS, HEAD_DIM)
    _store_heads(q32, p_ref, qT_ref, permute=False, transpose=True)
    _store_heads(k32, p_ref, k_ref, permute=True, transpose=False)
    _store_heads(v32, p_ref, vT_ref, permute=True, transpose=True)


def _qkv_layouts(proj3):
    b, t, _ = proj3.shape
    rows = KV_CHUNK
    assert t % rows == 0
    heads_t = jax.ShapeDtypeStruct((b, N_HEADS, HEAD_DIM, t), BF16)
    heads = jax.ShapeDtypeStruct((b, N_HEADS, t, HEAD_DIM), BF16)
    plain = jax.ShapeDtypeStruct((b, t, N_HEADS, HEAD_DIM), F32)
    spec_t = pl.BlockSpec((1, N_HEADS, HEAD_DIM, rows), lambda bi, i: (bi, 0, 0, i))
    spec_h = pl.BlockSpec((1, N_HEADS, rows, HEAD_DIM), lambda bi, i: (bi, 0, i, 0))
    spec_p = pl.BlockSpec((1, rows, N_HEADS, HEAD_DIM), lambda bi, i: (bi, i, 0, 0))
    norms = jax.ShapeDtypeStruct((b, t // rows, N_HEADS, LANES), F32)
    spec_n = pl.BlockSpec((1, 1, N_HEADS, LANES), lambda bi, i: (bi, i, 0, 0))
    qT, k_h, vT, k32, v32, kn = pl.pallas_call(
        _qkv_layouts_kernel,
        out_shape=(heads_t, heads, heads_t, plain, plain, norms),
        grid=(b, t // rows),
        in_specs=[
            pl.BlockSpec((KV_CHUNK, KV_CHUNK), lambda bi, i: (0, 0)),
            pl.BlockSpec((1, rows, 3 * ATT_WIDTH), lambda bi, i: (bi, i, 0)),
        ],
        out_specs=(spec_t, spec_h, spec_t, spec_p, spec_p, spec_n),
        compiler_params=_params(("arbitrary", "arbitrary")),
        name="qkv_layouts",
    )(_key_row_permutation(), proj3)
    merge = lambda a: a.reshape((b * N_HEADS,) + a.shape[2:])
    return merge(qT), merge(k_h), merge(vT), k32, v32, kn


def _to_heads(src, tail, *, permute, transpose, key_norms=False):
    b, t_src = src.shape[:2]
    t = t_src + tail.shape[1]
    rows = KV_CHUNK if permute else math.gcd(math.gcd(t_src, tail.shape[1]), KV_CHUNK)
    assert t_src % rows == 0 and tail.shape[1] % rows == 0 and rows % LANES == 0
    n_src = t_src // rows
    if n_src == 0:
        src = tail.reshape(b, tail.shape[1], N_HEADS, HEAD_DIM)
    if transpose:
        out_shape = jax.ShapeDtypeStruct((b, N_HEADS, HEAD_DIM, t), BF16)
        out_spec = pl.BlockSpec((1, N_HEADS, HEAD_DIM, rows), lambda bi, i: (bi, 0, 0, i))
    else:
        out_shape = jax.ShapeDtypeStruct((b, N_HEADS, t, HEAD_DIM), BF16)
        out_spec = pl.BlockSpec((1, N_HEADS, rows, HEAD_DIM), lambda bi, i: (bi, 0, i, 0))
    if key_norms:
        out_shape = (out_shape, jax.ShapeDtypeStruct((b, t // rows, N_HEADS, LANES), F32))
        out_spec = (out_spec, pl.BlockSpec((1, 1, N_HEADS, LANES), lambda bi, i: (bi, i, 0, 0)))
    out = pl.pallas_call(
        functools.partial(_to_heads_kernel, permute=permute, transpose=transpose, n_src_blocks=n_src),
        out_shape=out_shape,
        grid=(b, t // rows),
        in_specs=[
            pl.BlockSpec((KV_CHUNK, KV_CHUNK), lambda bi, i: (0, 0)),
            pl.BlockSpec((1, rows, N_HEADS, HEAD_DIM), lambda bi, i: (bi, jnp.minimum(i, max(n_src - 1, 0)), 0, 0)),
            pl.BlockSpec((1, rows, ATT_WIDTH), lambda bi, i: (bi, jnp.maximum(i - n_src, 0), 0)),
        ],
        out_specs=out_spec,
        compiler_params=_params(("arbitrary", "arbitrary")),
        name="to_heads",
    )(_key_row_permutation(), src, tail)
    merge = lambda a: a.reshape((b * N_HEADS,) + a.shape[2:])
    return (merge(out[0]), out[1]) if key_norms else merge(out)


EXP2_UNDERFLOW = 160.0
SCORE_BOUND_SLACK = 1.01


def _softplus2(z):
    e = jnp.exp2(-jnp.abs(z))
    return jnp.maximum(z, 0.0) + jnp.log(1.0 + e) * math.log2(math.e)


def _attn_kernel(qT_ref, k_ref, vT_ref, kn_ref, g_ref, o_ref, z_ref, d_ref, w_ref, acc_ref, carry_ref,
                 *, past, tq):
    qi = pl.program_id(1)
    q_lo = past + qi * tq
    j_top = q_lo >> (KV_CHUNK.bit_length() - 1)

    qT = qT_ref[0]
    qpos = q_lo + lax.broadcasted_iota(jnp.int32, (SEGS, tq), 1)
    top_key = j_top * KV_CHUNK + SEG_LEN * lax.broadcasted_iota(jnp.int32, (SEGS, tq), 0) + (SEG_LEN - 1)

    def causal(i):
        return top_key - i < qpos

    def scores(j):
        base = pl.multiple_of(j * KV_CHUNK, KV_CHUNK)
        kc = k_ref[0, pl.ds(base, KV_CHUNK), :]
        return jnp.dot(kc, qT, preferred_element_type=F32)

    def add_values(j):
        base = pl.multiple_of(j * KV_CHUNK, KV_CHUNK)
        vc = vT_ref[0, :, pl.ds(base, KV_CHUNK)]
        acc_ref[...] += jnp.dot(vc, w_ref[...], preferred_element_type=F32)

    def visit(j, cur, masked):
        z_ref[1 - cur] = scores(jnp.maximum(j - 1, 0))
        if not masked:
            add_values(j + 1)
        run = jnp.zeros((SEGS, tq), F32)
        for i in range(SEG_LEN):
            rows = slice(i * SEGS, (i + 1) * SEGS)
            z = z_ref[cur, rows, :]
            sp = _softplus2(z)
            if masked:
                sp = jnp.where(causal(i), sp, 0.0)
            run = run + sp
            d_ref[rows, :] = z - run
        tail = carry_ref[0:1, :]
        offs = [None] * SEGS
        for s in reversed(range(SEGS)):
            offs[s] = tail
            tail = tail + run[s:s + 1, :]
        carry_ref[...] = jnp.broadcast_to(tail, carry_ref.shape)
        off = jnp.concatenate(offs, axis=0)
        for i in range(SEG_LEN):
            rows = slice(i * SEGS, (i + 1) * SEGS)
            w = jnp.exp2(d_ref[rows, :] - off)
            if masked:
                w = jnp.where(causal(i), w, 0.0)
            w_ref[rows, :] = w.astype(BF16)

    kn = jnp.max(kn_ref[0], axis=0)
    head = pl.program_id(0) % N_HEADS
    kn = jnp.where(lax.broadcasted_iota(jnp.int32, kn.shape, 0) == head, kn, 0.0)
    k_norm = jnp.sqrt(jnp.max(kn, axis=0, keepdims=True)[:, 0:1])

    q32 = qT.astype(F32)
    q_norm = jnp.sqrt(jnp.sum(q32 * q32, axis=0, keepdims=True))
    dead_at = q_norm * k_norm * SCORE_BOUND_SLACK + EXP2_UNDERFLOW

    def rest_is_zero():
        return jnp.min(carry_ref[0:1, :] - dead_at) >= 0.0

    acc_ref[...] = jnp.zeros_like(acc_ref)
    carry_ref[...] = jnp.zeros_like(carry_ref)
    z_ref[0] = scores(j_top)

    visit(j_top, 0, True)
    first = jnp.logical_and(j_top >= 1, jnp.logical_not(rest_is_zero()))

    @pl.when(first)
    def _():
        visit(j_top - 1, 1, False)

    n_pairs = jnp.maximum(j_top - 1, 0) >> 1

    def more_pairs(c):
        t, done = c
        return jnp.logical_and(t < n_pairs, jnp.logical_not(done))

    def pair(c):
        t, _ = c
        j = j_top - 2 - 2 * t
        visit(j, 0, False)
        visit(j - 1, 1, False)
        return t + 1, rest_is_zero()

    stop = jnp.logical_or(jnp.logical_not(first), rest_is_zero())
    t_end, done = lax.while_loop(more_pairs, pair, (jnp.int32(0), stop))
    odd_visit = jnp.logical_and(((j_top - 1) & 1) == 1, jnp.logical_not(done))

    @pl.when(odd_visit)
    def _():
        visit(0, 0, False)

    j_last = jnp.where(odd_visit, 0, jnp.where(first, j_top - 1 - 2 * t_end, j_top))
    add_values(j_last)

    out = acc_ref[...]
    ms = jnp.mean(out * out, axis=0, keepdims=True)
    y = (out * lax.rsqrt(ms + EPS)) * g_ref[0]
    o_ref[0] = y.T


def _attention(qT, k_perm, vT_perm, key_norms, g_att, *, past, tq):
    bh_total, _, t_q = qT.shape
    b = bh_total // N_HEADS
    t_k = k_perm.shape[1]
    n_chunks = t_k // KV_CHUNK
    assert KV_CHUNK % tq == 0 and past % KV_CHUNK == 0 and t_q % tq == 0
    assert t_k % KV_CHUNK == 0 and t_k >= -(-(past + t_q) // KV_CHUNK) * KV_CHUNK
    assert key_norms.shape == (b, n_chunks, N_HEADS, LANES)
    kernel = functools.partial(_attn_kernel, past=past, tq=tq)
    return pl.pallas_call(
        kernel,
        out_shape=jax.ShapeDtypeStruct((b, t_q, ATT_WIDTH), F32),
        grid=(b * N_HEADS, t_q // tq),
        in_specs=[
            pl.BlockSpec((1, HEAD_DIM, tq), lambda bh, i: (bh, 0, i)),
            pl.BlockSpec((1, t_k, HEAD_DIM), lambda bh, i: (bh, 0, 0)),
            pl.BlockSpec((1, HEAD_DIM, t_k), lambda bh, i: (bh, 0, 0)),
            pl.BlockSpec((1, n_chunks, N_HEADS, LANES), lambda bh, i: (bh // N_HEADS, 0, 0, 0)),
            pl.BlockSpec((1, HEAD_DIM, 1), lambda bh, i: (bh % N_HEADS, 0, 0)),
        ],
        out_specs=pl.BlockSpec((1, tq, HEAD_DIM), lambda bh, i: (bh // N_HEADS, i, bh % N_HEADS)),
        scratch_shapes=[
            pltpu.VMEM((2, KV_CHUNK, tq), F32),
            pltpu.VMEM((KV_CHUNK, tq), F32),
            pltpu.VMEM((KV_CHUNK, tq), BF16),
            pltpu.VMEM((HEAD_DIM, tq), F32),
            pltpu.VMEM((SUBLANES, tq), F32),
        ],
        compiler_params=_params(("arbitrary", "arbitrary")),
        name="sb_attention",
    )(qT, k_perm, vT_perm, key_norms, g_att.reshape(N_HEADS, HEAD_DIM, 1))


def _dwconv3_rows(cur, w_ref, full_ref, cols):
    n = cur.shape[0]
    full_ref[SUBLANES:SUBLANES + n, cols] = cur
    return (full_ref[SUBLANES - 2:SUBLANES - 2 + n, cols] * w_ref[0:1, cols]
            + full_ref[SUBLANES - 1:SUBLANES - 1 + n, cols] * w_ref[1:2, cols]
            + cur * w_ref[2:3, cols])


def _seq_geometry(tm, seq_len):
    sub_len = min(seq_len, tm)
    return sub_len, tm // sub_len, max(seq_len // tm, 1)


def _mix_kernel(att_ref, b_ref, c_ref, u_ref, cp_ref, up_ref, st_ref, wc_ref, gc_ref, woa_ref, wob_ref,
                x_ref, o_ref, nc_ref, full_ref, yc_ref, *, tm, seq_len):
    i = pl.program_id(0)
    sub_len, n_sub, tiles_per_seq = _seq_geometry(tm, seq_len)
    halo_rows = slice(0, SUBLANES)
    all_cols = slice(0, CONV_WIDTH)
    if tiles_per_seq > 1:
        @pl.when(i % tiles_per_seq == 0)
        def _():
            full_ref[halo_rows, :] = st_ref[0]

        @pl.when(i % tiles_per_seq != 0)
        def _():
            full_ref[halo_rows, :] = cp_ref[...] * up_ref[...]

    out = x_ref[...] + jnp.dot(att_ref[...].astype(BF16), woa_ref[...], preferred_element_type=F32)
    cu = c_ref[...] * u_ref[...]
    for s in range(n_sub):
        rows = slice(s * sub_len, (s + 1) * sub_len)
        if tiles_per_seq == 1:
            full_ref[halo_rows, :] = st_ref[s]
        cur = cu[rows, :]
        y = b_ref[rows, :] * _dwconv3_rows(cur, wc_ref, full_ref, all_cols)
        for g in range(N_GROUPS):
            cols = slice(g * GROUP, (g + 1) * GROUP)
            yc_ref[rows, cols] = _rms_rows(y[:, cols], gc_ref[:, cols]).astype(BF16)
        if tiles_per_seq == 1:
            nc_ref[i * n_sub + s] = cur[sub_len - (CONV_K - 1):, :]

    o_ref[...] = out + jnp.dot(yc_ref[...], wob_ref[...], preferred_element_type=F32)

    if tiles_per_seq > 1:
        @pl.when(i % tiles_per_seq == tiles_per_seq - 1)
        def _():
            last = SUBLANES + sub_len
            nc_ref[i // tiles_per_seq] = full_ref[last - (CONV_K - 1):last, :]


def _state_rows(state):
    return jnp.pad(state, ((0, 0), (SUBLANES - (CONV_K - 1), 0), (0, 0)))


def _mix(att2d, proj, state, w_conv, g_conv, w_o_bf16, x2d, *, tm, seq_len):
    m = x2d.shape[0]
    n_seq = m // seq_len
    sub_len, n_sub, tiles_per_seq = _seq_geometry(tm, seq_len)
    rows_per_block = tm // SUBLANES
    col0 = 3 * ATT_WIDTH // CONV_WIDTH
    prev_map = lambda c: (lambda i: (jnp.maximum(i * rows_per_block - 1, 0), c))
    kernel = functools.partial(_mix_kernel, tm=tm, seq_len=seq_len)
    return pl.pallas_call(
        kernel,
        out_shape=(jax.ShapeDtypeStruct((m, D_MODEL), F32),
                   jax.ShapeDtypeStruct((n_seq, CONV_K - 1, CONV_WIDTH), F32)),
        grid=(m // tm,),
        in_specs=[
            pl.BlockSpec((tm, ATT_WIDTH), lambda i: (i, 0)),
            pl.BlockSpec((tm, CONV_WIDTH), lambda i: (i, col0)),
            pl.BlockSpec((tm, CONV_WIDTH), lambda i: (i, col0 + 1)),
            pl.BlockSpec((tm, CONV_WIDTH), lambda i: (i, col0 + 2)),
            pl.BlockSpec((SUBLANES, CONV_WIDTH), prev_map(col0 + 1)),
            pl.BlockSpec((SUBLANES, CONV_WIDTH), prev_map(col0 + 2)),
            pl.BlockSpec((n_sub, SUBLANES, CONV_WIDTH), lambda i: (i // tiles_per_seq, 0, 0)),
            pl.BlockSpec((CONV_K, CONV_WIDTH), lambda i: (0, 0)),
            pl.BlockSpec((1, CONV_WIDTH), lambda i: (0, 0)),
            pl.BlockSpec((ATT_WIDTH, D_MODEL), lambda i: (0, 0)),
            pl.BlockSpec((CONV_WIDTH, D_MODEL), lambda i: (1, 0)),
            pl.BlockSpec((tm, D_MODEL), lambda i: (i, 0)),
        ],
        out_specs=(pl.BlockSpec((tm, D_MODEL), lambda i: (i, 0)),
                   pl.BlockSpec((n_seq, CONV_K - 1, CONV_WIDTH), lambda i: (0, 0, 0))),
        scratch_shapes=[pltpu.VMEM((SUBLANES + sub_len, CONV_WIDTH), F32),
                        pltpu.VMEM((tm, CONV_WIDTH), BF16)],
        compiler_params=_params(("arbitrary",)),
        name="mix_out_proj",
    )(att2d, proj, proj, proj, proj, proj, _state_rows(state), w_conv, g_conv.reshape(1, CONV_WIDTH),
      w_o_bf16, w_o_bf16, x2d)


def _ffn_kernel(x_ref, g_ref, wg_ref, wu_ref, wc_ref, st_ref, wd_ref, gf_ref, o_ref, nf_ref,
                h_ref, halo_ref, full_ref, act_ref, *, tm, seq_len, n_ff):
    i = pl.program_id(0)
    j = pl.program_id(1)
    sub_len, n_sub, tiles_per_seq = _seq_geometry(tm, seq_len)
    tf = wg_ref.shape[1]
    half = tf // 2

    @pl.when(j == 0)
    def _():
        x = x_ref[...]
        h_ref[...] = _rms_rows(x, g_ref[...]).astype(BF16)
        o_ref[...] = x

    halo_rows = slice(0, SUBLANES)
    if tiles_per_seq > 1:
        @pl.when(i % tiles_per_seq == 0)
        def _():
            halo_ref[j] = st_ref[0]

    h = h_ref[...]
    for c in range(2):
        cols = slice(c * half, (c + 1) * half)
        gate = jnp.dot(h, wg_ref[:, cols], preferred_element_type=F32)
        up = jnp.dot(h, wu_ref[:, cols], preferred_element_type=F32)
        for s in range(n_sub):
            rows = slice(s * sub_len, (s + 1) * sub_len)
            if tiles_per_seq == 1:
                full_ref[halo_rows, cols] = st_ref[s, :, cols]
            else:
                full_ref[halo_rows, cols] = halo_ref[j, :, cols]
            cur = gate[rows, :]
            gc = _dwconv3_rows(cur, wc_ref, full_ref, cols)
            act_ref[rows, cols] = (gc * jax.nn.sigmoid(gc) * up[rows, :]).astype(BF16)
            if tiles_per_seq == 1:
                nf_ref[i * n_sub + s, j, :, cols] = cur[sub_len - (CONV_K - 1):, :]
            else:
                halo_ref[j, :, cols] = cur[sub_len - SUBLANES:, :]

    o_ref[...] += jnp.dot(act_ref[...], wd_ref[...], preferred_element_type=F32)

    if tiles_per_seq > 1:
        @pl.when(i % tiles_per_seq == tiles_per_seq - 1)
        def _():
            nf_ref[i // tiles_per_seq, j] = halo_ref[j, SUBLANES - (CONV_K - 1):, :]

    @pl.when(j == n_ff - 1)
    def _():
        o_ref[...] = _rms_rows(o_ref[...], gf_ref[...])


def _ffn(x2d, g_ffn, w_gate_up_bf16, w_ffn_conv, state, w_down_bf16, g_final, *, tm, seq_len):
    m = x2d.shape[0]
    n_seq = m // seq_len
    tf = FF_TILE
    n_ff = D_FF // tf
    sub_len, n_sub, tiles_per_seq = _seq_geometry(tm, seq_len)
    kernel = functools.partial(_ffn_kernel, tm=tm, seq_len=seq_len, n_ff=n_ff)
    f32_b, bf16_b = 4, 2
    vmem_bytes = (4 * tm * D_MODEL * f32_b + 2 * 3 * D_MODEL * tf * bf16_b + tm * D_MODEL * bf16_b
                  + tm * tf * bf16_b + (SUBLANES + sub_len) * tf * f32_b + 4 * tm * tf * f32_b)
    y, new_ffn = pl.pallas_call(
        kernel,
        out_shape=(jax.ShapeDtypeStruct((m, D_MODEL), F32),
                   jax.ShapeDtypeStruct((n_seq, n_ff, CONV_K - 1, tf), F32)),
        grid=(m // tm, n_ff),
        in_specs=[
            pl.BlockSpec((tm, D_MODEL), lambda i, j: (i, 0)),
            pl.BlockSpec((1, D_MODEL), lambda i, j: (0, 0)),
            pl.BlockSpec((D_MODEL, tf), lambda i, j: (0, j)),
            pl.BlockSpec((D_MODEL, tf), lambda i, j: (0, j + n_ff)),
            pl.BlockSpec((CONV_K, tf), lambda i, j: (0, j)),
            pl.BlockSpec((n_sub, SUBLANES, tf), lambda i, j: (i // tiles_per_seq, 0, j)),
            pl.BlockSpec((tf, D_MODEL), lambda i, j: (j, 0)),
            pl.BlockSpec((1, D_MODEL), lambda i, j: (0, 0)),
        ],
        out_specs=(pl.BlockSpec((tm, D_MODEL), lambda i, j: (i, 0)),
                   pl.BlockSpec((n_seq, n_ff, CONV_K - 1, tf), lambda i, j: (0, 0, 0, 0))),
        scratch_shapes=[pltpu.VMEM((tm, D_MODEL), BF16),
                        pltpu.VMEM((n_ff, SUBLANES, tf), F32),
                        pltpu.VMEM((SUBLANES + sub_len, tf), F32),
                        pltpu.VMEM((tm, tf), BF16)],
        compiler_params=_params(("arbitrary", "arbitrary"), vmem_bytes),
        name="conv_ffn",
    )(x2d, g_ffn.reshape(1, D_MODEL), w_gate_up_bf16, w_gate_up_bf16, w_ffn_conv, _state_rows(state),
      w_down_bf16, g_final.reshape(1, D_MODEL))
    return y, new_ffn.transpose(0, 2, 1, 3).reshape(n_seq, CONV_K - 1, D_FF)


def _tiles(batch, seq_len):
    rows = batch * seq_len
    if rows >= 4096:
        return dict(tm_proj=1024, tm_mix=256, tm_ffn=512, tq=KV_CHUNK)
    return dict(tm_proj=rows, tm_mix=rows, tm_ffn=rows, tq=LANES)


def _trunk(x, cache_k, cache_v, state_conv, state_ffn, weights):
    g_mix, w_in, w_conv, g_att, g_conv, w_o, g_ffn, w_gate_up, w_ffn_conv, w_down, g_final = weights
    b, t, _ = x.shape
    past = cache_k.shape[1]
    tiles = _tiles(b, t)
    tq = tiles["tq"]
    x2d = x.reshape(b * t, D_MODEL)

    proj = _in_proj(x2d, g_mix, w_in, tm=tiles["tm_proj"])
    proj3 = proj.reshape(b, t, IN_WIDTH)
    tq_pad = -(-t // tq) * tq
    tk_pad = -(-(past + tq_pad) // KV_CHUNK) * KV_CHUNK
    if past == 0 and tq_pad == t and tk_pad == t:
        q_h, k_h, v_h, k, v, k_norms = _qkv_layouts(proj3)
    else:
        k = proj3[:, :, ATT_WIDTH:2 * ATT_WIDTH]
        v = proj3[:, :, 2 * ATT_WIDTH:3 * ATT_WIDTH]

        def key_heads(cache, new, **kw):
            tail = jnp.pad(new, ((0, 0), (0, tk_pad - past - t), (0, 0)))
            return _to_heads(cache, tail, permute=True, **kw)

        q_pad = jnp.pad(proj3[:, :, :ATT_WIDTH], ((0, 0), (0, tq_pad - t), (0, 0)))
        q_h = _to_heads(jnp.zeros((b, 0, N_HEADS, HEAD_DIM), F32), q_pad, permute=False, transpose=True)
        k_h, k_norms = key_heads(cache_k, k, transpose=False, key_norms=True)
        v_h = key_heads(cache_v, v, transpose=True)
    att = _attention(q_h, k_h, v_h, k_norms, g_att, past=past, tq=tq)
    att2d = att[:, :t, :].reshape(b * t, ATT_WIDTH)

    x1, new_conv = _mix(att2d, proj, state_conv, w_conv, g_conv, w_o, x2d, tm=tiles["tm_mix"], seq_len=t)
    y, new_ffn = _ffn(x1, g_ffn, w_gate_up, w_ffn_conv, state_ffn, w_down, g_final, tm=tiles["tm_ffn"], seq_len=t)
    shape4 = (b, t, N_HEADS, HEAD_DIM)
    return y.reshape(b, t, D_MODEL), k.reshape(shape4), v.reshape(shape4), new_conv, new_ffn


def kernel(x_prompt, x_sample, cache_k, cache_v, state_conv, state_ffn_conv, g_mix, w_in, w_conv, g_att_out,
           g_conv_out, w_o, g_ffn, w_gate_up, w_ffn_conv, w_down, g_final):
    depth = g_mix.shape[0]
    assert depth == 1
    weights = (g_mix[0], w_in[0].astype(BF16), w_conv[0], g_att_out[0], g_conv_out[0], w_o[0].astype(BF16),
               g_ffn[0], w_gate_up[0].astype(BF16), w_ffn_conv[0], w_down[0].astype(BF16), g_final)
    bp = x_prompt.shape[0]
    dt = x_prompt.dtype
    empty = jnp.zeros((bp, 0, N_HEADS, HEAD_DIM), dt)
    y_p, k_p, v_p, c_p, f_p = _trunk(
        x_prompt, empty, empty, jnp.zeros((bp, CONV_K - 1, CONV_WIDTH), dt), jnp.zeros((bp, CONV_K - 1, D_FF), dt),
        weights)
    y_s, k_s, v_s, c_s, f_s = _trunk(x_sample, cache_k[0], cache_v[0], state_conv[0], state_ffn_conv[0], weights)
    return (y_p, y_s, k_p[None], v_p[None], c_p[None], f_p[None], k_s[None], v_s[None], c_s[None], f_s[None])
```

```python
import functools
import math

import jax
import jax.numpy as jnp
import numpy as np
from jax import lax
from jax.experimental import pallas as pl
from jax.experimental.pallas import tpu as pltpu

F32 = jnp.float32
BF16 = jnp.bfloat16

D_MODEL = 2048
N_HEADS = 8
HEAD_DIM = 128
ATT_WIDTH = N_HEADS * HEAD_DIM
N_GROUPS = 8
GROUP = 128
CONV_WIDTH = N_GROUPS * GROUP
IN_WIDTH = 3 * ATT_WIDTH + 3 * CONV_WIDTH
D_FF = 5632
CONV_K = 3
EPS = 1e-6
Q_SCALE = (HEAD_DIM ** -0.5) * math.log2(math.e)

SUBLANES = 8
LANES = 128
SEGS = 2 * SUBLANES
SEG_LEN = 32
KV_CHUNK = SEGS * SEG_LEN
VMEM_LIMIT_BYTES = 48 * 1024 * 1024

FF_TILE = 512
IN_PROJ_COL_TILE = 1024
assert ATT_WIDTH == IN_PROJ_COL_TILE and CONV_WIDTH == IN_PROJ_COL_TILE


def _params(sem, vmem_limit_bytes=VMEM_LIMIT_BYTES):
    return pltpu.CompilerParams(dimension_semantics=sem, vmem_limit_bytes=vmem_limit_bytes)


def _rms_rows(x, g):
    ms = jnp.mean(x * x, axis=-1, keepdims=True)
    return (x * lax.rsqrt(ms + EPS)) * g


def _key_row_permutation():
    p = np.zeros((KV_CHUNK, KV_CHUNK), np.float32)
    for i in range(SEG_LEN):
        for s in range(SEGS):
            p[SEGS * i + s, SEG_LEN * s + SEG_LEN - 1 - i] = 1.0
    return jnp.asarray(p, BF16)


def _in_proj_kernel(x_ref, g_ref, w_ref, o_ref, h_ref):
    j = pl.program_id(1)

    @pl.when(j == 0)
    def _():
        h_ref[...] = _rms_rows(x_ref[...], g_ref[...]).astype(BF16)

    acc = jnp.dot(h_ref[...], w_ref[...], preferred_element_type=F32)
    o_ref[...] = acc * jnp.where(j == 0, Q_SCALE, 1.0).astype(F32)


def _in_proj(x2d, g_mix, w_in_bf16, *, tm):
    m = x2d.shape[0]
    tn = IN_PROJ_COL_TILE
    return pl.pallas_call(
        _in_proj_kernel,
        out_shape=jax.ShapeDtypeStruct((m, IN_WIDTH), F32),
        grid=(m // tm, IN_WIDTH // tn),
        in_specs=[
            pl.BlockSpec((tm, D_MODEL), lambda i, j: (i, 0)),
            pl.BlockSpec((1, D_MODEL), lambda i, j: (0, 0)),
            pl.BlockSpec((D_MODEL, tn), lambda i, j: (0, j)),
        ],
        out_specs=pl.BlockSpec((tm, tn), lambda i, j: (i, j)),
        scratch_shapes=[pltpu.VMEM((tm, D_MODEL), BF16)],
        compiler_params=_params(("arbitrary", "arbitrary")),
        name="in_proj",
    )(x2d, g_mix.reshape(1, D_MODEL), w_in_bf16)


def _store_heads(x32, p_ref, o_ref, *, permute, transpose):
    if permute:
        x32 = jnp.dot(p_ref[...], x32.astype(BF16), preferred_element_type=F32)
    for h in range(N_HEADS):
        xh = x32[:, h * HEAD_DIM:(h + 1) * HEAD_DIM]
        o_ref[0, h] = (xh.T if transpose else xh).astype(BF16)


def _max_key_norm2(k32):
    kb = k32.astype(BF16).astype(F32)
    out = []
    for h in range(N_HEADS):
        kh = kb[:, h * HEAD_DIM:(h + 1) * HEAD_DIM]
        ss = jnp.sum(kh * kh, axis=1, keepdims=True)
        out.append(jnp.broadcast_to(jnp.max(ss, axis=0, keepdims=True), (1, LANES)))
    return jnp.concatenate(out, axis=0)


def _to_heads_kernel(p_ref, x_ref, tail_ref, o_ref, *kn_ref, permute, transpose, n_src_blocks):
    src32 = jnp.concatenate([x_ref[0, :, h, :] for h in range(N_HEADS)], axis=1)
    x32 = jnp.where(pl.program_id(1) < n_src_blocks, src32, tail_ref[0])
    for ref in kn_ref:
        ref[0, 0] = _max_key_norm2(x32)
    _store_heads(x32, p_ref, o_ref, permute=permute, transpose=transpose)


def _qkv_layouts_kernel(p_ref, x_ref, qT_ref, k_ref, vT_ref, k32_ref, v32_ref, kn_ref):
    q32 = x_ref[0, :, 0:ATT_WIDTH]
    k32 = x_ref[0, :, ATT_WIDTH:2 * ATT_WIDTH]
    v32 = x_ref[0, :, 2 * ATT_WIDTH:3 * ATT_WIDTH]
    kn_ref[0, 0] = _max_key_norm2(k32)
    k32_ref[0] = k32.reshape(k32.shape[0], N_HEADS, HEAD_DIM)
    v32_ref[0] = v32.reshape(v32.shape[0], N_HEADS, HEAD_DIM)
    _store_heads(q32, p_ref, qT_ref, permute=False, transpose=True)
    _store_heads(k32, p_ref, k_ref, permute=True, transpose=False)
    _store_heads(v32, p_ref, vT_ref, permute=True, transpose=True)


def _qkv_layouts(proj3):
    b, t, _ = proj3.shape
    rows = KV_CHUNK
    assert t % rows == 0
    heads_t = jax.ShapeDtypeStruct((b, N_HEADS, HEAD_DIM, t), BF16)
    heads = jax.ShapeDtypeStruct((b, N_HEADS, t, HEAD_DIM), BF16)
    plain = jax.ShapeDtypeStruct((b, t, N_HEADS, HEAD_DIM), F32)
    spec_t = pl.BlockSpec((1, N_HEADS, HEAD_DIM, rows), lambda bi, i: (bi, 0, 0, i))
    spec_h = pl.BlockSpec((1, N_HEADS, rows, HEAD_DIM), lambda bi, i: (bi, 0, i, 0))
    spec_p = pl.BlockSpec((1, rows, N_HEADS, HEAD_DIM), lambda bi, i: (bi, i, 0, 0))
    norms = jax.ShapeDtypeStruct((b, t // rows, N_HEADS, LANES), F32)
    spec_n = pl.BlockSpec((1, 1, N_HEADS, LANES), lambda bi, i: (bi, i, 0, 0))
    qT, k_h, vT, k32, v32, kn = pl.pallas_call(
        _qkv_layouts_kernel,
        out_shape=(heads_t, heads, heads_t, plain, plain, norms),
        grid=(b, t // rows),
        in_specs=[
            pl.BlockSpec((KV_CHUNK, KV_CHUNK), lambda bi, i: (0, 0)),
            pl.BlockSpec((1, rows, 3 * ATT_WIDTH), lambda bi, i: (bi, i, 0)),
        ],
        out_specs=(spec_t, spec_h, spec_t, spec_p, spec_p, spec_n),
        compiler_params=_params(("arbitrary", "arbitrary")),
        name="qkv_layouts",
    )(_key_row_permutation(), proj3)
    merge = lambda a: a.reshape((b * N_HEADS,) + a.shape[2:])
    return merge(qT), merge(k_h), merge(vT), k32, v32, kn


def _to_heads(src, tail, *, permute, transpose, key_norms=False):
    b, t_src = src.shape[:2]
    t = t_src + tail.shape[1]
    rows = KV_CHUNK if permute else math.gcd(math.gcd(t_src, tail.shape[1]), KV_CHUNK)
    assert t_src % rows == 0 and tail.shape[1] % rows == 0 and rows % LANES == 0
    n_src = t_src // rows
    if n_src == 0:
        src = tail.reshape(b, tail.shape[1], N_HEADS, HEAD_DIM)
    if transpose:
        out_shape = jax.ShapeDtypeStruct((b, N_HEADS, HEAD_DIM, t), BF16)
        out_spec = pl.BlockSpec((1, N_HEADS, HEAD_DIM, rows), lambda bi, i: (bi, 0, 0, i))
    else:
        out_shape = jax.ShapeDtypeStruct((b, N_HEADS, t, HEAD_DIM), BF16)
        out_spec = pl.BlockSpec((1, N_HEADS, rows, HEAD_DIM), lambda bi, i: (bi, 0, i, 0))
    if key_norms:
        out_shape = (out_shape, jax.ShapeDtypeStruct((b, t // rows, N_HEADS, LANES), F32))
        out_spec = (out_spec, pl.BlockSpec((1, 1, N_HEADS, LANES), lambda bi, i: (bi, i, 0, 0)))
    out = pl.pallas_call(
        functools.partial(_to_heads_kernel, permute=permute, transpose=transpose, n_src_blocks=n_src),
        out_shape=out_shape,
        grid=(b, t // rows),
        in_specs=[
            pl.BlockSpec((KV_CHUNK, KV_CHUNK), lambda bi, i: (0, 0)),
            pl.BlockSpec((1, rows, N_HEADS, HEAD_DIM), lambda bi, i: (bi, jnp.minimum(i, max(n_src - 1, 0)), 0, 0)),
            pl.BlockSpec((1, rows, ATT_WIDTH), lambda bi, i: (bi, jnp.maximum(i - n_src, 0), 0)),
        ],
        out_specs=out_spec,
        compiler_params=_params(("arbitrary", "arbitrary")),
        name="to_heads",
    )(_key_row_permutation(), src, tail)
    merge = lambda a: a.reshape((b * N_HEADS,) + a.shape[2:])
    return (merge(out[0]), out[1]) if key_norms else merge(out)


EXP2_UNDERFLOW = 160.0
SCORE_BOUND_SLACK = 1.01


def _softplus2(z):
    e = jnp.exp2(-jnp.abs(z))
    return jnp.maximum(z, 0.0) + jnp.log(1.0 + e) * math.log2(math.e)


def _attn_kernel(qT_ref, k_ref, vT_ref, kn_ref, g_ref, o_ref, z_ref, d_ref, w_ref, acc_ref, carry_ref,
                 *, past, tq):
    qi = pl.program_id(1)
    q_lo = past + qi * tq
    j_top = q_lo >> (KV_CHUNK.bit_length() - 1)

    qT = qT_ref[0]
    qpos = q_lo + lax.broadcasted_iota(jnp.int32, (SEGS, tq), 1)
    top_key = j_top * KV_CHUNK + SEG_LEN * lax.broadcasted_iota(jnp.int32, (SEGS, tq), 0) + (SEG_LEN - 1)

    def causal(i):
        return top_key - i < qpos

    def scores(j):
        base = pl.multiple_of(j * KV_CHUNK, KV_CHUNK)
        kc = k_ref[0, pl.ds(base, KV_CHUNK), :]
        return jnp.dot(kc, qT, preferred_element_type=F32)

    def add_values(j):
        base = pl.multiple_of(j * KV_CHUNK, KV_CHUNK)
        vc = vT_ref[0, :, pl.ds(base, KV_CHUNK)]
        acc_ref[...] += jnp.dot(vc, w_ref[...], preferred_element_type=F32)

    def visit(j, cur, masked):
        z_ref[1 - cur] = scores(jnp.maximum(j - 1, 0))
        if not masked:
            add_values(j + 1)
        run = jnp.zeros((SEGS, tq), F32)
        for i in range(SEG_LEN):
            rows = slice(i * SEGS, (i + 1) * SEGS)
            z = z_ref[cur, rows, :]
            sp = _softplus2(z)
            if masked:
                sp = jnp.where(causal(i), sp, 0.0)
            run = run + sp
            d_ref[rows, :] = z - run
        tail = carry_ref[0:1, :]
        offs = [None] * SEGS
        for s in reversed(range(SEGS)):
            offs[s] = tail
            tail = tail + run[s:s + 1, :]
        carry_ref[...] = jnp.broadcast_to(tail, carry_ref.shape)
        off = jnp.concatenate(offs, axis=0)
        for i in range(SEG_LEN):
            rows = slice(i * SEGS, (i + 1) * SEGS)
            w = jnp.exp2(d_ref[rows, :] - off)
            if masked:
                w = jnp.where(causal(i), w, 0.0)
            w_ref[rows, :] = w.astype(BF16)

    kn = jnp.max(kn_ref[0], axis=0)
    head = pl.program_id(0) % N_HEADS
    kn = jnp.where(lax.broadcasted_iota(jnp.int32, kn.shape, 0) == head, kn, 0.0)
    k_norm = jnp.sqrt(jnp.max(kn, axis=0, keepdims=True)[:, 0:1])

    q32 = qT.astype(F32)
    q_norm = jnp.sqrt(jnp.sum(q32 * q32, axis=0, keepdims=True))
    dead_at = q_norm * k_norm * SCORE_BOUND_SLACK + EXP2_UNDERFLOW

    def rest_is_zero():
        return jnp.min(carry_ref[0:1, :] - dead_at) >= 0.0

    acc_ref[...] = jnp.zeros_like(acc_ref)
    carry_ref[...] = jnp.zeros_like(carry_ref)
    z_ref[0] = scores(j_top)

    visit(j_top, 0, True)
    if tq == KV_CHUNK:
        first = j_top >= 1
    else:
        first = jnp.logical_and(j_top >= 1, jnp.logical_not(rest_is_zero()))

    @pl.when(first)
    def _():
        visit(j_top - 1, 1, False)

    n_pairs = jnp.maximum(j_top - 1, 0) >> 1

    def more_pairs(c):
        t, done = c
        return jnp.logical_and(t < n_pairs, jnp.logical_not(done))

    def pair(c):
        t, _ = c
        j = j_top - 2 - 2 * t
        visit(j, 0, False)
        visit(j - 1, 1, False)
        return t + 1, rest_is_zero()

    stop = jnp.logical_or(jnp.logical_not(first), rest_is_zero())
    t_end, done = lax.while_loop(more_pairs, pair, (jnp.int32(0), stop))
    odd_visit = jnp.logical_and(((j_top - 1) & 1) == 1, jnp.logical_not(done))

    @pl.when(odd_visit)
    def _():
        visit(0, 0, False)

    j_last = jnp.where(odd_visit, 0, jnp.where(first, j_top - 1 - 2 * t_end, j_top))
    add_values(j_last)

    out = acc_ref[...]
    ms = jnp.mean(out * out, axis=0, keepdims=True)
    y = (out * lax.rsqrt(ms + EPS)) * g_ref[0]
    o_ref[0] = y.T


def _attention(qT, k_perm, vT_perm, key_norms, g_att, *, past, tq):
    bh_total, _, t_q = qT.shape
    b = bh_total // N_HEADS
    t_k = k_perm.shape[1]
    n_chunks = t_k // KV_CHUNK
    assert KV_CHUNK % tq == 0 and past % KV_CHUNK == 0 and t_q % tq == 0
    assert t_k % KV_CHUNK == 0 and t_k >= -(-(past + t_q) // KV_CHUNK) * KV_CHUNK
    assert key_norms.shape == (b, n_chunks, N_HEADS, LANES)
    kernel = functools.partial(_attn_kernel, past=past, tq=tq)
    return pl.pallas_call(
        kernel,
        out_shape=jax.ShapeDtypeStruct((b, t_q, ATT_WIDTH), F32),
        grid=(b * N_HEADS, t_q // tq),
        in_specs=[
            pl.BlockSpec((1, HEAD_DIM, tq), lambda bh, i: (bh, 0, i)),
            pl.BlockSpec((1, t_k, HEAD_DIM), lambda bh, i: (bh, 0, 0)),
            pl.BlockSpec((1, HEAD_DIM, t_k), lambda bh, i: (bh, 0, 0)),
            pl.BlockSpec((1, n_chunks, N_HEADS, LANES), lambda bh, i: (bh // N_HEADS, 0, 0, 0)),
            pl.BlockSpec((1, HEAD_DIM, 1), lambda bh, i: (bh % N_HEADS, 0, 0)),
        ],
        out_specs=pl.BlockSpec((1, tq, HEAD_DIM), lambda bh, i: (bh // N_HEADS, i, bh % N_HEADS)),
        scratch_shapes=[
            pltpu.VMEM((2, KV_CHUNK, tq), F32),
            pltpu.VMEM((KV_CHUNK, tq), F32),
            pltpu.VMEM((KV_CHUNK, tq), BF16),
            pltpu.VMEM((HEAD_DIM, tq), F32),
            pltpu.VMEM((SUBLANES, tq), F32),
        ],
        compiler_params=_params(("arbitrary", "arbitrary")),
        name="sb_attention",
    )(qT, k_perm, vT_perm, key_norms, g_att.reshape(N_HEADS, HEAD_DIM, 1))


def _dwconv3_rows(cur, w_ref, full_ref, cols):
    n = cur.shape[0]
    full_ref[SUBLANES:SUBLANES + n, cols] = cur
    return (full_ref[SUBLANES - 2:SUBLANES - 2 + n, cols] * w_ref[0:1, cols]
            + full_ref[SUBLANES - 1:SUBLANES - 1 + n, cols] * w_ref[1:2, cols]
            + cur * w_ref[2:3, cols])


def _seq_geometry(tm, seq_len):
    sub_len = min(seq_len, tm)
    return sub_len, tm // sub_len, max(seq_len // tm, 1)


def _mix_kernel(att_ref, b_ref, c_ref, u_ref, cp_ref, up_ref, st_ref, wc_ref, gc_ref, woa_ref, wob_ref,
                x_ref, o_ref, nc_ref, full_ref, yc_ref, *, tm, seq_len):
    i = pl.program_id(0)
    sub_len, n_sub, tiles_per_seq = _seq_geometry(tm, seq_len)
    halo_rows = slice(0, SUBLANES)
    all_cols = slice(0, CONV_WIDTH)
    if tiles_per_seq > 1:
        @pl.when(i % tiles_per_seq == 0)
        def _():
            full_ref[halo_rows, :] = st_ref[0]

        @pl.when(i % tiles_per_seq != 0)
        def _():
            full_ref[halo_rows, :] = cp_ref[...] * up_ref[...]

    out = x_ref[...] + jnp.dot(att_ref[...].astype(BF16), woa_ref[...], preferred_element_type=F32)
    cu = c_ref[...] * u_ref[...]
    for s in range(n_sub):
        rows = slice(s * sub_len, (s + 1) * sub_len)
        if tiles_per_seq == 1:
            full_ref[halo_rows, :] = st_ref[s]
        cur = cu[rows, :]
        y = b_ref[rows, :] * _dwconv3_rows(cur, wc_ref, full_ref, all_cols)
        for g in range(N_GROUPS):
            cols = slice(g * GROUP, (g + 1) * GROUP)
            yc_ref[rows, cols] = _rms_rows(y[:, cols], gc_ref[:, cols]).astype(BF16)
        if tiles_per_seq == 1:
            nc_ref[i * n_sub + s] = cur[sub_len - (CONV_K - 1):, :]

    o_ref[...] = out + jnp.dot(yc_ref[...], wob_ref[...], preferred_element_type=F32)

    if tiles_per_seq > 1:
        @pl.when(i % tiles_per_seq == tiles_per_seq - 1)
        def _():
            last = SUBLANES + sub_len
            nc_ref[i // tiles_per_seq] = full_ref[last - (CONV_K - 1):last, :]


def _state_rows(state):
    return jnp.pad(state, ((0, 0), (SUBLANES - (CONV_K - 1), 0), (0, 0)))


def _mix(att2d, proj, state, w_conv, g_conv, w_o_bf16, x2d, *, tm, seq_len):
    m = x2d.shape[0]
    n_seq = m // seq_len
    sub_len, n_sub, tiles_per_seq = _seq_geometry(tm, seq_len)
    rows_per_block = tm // SUBLANES
    col0 = 3 * ATT_WIDTH // CONV_WIDTH
    prev_map = lambda c: (lambda i: (jnp.maximum(i * rows_per_block - 1, 0), c))
    kernel = functools.partial(_mix_kernel, tm=tm, seq_len=seq_len)
    return pl.pallas_call(
        kernel,
        out_shape=(jax.ShapeDtypeStruct((m, D_MODEL), F32),
                   jax.ShapeDtypeStruct((n_seq, CONV_K - 1, CONV_WIDTH), F32)),
        grid=(m // tm,),
        in_specs=[
            pl.BlockSpec((tm, ATT_WIDTH), lambda i: (i, 0)),
            pl.BlockSpec((tm, CONV_WIDTH), lambda i: (i, col0)),
            pl.BlockSpec((tm, CONV_WIDTH), lambda i: (i, col0 + 1)),
            pl.BlockSpec((tm, CONV_WIDTH), lambda i: (i, col0 + 2)),
            pl.BlockSpec((SUBLANES, CONV_WIDTH), prev_map(col0 + 1)),
            pl.BlockSpec((SUBLANES, CONV_WIDTH), prev_map(col0 + 2)),
            pl.BlockSpec((n_sub, SUBLANES, CONV_WIDTH), lambda i: (i // tiles_per_seq, 0, 0)),
            pl.BlockSpec((CONV_K, CONV_WIDTH), lambda i: (0, 0)),
            pl.BlockSpec((1, CONV_WIDTH), lambda i: (0, 0)),
            pl.BlockSpec((ATT_WIDTH, D_MODEL), lambda i: (0, 0)),
            pl.BlockSpec((CONV_WIDTH, D_MODEL), lambda i: (1, 0)),
            pl.BlockSpec((tm, D_MODEL), lambda i: (i, 0)),
        ],
        out_specs=(pl.BlockSpec((tm, D_MODEL), lambda i: (i, 0)),
                   pl.BlockSpec((n_seq, CONV_K - 1, CONV_WIDTH), lambda i: (0, 0, 0))),
        scratch_shapes=[pltpu.VMEM((SUBLANES + sub_len, CONV_WIDTH), F32),
                        pltpu.VMEM((tm, CONV_WIDTH), BF16)],
        compiler_params=_params(("arbitrary",)),
        name="mix_out_proj",
    )(att2d, proj, proj, proj, proj, proj, _state_rows(state), w_conv, g_conv.reshape(1, CONV_WIDTH),
      w_o_bf16, w_o_bf16, x2d)


def _ffn_kernel(x_ref, g_ref, wg_ref, wu_ref, wc_ref, st_ref, wd_ref, gf_ref, o_ref, nf_ref,
                h_ref, halo_ref, full_ref, act_ref, *, tm, seq_len, n_ff):
    i = pl.program_id(0)
    j = pl.program_id(1)
    sub_len, n_sub, tiles_per_seq = _seq_geometry(tm, seq_len)
    tf = wg_ref.shape[1]
    half = tf // 2

    @pl.when(j == 0)
    def _():
        x = x_ref[...]
        h_ref[...] = _rms_rows(x, g_ref[...]).astype(BF16)
        o_ref[...] = x

    halo_rows = slice(0, SUBLANES)
    if tiles_per_seq > 1:
        @pl.when(i % tiles_per_seq == 0)
        def _():
            halo_ref[j] = st_ref[0]

    h = h_ref[...]
    for c in range(2):
        cols = slice(c * half, (c + 1) * half)
        gate = jnp.dot(h, wg_ref[:, cols], preferred_element_type=F32)
        up = jnp.dot(h, wu_ref[:, cols], preferred_element_type=F32)
        for s in range(n_sub):
            rows = slice(s * sub_len, (s + 1) * sub_len)
            if tiles_per_seq == 1:
                full_ref[halo_rows, cols] = st_ref[s, :, cols]
            else:
                full_ref[halo_rows, cols] = halo_ref[j, :, cols]
            cur = gate[rows, :]
            gc = _dwconv3_rows(cur, wc_ref, full_ref, cols)
            act_ref[rows, cols] = (gc * jax.nn.sigmoid(gc) * up[rows, :]).astype(BF16)
            if tiles_per_seq == 1:
                nf_ref[i * n_sub + s, j, :, cols] = cur[sub_len - (CONV_K - 1):, :]
            else:
                halo_ref[j, :, cols] = cur[sub_len - SUBLANES:, :]

    o_ref[...] += jnp.dot(act_ref[...], wd_ref[...], preferred_element_type=F32)

    if tiles_per_seq > 1:
        @pl.when(i % tiles_per_seq == tiles_per_seq - 1)
        def _():
            nf_ref[i // tiles_per_seq, j] = halo_ref[j, SUBLANES - (CONV_K - 1):, :]

    @pl.when(j == n_ff - 1)
    def _():
        o_ref[...] = _rms_rows(o_ref[...], gf_ref[...])


def _ffn(x2d, g_ffn, w_gate_up_bf16, w_ffn_conv, state, w_down_bf16, g_final, *, tm, seq_len):
    m = x2d.shape[0]
    n_seq = m // seq_len
    tf = FF_TILE
    n_ff = D_FF // tf
    sub_len, n_sub, tiles_per_seq = _seq_geometry(tm, seq_len)
    kernel = functools.partial(_ffn_kernel, tm=tm, seq_len=seq_len, n_ff=n_ff)
    f32_b, bf16_b = 4, 2
    vmem_bytes = (4 * tm * D_MODEL * f32_b + 2 * 3 * D_MODEL * tf * bf16_b + tm * D_MODEL * bf16_b
                  + tm * tf * bf16_b + (SUBLANES + sub_len) * tf * f32_b + 4 * tm * tf * f32_b)
    y, new_ffn = pl.pallas_call(
        kernel,
        out_shape=(jax.ShapeDtypeStruct((m, D_MODEL), F32),
                   jax.ShapeDtypeStruct((n_seq, n_ff, CONV_K - 1, tf), F32)),
        grid=(m // tm, n_ff),
        in_specs=[
            pl.BlockSpec((tm, D_MODEL), lambda i, j: (i, 0)),
            pl.BlockSpec((1, D_MODEL), lambda i, j: (0, 0)),
            pl.BlockSpec((D_MODEL, tf), lambda i, j: (0, j)),
            pl.BlockSpec((D_MODEL, tf), lambda i, j: (0, j + n_ff)),
            pl.BlockSpec((CONV_K, tf), lambda i, j: (0, j)),
            pl.BlockSpec((n_sub, SUBLANES, tf), lambda i, j: (i // tiles_per_seq, 0, j)),
            pl.BlockSpec((tf, D_MODEL), lambda i, j: (j, 0)),
            pl.BlockSpec((1, D_MODEL), lambda i, j: (0, 0)),
        ],
        out_specs=(pl.BlockSpec((tm, D_MODEL), lambda i, j: (i, 0)),
                   pl.BlockSpec((n_seq, n_ff, CONV_K - 1, tf), lambda i, j: (0, 0, 0, 0))),
        scratch_shapes=[pltpu.VMEM((tm, D_MODEL), BF16),
                        pltpu.VMEM((n_ff, SUBLANES, tf), F32),
                        pltpu.VMEM((SUBLANES + sub_len, tf), F32),
                        pltpu.VMEM((tm, tf), BF16)],
        compiler_params=_params(("arbitrary", "arbitrary"), vmem_bytes),
        name="conv_ffn",
    )(x2d, g_ffn.reshape(1, D_MODEL), w_gate_up_bf16, w_gate_up_bf16, w_ffn_conv, _state_rows(state),
      w_down_bf16, g_final.reshape(1, D_MODEL))
    return y, new_ffn.transpose(0, 2, 1, 3).reshape(n_seq, CONV_K - 1, D_FF)


def _tiles(batch, seq_len):
    rows = batch * seq_len
    if rows >= 4096:
        return dict(tm_proj=1024, tm_mix=256, tm_ffn=512, tq=KV_CHUNK)
    return dict(tm_proj=rows, tm_mix=rows, tm_ffn=rows, tq=LANES)


def _trunk(x, cache_k, cache_v, state_conv, state_ffn, weights):
    g_mix, w_in, w_conv, g_att, g_conv, w_o, g_ffn, w_gate_up, w_ffn_conv, w_down, g_final = weights
    b, t, _ = x.shape
    past = cache_k.shape[1]
    tiles = _tiles(b, t)
    tq = tiles["tq"]
    x2d = x.reshape(b * t, D_MODEL)

    proj = _in_proj(x2d, g_mix, w_in, tm=tiles["tm_proj"])
    proj3 = proj.reshape(b, t, IN_WIDTH)
    tq_pad = -(-t // tq) * tq
    tk_pad = -(-(past + tq_pad) // KV_CHUNK) * KV_CHUNK
    if past == 0 and tq_pad == t and tk_pad == t:
        q_h, k_h, v_h, k, v, k_norms = _qkv_layouts(proj3)
    else:
        k = proj3[:, :, ATT_WIDTH:2 * ATT_WIDTH]
        v = proj3[:, :, 2 * ATT_WIDTH:3 * ATT_WIDTH]

        def key_heads(cache, new, **kw):
            tail = jnp.pad(new, ((0, 0), (0, tk_pad - past - t), (0, 0)))
            return _to_heads(cache, tail, permute=True, **kw)

        q_pad = jnp.pad(proj3[:, :, :ATT_WIDTH], ((0, 0), (0, tq_pad - t), (0, 0)))
        q_h = _to_heads(jnp.zeros((b, 0, N_HEADS, HEAD_DIM), F32), q_pad, permute=False, transpose=True)
        k_h, k_norms = key_heads(cache_k, k, transpose=False, key_norms=True)
        v_h = key_heads(cache_v, v, transpose=True)
    att = _attention(q_h, k_h, v_h, k_norms, g_att, past=past, tq=tq)
    att2d = att[:, :t, :].reshape(b * t, ATT_WIDTH)

    x1, new_conv = _mix(att2d, proj, state_conv, w_conv, g_conv, w_o, x2d, tm=tiles["tm_mix"], seq_len=t)
    y, new_ffn = _ffn(x1, g_ffn, w_gate_up, w_ffn_conv, state_ffn, w_down, g_final, tm=tiles["tm_ffn"], seq_len=t)
    shape4 = (b, t, N_HEADS, HEAD_DIM)
    return y.reshape(b, t, D_MODEL), k.reshape(shape4), v.reshape(shape4), new_conv, new_ffn


def kernel(x_prompt, x_sample, cache_k, cache_v, state_conv, state_ffn_conv, g_mix, w_in, w_conv, g_att_out,
           g_conv_out, w_o, g_ffn, w_gate_up, w_ffn_conv, w_down, g_final):
    depth = g_mix.shape[0]
    assert depth == 1
    weights = (g_mix[0], w_in[0].astype(BF16), w_conv[0], g_att_out[0], g_conv_out[0], w_o[0].astype(BF16),
               g_ffn[0], w_gate_up[0].astype(BF16), w_ffn_conv[0], w_down[0].astype(BF16), g_final)
    bp = x_prompt.shape[0]
    dt = x_prompt.dtype
    empty = jnp.zeros((bp, 0, N_HEADS, HEAD_DIM), dt)
    y_p, k_p, v_p, c_p, f_p = _trunk(
        x_prompt, empty, empty, jnp.zeros((bp, CONV_K - 1, CONV_WIDTH), dt), jnp.zeros((bp, CONV_K - 1, D_FF), dt),
        weights)
    y_s, k_s, v_s, c_s, f_s = _trunk(x_sample, cache_k[0], cache_v[0], state_conv[0], state_ffn_conv[0], weights)
    return (y_p, y_s, k_p[None], v_p[None], c_p[None], f_p[None], k_s[None], v_s[None], c_s[None], f_s[None])
```

```python
import functools
import math

import jax
import jax.numpy as jnp
import numpy as np
from jax import lax
from jax.experimental import pallas as pl
from jax.experimental.pallas import tpu as pltpu

F32 = jnp.float32
BF16 = jnp.bfloat16

D_MODEL = 2048
N_HEADS = 8
HEAD_DIM = 128
ATT_WIDTH = N_HEADS * HEAD_DIM
N_GROUPS = 8
GROUP = 128
CONV_WIDTH = N_GROUPS * GROUP
IN_WIDTH = 3 * ATT_WIDTH + 3 * CONV_WIDTH
D_FF = 5632
CONV_K = 3
EPS = 1e-6
Q_SCALE = (HEAD_DIM ** -0.5) * math.log2(math.e)

SUBLANES = 8
LANES = 128
SEGS = 2 * SUBLANES
SEG_LEN = 32
KV_CHUNK = SEGS * SEG_LEN
VMEM_LIMIT_BYTES = 48 * 1024 * 1024

FF_TILE = 512
IN_PROJ_COL_TILE = 1024
assert ATT_WIDTH == IN_PROJ_COL_TILE and CONV_WIDTH == IN_PROJ_COL_TILE


def _params(sem, vmem_limit_bytes=VMEM_LIMIT_BYTES):
    return pltpu.CompilerParams(dimension_semantics=sem, vmem_limit_bytes=vmem_limit_bytes)


def _rms_rows(x, g):
    ms = jnp.mean(x * x, axis=-1, keepdims=True)
    return (x * lax.rsqrt(ms + EPS)) * g


def _key_row_permutation():
    p = np.zeros((KV_CHUNK, KV_CHUNK), np.float32)
    for i in range(SEG_LEN):
        for s in range(SEGS):
            p[SEGS * i + s, SEG_LEN * s + SEG_LEN - 1 - i] = 1.0
    return jnp.asarray(p, BF16)


def _in_proj_kernel(x_ref, g_ref, w_ref, o_ref, h_ref):
    j = pl.program_id(1)

    @pl.when(j == 0)
    def _():
        h_ref[...] = _rms_rows(x_ref[...], g_ref[...]).astype(BF16)

    acc = jnp.dot(h_ref[...], w_ref[...], preferred_element_type=F32)
    o_ref[...] = acc * jnp.where(j == 0, Q_SCALE, 1.0).astype(F32)


def _in_proj(x2d, g_mix, w_in_bf16, *, tm):
    m = x2d.shape[0]
    tn = IN_PROJ_COL_TILE
    return pl.pallas_call(
        _in_proj_kernel,
        out_shape=jax.ShapeDtypeStruct((m, IN_WIDTH), F32),
        grid=(m // tm, IN_WIDTH // tn),
        in_specs=[
            pl.BlockSpec((tm, D_MODEL), lambda i, j: (i, 0)),
            pl.BlockSpec((1, D_MODEL), lambda i, j: (0, 0)),
            pl.BlockSpec((D_MODEL, tn), lambda i, j: (0, j)),
        ],
        out_specs=pl.BlockSpec((tm, tn), lambda i, j: (i, j)),
        scratch_shapes=[pltpu.VMEM((tm, D_MODEL), BF16)],
        compiler_params=_params(("arbitrary", "arbitrary")),
        name="in_proj",
    )(x2d, g_mix.reshape(1, D_MODEL), w_in_bf16)


def _store_heads(x32, p_ref, o_ref, *, permute, transpose):
    if permute:
        x32 = jnp.dot(p_ref[...], x32.astype(BF16), preferred_element_type=F32)
    for h in range(N_HEADS):
        xh = x32[:, h * HEAD_DIM:(h + 1) * HEAD_DIM]
        o_ref[0, h] = (xh.T if transpose else xh).astype(BF16)


def _max_key_norm2(k32):
    kb = k32.astype(BF16).astype(F32)
    out = []
    for h in range(N_HEADS):
        kh = kb[:, h * HEAD_DIM:(h + 1) * HEAD_DIM]
        ss = jnp.sum(kh * kh, axis=1, keepdims=True)
        out.append(jnp.broadcast_to(jnp.max(ss, axis=0, keepdims=True), (1, LANES)))
    return jnp.concatenate(out, axis=0)


def _to_heads_kernel(p_ref, x_ref, tail_ref, o_ref, *kn_ref, permute, transpose, n_src_blocks):
    src32 = x_ref[0].reshape(x_ref.shape[1], ATT_WIDTH)
    x32 = jnp.where(pl.program_id(1) < n_src_blocks, src32, tail_ref[0])
    for ref in kn_ref:
        ref[0, 0] = _max_key_norm2(x32)
    _store_heads(x32, p_ref, o_ref, permute=permute, transpose=transpose)


def _qkv_layouts_kernel(p_ref, x_ref, qT_ref, k_ref, vT_ref, k32_ref, v32_ref, kn_ref):
    q32 = x_ref[0, :, 0:ATT_WIDTH]
    k32 = x_ref[0, :, ATT_WIDTH:2 * ATT_WIDTH]
    v32 = x_ref[0, :, 2 * ATT_WIDTH:3 * ATT_WIDTH]
    kn_ref[0, 0] = _max_key_norm2(k32)
    k32_ref[0] = k32.reshape(k32.shape[0], N_HEADS, HEAD_DIM)
    v32_ref[0] = v32.reshape(v32.shape[0], N_HEADS, HEAD_DIM)
    _store_heads(q32, p_ref, qT_ref, permute=False, transpose=True)
    _store_heads(k32, p_ref, k_ref, permute=True, transpose=False)
    _store_heads(v32, p_ref, vT_ref, permute=True, transpose=True)


def _qkv_layouts(proj3):
    b, t, _ = proj3.shape
    rows = KV_CHUNK
    assert t % rows == 0
    heads_t = jax.ShapeDtypeStruct((b, N_HEADS, HEAD_DIM, t), BF16)
    heads = jax.ShapeDtypeStruct((b, N_HEADS, t, HEAD_DIM), BF16)
    plain = jax.ShapeDtypeStruct((b, t, N_HEADS, HEAD_DIM), F32)
    spec_t = pl.BlockSpec((1, N_HEADS, HEAD_DIM, rows), lambda bi, i: (bi, 0, 0, i))
    spec_h = pl.BlockSpec((1, N_HEADS, rows, HEAD_DIM), lambda bi, i: (bi, 0, i, 0))
    spec_p = pl.BlockSpec((1, rows, N_HEADS, HEAD_DIM), lambda bi, i: (bi, i, 0, 0))
    norms = jax.ShapeDtypeStruct((b, t // rows, N_HEADS, LANES), F32)
    spec_n = pl.BlockSpec((1, 1, N_HEADS, LANES), lambda bi, i: (bi, i, 0, 0))
    qT, k_h, vT, k32, v32, kn = pl.pallas_call(
        _qkv_layouts_kernel,
        out_shape=(heads_t, heads, heads_t, plain, plain, norms),
        grid=(b, t // rows),
        in_specs=[
            pl.BlockSpec((KV_CHUNK, KV_CHUNK), lambda bi, i: (0, 0)),
            pl.BlockSpec((1, rows, 3 * ATT_WIDTH), lambda bi, i: (bi, i, 0)),
        ],
        out_specs=(spec_t, spec_h, spec_t, spec_p, spec_p, spec_n),
        compiler_params=_params(("arbitrary", "arbitrary")),
        name="qkv_layouts",
    )(_key_row_permutation(), proj3)
    merge = lambda a: a.reshape((b * N_HEADS,) + a.shape[2:])
    return merge(qT), merge(k_h), merge(vT), k32, v32, kn


def _to_heads(src, tail, *, permute, transpose, key_norms=False):
    b, t_src = src.shape[:2]
    t = t_src + tail.shape[1]
    rows = KV_CHUNK if permute else math.gcd(math.gcd(t_src, tail.shape[1]), KV_CHUNK)
    assert t_src % rows == 0 and tail.shape[1] % rows == 0 and rows % LANES == 0
    n_src = t_src // rows
    if n_src == 0:
        src = tail.reshape(b, tail.shape[1], N_HEADS, HEAD_DIM)
    if transpose:
        out_shape = jax.ShapeDtypeStruct((b, N_HEADS, HEAD_DIM, t), BF16)
        out_spec = pl.BlockSpec((1, N_HEADS, HEAD_DIM, rows), lambda bi, i: (bi, 0, 0, i))
    else:
        out_shape = jax.ShapeDtypeStruct((b, N_HEADS, t, HEAD_DIM), BF16)
        out_spec = pl.BlockSpec((1, N_HEADS, rows, HEAD_DIM), lambda bi, i: (bi, 0, i, 0))
    if key_norms:
        out_shape = (out_shape, jax.ShapeDtypeStruct((b, t // rows, N_HEADS, LANES), F32))
        out_spec = (out_spec, pl.BlockSpec((1, 1, N_HEADS, LANES), lambda bi, i: (bi, i, 0, 0)))
    out = pl.pallas_call(
        functools.partial(_to_heads_kernel, permute=permute, transpose=transpose, n_src_blocks=n_src),
        out_shape=out_shape,
        grid=(b, t // rows),
        in_specs=[
            pl.BlockSpec((KV_CHUNK, KV_CHUNK), lambda bi, i: (0, 0)),
            pl.BlockSpec((1, rows, N_HEADS, HEAD_DIM), lambda bi, i: (bi, jnp.minimum(i, max(n_src - 1, 0)), 0, 0)),
            pl.BlockSpec((1, rows, ATT_WIDTH), lambda bi, i: (bi, jnp.maximum(i - n_src, 0), 0)),
        ],
        out_specs=out_spec,
        compiler_params=_params(("arbitrary", "arbitrary")),
        name="to_heads",
    )(_key_row_permutation(), src, tail)
    merge = lambda a: a.reshape((b * N_HEADS,) + a.shape[2:])
    return (merge(out[0]), out[1]) if key_norms else merge(out)


EXP2_UNDERFLOW = 160.0
SCORE_BOUND_SLACK = 1.01


def _softplus2(z):
    e = jnp.exp2(-jnp.abs(z))
    return jnp.maximum(z, 0.0) + jnp.log(1.0 + e) * math.log2(math.e)


def _attn_kernel(qT_ref, k_ref, vT_ref, kn_ref, g_ref, o_ref, z_ref, d_ref, w_ref, acc_ref, carry_ref,
                 *, past, tq):
    qi = pl.program_id(1)
    q_lo = past + qi * tq
    j_top = q_lo >> (KV_CHUNK.bit_length() - 1)

    qT = qT_ref[0]
    qpos = q_lo + lax.broadcasted_iota(jnp.int32, (SEGS, tq), 1)
    top_key = j_top * KV_CHUNK + SEG_LEN * lax.broadcasted_iota(jnp.int32, (SEGS, tq), 0) + (SEG_LEN - 1)

    def causal(i):
        return top_key - i < qpos

    def scores(j):
        base = pl.multiple_of(j * KV_CHUNK, KV_CHUNK)
        kc = k_ref[0, pl.ds(base, KV_CHUNK), :]
        return jnp.dot(kc, qT, preferred_element_type=F32)

    def add_values(j):
        base = pl.multiple_of(j * KV_CHUNK, KV_CHUNK)
        vc = vT_ref[0, :, pl.ds(base, KV_CHUNK)]
        acc_ref[...] += jnp.dot(vc, w_ref[...], preferred_element_type=F32)

    def visit(j, cur, masked):
        z_ref[1 - cur] = scores(jnp.maximum(j - 1, 0))
        if not masked:
            add_values(j + 1)
        run = jnp.zeros((SEGS, tq), F32)
        for i in range(SEG_LEN):
            rows = slice(i * SEGS, (i + 1) * SEGS)
            z = z_ref[cur, rows, :]
            sp = _softplus2(z)
            if masked:
                sp = jnp.where(causal(i), sp, 0.0)
            run = run + sp
            d_ref[rows, :] = z - run
        tail = carry_ref[0:1, :]
        offs = [None] * SEGS
        for s in reversed(range(SEGS)):
            offs[s] = tail
            tail = tail + run[s:s + 1, :]
        carry_ref[...] = jnp.broadcast_to(tail, carry_ref.shape)
        off = jnp.concatenate(offs, axis=0)
        for i in range(SEG_LEN):
            rows = slice(i * SEGS, (i + 1) * SEGS)
            w = jnp.exp2(d_ref[rows, :] - off)
            if masked:
                w = jnp.where(causal(i), w, 0.0)
            w_ref[rows, :] = w.astype(BF16)

    kn = jnp.max(kn_ref[0], axis=0)
    head = pl.program_id(0) % N_HEADS
    kn = jnp.where(lax.broadcasted_iota(jnp.int32, kn.shape, 0) == head, kn, 0.0)
    k_norm = jnp.sqrt(jnp.max(kn, axis=0, keepdims=True)[:, 0:1])

    q32 = qT.astype(F32)
    q_norm = jnp.sqrt(jnp.sum(q32 * q32, axis=0, keepdims=True))
    dead_at = q_norm * k_norm * SCORE_BOUND_SLACK + EXP2_UNDERFLOW

    def rest_is_zero():
        return jnp.min(carry_ref[0:1, :] - dead_at) >= 0.0

    acc_ref[...] = jnp.zeros_like(acc_ref)
    carry_ref[...] = jnp.zeros_like(carry_ref)
    z_ref[0] = scores(j_top)

    visit(j_top, 0, True)
    if tq == KV_CHUNK:
        first = j_top >= 1
    else:
        first = jnp.logical_and(j_top >= 1, jnp.logical_not(rest_is_zero()))

    @pl.when(first)
    def _():
        visit(j_top - 1, 1, False)

    n_pairs = jnp.maximum(j_top - 1, 0) >> 1

    def more_pairs(c):
        t, done = c
        return jnp.logical_and(t < n_pairs, jnp.logical_not(done))

    def pair(c):
        t, _ = c
        j = j_top - 2 - 2 * t
        visit(j, 0, False)
        visit(j - 1, 1, False)
        return t + 1, rest_is_zero()

    stop = jnp.logical_or(jnp.logical_not(first), rest_is_zero())
    t_end, done = lax.while_loop(more_pairs, pair, (jnp.int32(0), stop))
    odd_visit = jnp.logical_and(((j_top - 1) & 1) == 1, jnp.logical_not(done))

    @pl.when(odd_visit)
    def _():
        visit(0, 0, False)

    j_last = jnp.where(odd_visit, 0, jnp.where(first, j_top - 1 - 2 * t_end, j_top))
    add_values(j_last)

    out = acc_ref[...]
    ms = jnp.mean(out * out, axis=0, keepdims=True)
    y = (out * lax.rsqrt(ms + EPS)) * g_ref[0]
    o_ref[0] = y.T


def _attention(qT, k_perm, vT_perm, key_norms, g_att, *, past, tq):
    bh_total, _, t_q = qT.shape
    b = bh_total // N_HEADS
    t_k = k_perm.shape[1]
    n_chunks = t_k // KV_CHUNK
    assert KV_CHUNK % tq == 0 and past % KV_CHUNK == 0 and t_q % tq == 0
    assert t_k % KV_CHUNK == 0 and t_k >= -(-(past + t_q) // KV_CHUNK) * KV_CHUNK
    assert key_norms.shape == (b, n_chunks, N_HEADS, LANES)
    kernel = functools.partial(_attn_kernel, past=past, tq=tq)
    return pl.pallas_call(
        kernel,
        out_shape=jax.ShapeDtypeStruct((b, t_q, ATT_WIDTH), F32),
        grid=(b * N_HEADS, t_q // tq),
        in_specs=[
            pl.BlockSpec((1, HEAD_DIM, tq), lambda bh, i: (bh, 0, i)),
            pl.BlockSpec((1, t_k, HEAD_DIM), lambda bh, i: (bh, 0, 0)),
            pl.BlockSpec((1, HEAD_DIM, t_k), lambda bh, i: (bh, 0, 0)),
            pl.BlockSpec((1, n_chunks, N_HEADS, LANES), lambda bh, i: (bh // N_HEADS, 0, 0, 0)),
            pl.BlockSpec((1, HEAD_DIM, 1), lambda bh, i: (bh % N_HEADS, 0, 0)),
        ],
        out_specs=pl.BlockSpec((1, tq, HEAD_DIM), lambda bh, i: (bh // N_HEADS, i, bh % N_HEADS)),
        scratch_shapes=[
            pltpu.VMEM((2, KV_CHUNK, tq), F32),
            pltpu.VMEM((KV_CHUNK, tq), F32),
            pltpu.VMEM((KV_CHUNK, tq), BF16),
            pltpu.VMEM((HEAD_DIM, tq), F32),
            pltpu.VMEM((SUBLANES, tq), F32),
        ],
        compiler_params=_params(("arbitrary", "arbitrary")),
        name="sb_attention",
    )(qT, k_perm, vT_perm, key_norms, g_att.reshape(N_HEADS, HEAD_DIM, 1))


def _dwconv3_rows(cur, w_ref, full_ref, cols):
    n = cur.shape[0]
    full_ref[SUBLANES:SUBLANES + n, cols] = cur
    return (full_ref[SUBLANES - 2:SUBLANES - 2 + n, cols] * w_ref[0:1, cols]
            + full_ref[SUBLANES - 1:SUBLANES - 1 + n, cols] * w_ref[1:2, cols]
            + cur * w_ref[2:3, cols])


def _seq_geometry(tm, seq_len):
    sub_len = min(seq_len, tm)
    return sub_len, tm // sub_len, max(seq_len // tm, 1)


def _mix_kernel(att_ref, b_ref, c_ref, u_ref, cp_ref, up_ref, st_ref, wc_ref, gc_ref, woa_ref, wob_ref,
                x_ref, o_ref, nc_ref, full_ref, yc_ref, *, tm, seq_len):
    i = pl.program_id(0)
    sub_len, n_sub, tiles_per_seq = _seq_geometry(tm, seq_len)
    halo_rows = slice(0, SUBLANES)
    all_cols = slice(0, CONV_WIDTH)
    if tiles_per_seq > 1:
        @pl.when(i % tiles_per_seq == 0)
        def _():
            full_ref[halo_rows, :] = st_ref[0]

        @pl.when(i % tiles_per_seq != 0)
        def _():
            full_ref[halo_rows, :] = cp_ref[...] * up_ref[...]

    out = x_ref[...] + jnp.dot(att_ref[...].astype(BF16), woa_ref[...], preferred_element_type=F32)
    cu = c_ref[...] * u_ref[...]
    for s in range(n_sub):
        rows = slice(s * sub_len, (s + 1) * sub_len)
        if tiles_per_seq == 1:
            full_ref[halo_rows, :] = st_ref[s]
        cur = cu[rows, :]
        y = b_ref[rows, :] * _dwconv3_rows(cur, wc_ref, full_ref, all_cols)
        for g in range(N_GROUPS):
            cols = slice(g * GROUP, (g + 1) * GROUP)
            yc_ref[rows, cols] = _rms_rows(y[:, cols], gc_ref[:, cols]).astype(BF16)
        if tiles_per_seq == 1:
            nc_ref[i * n_sub + s] = cur[sub_len - (CONV_K - 1):, :]

    o_ref[...] = out + jnp.dot(yc_ref[...], wob_ref[...], preferred_element_type=F32)

    if tiles_per_seq > 1:
        @pl.when(i % tiles_per_seq == tiles_per_seq - 1)
        def _():
            last = SUBLANES + sub_len
            nc_ref[i // tiles_per_seq] = full_ref[last - (CONV_K - 1):last, :]


def _state_rows(state):
    return jnp.pad(state, ((0, 0), (SUBLANES - (CONV_K - 1), 0), (0, 0)))


def _mix(att2d, proj, state, w_conv, g_conv, w_o_bf16, x2d, *, tm, seq_len):
    m = x2d.shape[0]
    n_seq = m // seq_len
    sub_len, n_sub, tiles_per_seq = _seq_geometry(tm, seq_len)
    rows_per_block = tm // SUBLANES
    col0 = 3 * ATT_WIDTH // CONV_WIDTH
    prev_map = lambda c: (lambda i: (jnp.maximum(i * rows_per_block - 1, 0), c))
    kernel = functools.partial(_mix_kernel, tm=tm, seq_len=seq_len)
    return pl.pallas_call(
        kernel,
        out_shape=(jax.ShapeDtypeStruct((m, D_MODEL), F32),
                   jax.ShapeDtypeStruct((n_seq, CONV_K - 1, CONV_WIDTH), F32)),
        grid=(m // tm,),
        in_specs=[
            pl.BlockSpec((tm, ATT_WIDTH), lambda i: (i, 0)),
            pl.BlockSpec((tm, CONV_WIDTH), lambda i: (i, col0)),
            pl.BlockSpec((tm, CONV_WIDTH), lambda i: (i, col0 + 1)),
            pl.BlockSpec((tm, CONV_WIDTH), lambda i: (i, col0 + 2)),
            pl.BlockSpec((SUBLANES, CONV_WIDTH), prev_map(col0 + 1)),
            pl.BlockSpec((SUBLANES, CONV_WIDTH), prev_map(col0 + 2)),
            pl.BlockSpec((n_sub, SUBLANES, CONV_WIDTH), lambda i: (i // tiles_per_seq, 0, 0)),
            pl.BlockSpec((CONV_K, CONV_WIDTH), lambda i: (0, 0)),
            pl.BlockSpec((1, CONV_WIDTH), lambda i: (0, 0)),
            pl.BlockSpec((ATT_WIDTH, D_MODEL), lambda i: (0, 0)),
            pl.BlockSpec((CONV_WIDTH, D_MODEL), lambda i: (1, 0)),
            pl.BlockSpec((tm, D_MODEL), lambda i: (i, 0)),
        ],
        out_specs=(pl.BlockSpec((tm, D_MODEL), lambda i: (i, 0)),
                   pl.BlockSpec((n_seq, CONV_K - 1, CONV_WIDTH), lambda i: (0, 0, 0))),
        scratch_shapes=[pltpu.VMEM((SUBLANES + sub_len, CONV_WIDTH), F32),
                        pltpu.VMEM((tm, CONV_WIDTH), BF16)],
        compiler_params=_params(("arbitrary",)),
        name="mix_out_proj",
    )(att2d, proj, proj, proj, proj, proj, _state_rows(state), w_conv, g_conv.reshape(1, CONV_WIDTH),
      w_o_bf16, w_o_bf16, x2d)


def _ffn_kernel(x_ref, g_ref, wg_ref, wu_ref, wc_ref, st_ref, wd_ref, gf_ref, o_ref, nf_ref,
                h_ref, halo_ref, full_ref, act_ref, *, tm, seq_len, n_ff):
    i = pl.program_id(0)
    j = pl.program_id(1)
    sub_len, n_sub, tiles_per_seq = _seq_geometry(tm, seq_len)
    tf = wg_ref.shape[1]
    half = tf // 2

    @pl.when(j == 0)
    def _():
        x = x_ref[...]
        h_ref[...] = _rms_rows(x, g_ref[...]).astype(BF16)
        o_ref[...] = x

    halo_rows = slice(0, SUBLANES)
    if tiles_per_seq > 1:
        @pl.when(i % tiles_per_seq == 0)
        def _():
            halo_ref[j] = st_ref[0]

    h = h_ref[...]
    for c in range(2):
        cols = slice(c * half, (c + 1) * half)
        gate = jnp.dot(h, wg_ref[:, cols], preferred_element_type=F32)
        up = jnp.dot(h, wu_ref[:, cols], preferred_element_type=F32)
        for s in range(n_sub):
            rows = slice(s * sub_len, (s + 1) * sub_len)
            if tiles_per_seq == 1:
                full_ref[halo_rows, cols] = st_ref[s, :, cols]
            else:
                full_ref[halo_rows, cols] = halo_ref[j, :, cols]
            cur = gate[rows, :]
            gc = _dwconv3_rows(cur, wc_ref, full_ref, cols)
            act_ref[rows, cols] = (gc * jax.nn.sigmoid(gc) * up[rows, :]).astype(BF16)
            if tiles_per_seq == 1:
                nf_ref[i * n_sub + s, j, :, cols] = cur[sub_len - (CONV_K - 1):, :]
            else:
                halo_ref[j, :, cols] = cur[sub_len - SUBLANES:, :]

    o_ref[...] += jnp.dot(act_ref[...], wd_ref[...], preferred_element_type=F32)

    if tiles_per_seq > 1:
        @pl.when(i % tiles_per_seq == tiles_per_seq - 1)
        def _():
            nf_ref[i // tiles_per_seq, j] = halo_ref[j, SUBLANES - (CONV_K - 1):, :]

    @pl.when(j == n_ff - 1)
    def _():
        o_ref[...] = _rms_rows(o_ref[...], gf_ref[...])


def _ffn(x2d, g_ffn, w_gate_up_bf16, w_ffn_conv, state, w_down_bf16, g_final, *, tm, seq_len):
    m = x2d.shape[0]
    n_seq = m // seq_len
    tf = FF_TILE
    n_ff = D_FF // tf
    sub_len, n_sub, tiles_per_seq = _seq_geometry(tm, seq_len)
    kernel = functools.partial(_ffn_kernel, tm=tm, seq_len=seq_len, n_ff=n_ff)
    f32_b, bf16_b = 4, 2
    vmem_bytes = (4 * tm * D_MODEL * f32_b + 2 * 3 * D_MODEL * tf * bf16_b + tm * D_MODEL * bf16_b
                  + tm * tf * bf16_b + (SUBLANES + sub_len) * tf * f32_b + 4 * tm * tf * f32_b)
    y, new_ffn = pl.pallas_call(
        kernel,
        out_shape=(jax.ShapeDtypeStruct((m, D_MODEL), F32),
                   jax.ShapeDtypeStruct((n_seq, n_ff, CONV_K - 1, tf), F32)),
        grid=(m // tm, n_ff),
        in_specs=[
            pl.BlockSpec((tm, D_MODEL), lambda i, j: (i, 0)),
            pl.BlockSpec((1, D_MODEL), lambda i, j: (0, 0)),
            pl.BlockSpec((D_MODEL, tf), lambda i, j: (0, j)),
            pl.BlockSpec((D_MODEL, tf), lambda i, j: (0, j + n_ff)),
            pl.BlockSpec((CONV_K, tf), lambda i, j: (0, j)),
            pl.BlockSpec((n_sub, SUBLANES, tf), lambda i, j: (i // tiles_per_seq, 0, j)),
            pl.BlockSpec((tf, D_MODEL), lambda i, j: (j, 0)),
            pl.BlockSpec((1, D_MODEL), lambda i, j: (0, 0)),
        ],
        out_specs=(pl.BlockSpec((tm, D_MODEL), lambda i, j: (i, 0)),
                   pl.BlockSpec((n_seq, n_ff, CONV_K - 1, tf), lambda i, j: (0, 0, 0, 0))),
        scratch_shapes=[pltpu.VMEM((tm, D_MODEL), BF16),
                        pltpu.VMEM((n_ff, SUBLANES, tf), F32),
                        pltpu.VMEM((SUBLANES + sub_len, tf), F32),
                        pltpu.VMEM((tm, tf), BF16)],
        compiler_params=_params(("arbitrary", "arbitrary"), vmem_bytes),
        name="conv_ffn",
    )(x2d, g_ffn.reshape(1, D_MODEL), w_gate_up_bf16, w_gate_up_bf16, w_ffn_conv, _state_rows(state),
      w_down_bf16, g_final.reshape(1, D_MODEL))
    return y, new_ffn.transpose(0, 2, 1, 3).reshape(n_seq, CONV_K - 1, D_FF)


def _tiles(batch, seq_len):
    rows = batch * seq_len
    if rows >= 4096:
        return dict(tm_proj=1024, tm_mix=256, tm_ffn=512, tq=KV_CHUNK)
    return dict(tm_proj=rows, tm_mix=rows, tm_ffn=rows, tq=LANES)


def _trunk(x, cache_k, cache_v, state_conv, state_ffn, weights):
    g_mix, w_in, w_conv, g_att, g_conv, w_o, g_ffn, w_gate_up, w_ffn_conv, w_down, g_final = weights
    b, t, _ = x.shape
    past = cache_k.shape[1]
    tiles = _tiles(b, t)
    tq = tiles["tq"]
    x2d = x.reshape(b * t, D_MODEL)

    proj = _in_proj(x2d, g_mix, w_in, tm=tiles["tm_proj"])
    proj3 = proj.reshape(b, t, IN_WIDTH)
    tq_pad = -(-t // tq) * tq
    tk_pad = -(-(past + tq_pad) // KV_CHUNK) * KV_CHUNK
    if past == 0 and tq_pad == t and tk_pad == t:
        q_h, k_h, v_h, k, v, k_norms = _qkv_layouts(proj3)
    else:
        k = proj3[:, :, ATT_WIDTH:2 * ATT_WIDTH]
        v = proj3[:, :, 2 * ATT_WIDTH:3 * ATT_WIDTH]

        def key_heads(cache, new, **kw):
            tail = jnp.pad(new, ((0, 0), (0, tk_pad - past - t), (0, 0)))
            return _to_heads(cache, tail, permute=True, **kw)

        q_pad = jnp.pad(proj3[:, :, :ATT_WIDTH], ((0, 0), (0, tq_pad - t), (0, 0)))
        q_h = _to_heads(jnp.zeros((b, 0, N_HEADS, HEAD_DIM), F32), q_pad, permute=False, transpose=True)
        k_h, k_norms = key_heads(cache_k, k, transpose=False, key_norms=True)
        v_h = key_heads(cache_v, v, transpose=True)
    att = _attention(q_h, k_h, v_h, k_norms, g_att, past=past, tq=tq)
    att2d = att[:, :t, :].reshape(b * t, ATT_WIDTH)

    x1, new_conv = _mix(att2d, proj, state_conv, w_conv, g_conv, w_o, x2d, tm=tiles["tm_mix"], seq_len=t)
    y, new_ffn = _ffn(x1, g_ffn, w_gate_up, w_ffn_conv, state_ffn, w_down, g_final, tm=tiles["tm_ffn"], seq_len=t)
    shape4 = (b, t, N_HEADS, HEAD_DIM)
    return y.reshape(b, t, D_MODEL), k.reshape(shape4), v.reshape(shape4), new_conv, new_ffn


def kernel(x_prompt, x_sample, cache_k, cache_v, state_conv, state_ffn_conv, g_mix, w_in, w_conv, g_att_out,
           g_conv_out, w_o, g_ffn, w_gate_up, w_ffn_conv, w_down, g_final):
    depth = g_mix.shape[0]
    assert depth == 1
    weights = (g_mix[0], w_in[0].astype(BF16), w_conv[0], g_att_out[0], g_conv_out[0], w_o[0].astype(BF16),
               g_ffn[0], w_gate_up[0].astype(BF16), w_ffn_conv[0], w_down[0].astype(BF16), g_final)
    bp = x_prompt.shape[0]
    dt = x_prompt.dtype
    empty = jnp.zeros((bp, 0, N_HEADS, HEAD_DIM), dt)
    y_p, k_p, v_p, c_p, f_p = _trunk(
        x_prompt, empty, empty, jnp.zeros((bp, CONV_K - 1, CONV_WIDTH), dt), jnp.zeros((bp, CONV_K - 1, D_FF), dt),
        weights)
    y_s, k_s, v_s, c_s, f_s = _trunk(x_sample, cache_k[0], cache_v[0], state_conv[0], state_ffn_conv[0], weights)
    return (y_p, y_s, k_p[None], v_p[None], c_p[None], f_p[None], k_s[None], v_s[None], c_s[None], f_s[None])
```

```python
import functools
import math

import jax
import jax.numpy as jnp
import numpy as np
from jax import lax
from jax.experimental import pallas as pl
from jax.experimental.pallas import tpu as pltpu

F32 = jnp.float32
BF16 = jnp.bfloat16

D_MODEL = 2048
N_HEADS = 8
HEAD_DIM = 128
ATT_WIDTH = N_HEADS * HEAD_DIM
N_GROUPS = 8
GROUP = 128
CONV_WIDTH = N_GROUPS * GROUP
IN_WIDTH = 3 * ATT_WIDTH + 3 * CONV_WIDTH
D_FF = 5632
CONV_K = 3
EPS = 1e-6
Q_SCALE = (HEAD_DIM ** -0.5) * math.log2(math.e)

SUBLANES = 8
LANES = 128
SEGS = 2 * SUBLANES
SEG_LEN = 32
KV_CHUNK = SEGS * SEG_LEN
VMEM_LIMIT_BYTES = 48 * 1024 * 1024

FF_TILE = 512
IN_PROJ_COL_TILE = 1024
assert ATT_WIDTH == IN_PROJ_COL_TILE and CONV_WIDTH == IN_PROJ_COL_TILE


def _params(sem, vmem_limit_bytes=VMEM_LIMIT_BYTES):
    return pltpu.CompilerParams(dimension_semantics=sem, vmem_limit_bytes=vmem_limit_bytes)


def _rms_rows(x, g):
    ms = jnp.mean(x * x, axis=-1, keepdims=True)
    return (x * lax.rsqrt(ms + EPS)) * g


def _key_row_permutation():
    p = np.zeros((KV_CHUNK, KV_CHUNK), np.float32)
    for i in range(SEG_LEN):
        for s in range(SEGS):
            p[SEGS * i + s, SEG_LEN * s + SEG_LEN - 1 - i] = 1.0
    return jnp.asarray(p, BF16)


def _in_proj_kernel(x_ref, g_ref, w_ref, o_ref, h_ref):
    j = pl.program_id(1)

    @pl.when(j == 0)
    def _():
        h_ref[...] = _rms_rows(x_ref[...], g_ref[...]).astype(BF16)

    acc = jnp.dot(h_ref[...], w_ref[...], preferred_element_type=F32)
    o_ref[...] = acc * jnp.where(j == 0, Q_SCALE, 1.0).astype(F32)


def _in_proj(x2d, g_mix, w_in_bf16, *, tm):
    m = x2d.shape[0]
    tn = IN_PROJ_COL_TILE
    return pl.pallas_call(
        _in_proj_kernel,
        out_shape=jax.ShapeDtypeStruct((m, IN_WIDTH), F32),
        grid=(m // tm, IN_WIDTH // tn),
        in_specs=[
            pl.BlockSpec((tm, D_MODEL), lambda i, j: (i, 0)),
            pl.BlockSpec((1, D_MODEL), lambda i, j: (0, 0)),
            pl.BlockSpec((D_MODEL, tn), lambda i, j: (0, j)),
        ],
        out_specs=pl.BlockSpec((tm, tn), lambda i, j: (i, j)),
        scratch_shapes=[pltpu.VMEM((tm, D_MODEL), BF16)],
        compiler_params=_params(("arbitrary", "arbitrary")),
        name="in_proj",
    )(x2d, g_mix.reshape(1, D_MODEL), w_in_bf16)


def _store_heads(x32, p_ref, o_ref, *, permute, transpose):
    if permute:
        x32 = jnp.dot(p_ref[...], x32.astype(BF16), preferred_element_type=F32)
    for h in range(N_HEADS):
        xh = x32[:, h * HEAD_DIM:(h + 1) * HEAD_DIM]
        o_ref[0, h] = (xh.T if transpose else xh).astype(BF16)


def _max_key_norm2(k32):
    kb = k32.astype(BF16).astype(F32)
    out = []
    for h in range(N_HEADS):
        kh = kb[:, h * HEAD_DIM:(h + 1) * HEAD_DIM]
        ss = jnp.sum(kh * kh, axis=1, keepdims=True)
        out.append(jnp.broadcast_to(jnp.max(ss, axis=0, keepdims=True), (1, LANES)))
    return jnp.concatenate(out, axis=0)


def _to_heads_kernel(p_ref, x_ref, tail_ref, o_ref, *kn_ref, permute, transpose, n_src_blocks):
    src32 = x_ref[0].reshape(x_ref.shape[1], ATT_WIDTH)
    x32 = jnp.where(pl.program_id(1) < n_src_blocks, src32, tail_ref[0])
    for ref in kn_ref:
        ref[0, 0] = _max_key_norm2(x32)
    _store_heads(x32, p_ref, o_ref, permute=permute, transpose=transpose)


def _qkv_layouts_kernel(p_ref, x_ref, qT_ref, k_ref, vT_ref, k32_ref, v32_ref, kn_ref):
    q32 = x_ref[0, :, 0:ATT_WIDTH]
    k32 = x_ref[0, :, ATT_WIDTH:2 * ATT_WIDTH]
    v32 = x_ref[0, :, 2 * ATT_WIDTH:3 * ATT_WIDTH]
    kn_ref[0, 0] = _max_key_norm2(k32)
    k32_ref[0] = k32.reshape(k32.shape[0], N_HEADS, HEAD_DIM)
    v32_ref[0] = v32.reshape(v32.shape[0], N_HEADS, HEAD_DIM)
    _store_heads(q32, p_ref, qT_ref, permute=False, transpose=True)
    _store_heads(k32, p_ref, k_ref, permute=True, transpose=False)
    _store_heads(v32, p_ref, vT_ref, permute=True, transpose=True)


def _qkv_layouts(proj3):
    b, t, _ = proj3.shape
    rows = KV_CHUNK
    assert t % rows == 0
    heads_t = jax.ShapeDtypeStruct((b, N_HEADS, HEAD_DIM, t), BF16)
    heads = jax.ShapeDtypeStruct((b, N_HEADS, t, HEAD_DIM), BF16)
    plain = jax.ShapeDtypeStruct((b, t, N_HEADS, HEAD_DIM), F32)
    spec_t = pl.BlockSpec((1, N_HEADS, HEAD_DIM, rows), lambda bi, i: (bi, 0, 0, i))
    spec_h = pl.BlockSpec((1, N_HEADS, rows, HEAD_DIM), lambda bi, i: (bi, 0, i, 0))
    spec_p = pl.BlockSpec((1, rows, N_HEADS, HEAD_DIM), lambda bi, i: (bi, i, 0, 0))
    norms = jax.ShapeDtypeStruct((b, t // rows, N_HEADS, LANES), F32)
    spec_n = pl.BlockSpec((1, 1, N_HEADS, LANES), lambda bi, i: (bi, i, 0, 0))
    qT, k_h, vT, k32, v32, kn = pl.pallas_call(
        _qkv_layouts_kernel,
        out_shape=(heads_t, heads, heads_t, plain, plain, norms),
        grid=(b, t // rows),
        in_specs=[
            pl.BlockSpec((KV_CHUNK, KV_CHUNK), lambda bi, i: (0, 0)),
            pl.BlockSpec((1, rows, 3 * ATT_WIDTH), lambda bi, i: (bi, i, 0)),
        ],
        out_specs=(spec_t, spec_h, spec_t, spec_p, spec_p, spec_n),
        compiler_params=_params(("arbitrary", "arbitrary")),
        name="qkv_layouts",
    )(_key_row_permutation(), proj3)
    merge = lambda a: a.reshape((b * N_HEADS,) + a.shape[2:])
    return merge(qT), merge(k_h), merge(vT), k32, v32, kn


def _to_heads(src, tail, *, permute, transpose, key_norms=False):
    b, t_src = src.shape[:2]
    t = t_src + tail.shape[1]
    rows = KV_CHUNK if permute else math.gcd(math.gcd(t_src, tail.shape[1]), KV_CHUNK)
    assert t_src % rows == 0 and tail.shape[1] % rows == 0 and rows % LANES == 0
    n_src = t_src // rows
    if n_src == 0:
        src = tail.reshape(b, tail.shape[1], N_HEADS, HEAD_DIM)
    if transpose:
        out_shape = jax.ShapeDtypeStruct((b, N_HEADS, HEAD_DIM, t), BF16)
        out_spec = pl.BlockSpec((1, N_HEADS, HEAD_DIM, rows), lambda bi, i: (bi, 0, 0, i))
    else:
        out_shape = jax.ShapeDtypeStruct((b, N_HEADS, t, HEAD_DIM), BF16)
        out_spec = pl.BlockSpec((1, N_HEADS, rows, HEAD_DIM), lambda bi, i: (bi, 0, i, 0))
    if key_norms:
        out_shape = (out_shape, jax.ShapeDtypeStruct((b, t // rows, N_HEADS, LANES), F32))
        out_spec = (out_spec, pl.BlockSpec((1, 1, N_HEADS, LANES), lambda bi, i: (bi, i, 0, 0)))
    out = pl.pallas_call(
        functools.partial(_to_heads_kernel, permute=permute, transpose=transpose, n_src_blocks=n_src),
        out_shape=out_shape,
        grid=(b, t // rows),
        in_specs=[
            pl.BlockSpec((KV_CHUNK, KV_CHUNK), lambda bi, i: (0, 0)),
            pl.BlockSpec((1, rows, N_HEADS, HEAD_DIM), lambda bi, i: (bi, jnp.minimum(i, max(n_src - 1, 0)), 0, 0)),
            pl.BlockSpec((1, rows, ATT_WIDTH), lambda bi, i: (bi, jnp.maximum(i - n_src, 0), 0)),
        ],
        out_specs=out_spec,
        compiler_params=_params(("arbitrary", "arbitrary")),
        name="to_heads",
    )(_key_row_permutation(), src, tail)
    merge = lambda a: a.reshape((b * N_HEADS,) + a.shape[2:])
    return (merge(out[0]), out[1]) if key_norms else merge(out)


EXP2_UNDERFLOW = 160.0
SCORE_BOUND_SLACK = 1.01


def _softplus2(z):
    e = jnp.exp2(-jnp.abs(z))
    return jnp.maximum(z, 0.0) + jnp.log(1.0 + e) * math.log2(math.e)


def _attn_kernel(qT_ref, k_ref, vT_ref, kn_ref, g_ref, o_ref, z_ref, d_ref, w_ref, acc_ref, carry_ref,
                 *, past, tq):
    qi = pl.program_id(1)
    q_lo = past + qi * tq
    j_top = q_lo >> (KV_CHUNK.bit_length() - 1)

    qT = qT_ref[0]
    qpos = q_lo + lax.broadcasted_iota(jnp.int32, (SEGS, tq), 1)
    top_key = j_top * KV_CHUNK + SEG_LEN * lax.broadcasted_iota(jnp.int32, (SEGS, tq), 0) + (SEG_LEN - 1)

    def causal(i):
        return top_key - i < qpos

    def scores(j):
        base = pl.multiple_of(j * KV_CHUNK, KV_CHUNK)
        kc = k_ref[0, pl.ds(base, KV_CHUNK), :]
        return jnp.dot(kc, qT, preferred_element_type=F32)

    def add_values(j):
        base = pl.multiple_of(j * KV_CHUNK, KV_CHUNK)
        vc = vT_ref[0, :, pl.ds(base, KV_CHUNK)]
        acc_ref[...] += jnp.dot(vc, w_ref[...], preferred_element_type=F32)

    def visit(j, cur, masked):
        z_ref[1 - cur] = scores(jnp.maximum(j - 1, 0))
        if not masked:
            add_values(j + 1)
        run = jnp.zeros((SEGS, tq), F32)
        for i in range(SEG_LEN):
            rows = slice(i * SEGS, (i + 1) * SEGS)
            z = z_ref[cur, rows, :]
            sp = _softplus2(z)
            if masked:
                sp = jnp.where(causal(i), sp, 0.0)
            run = run + sp
            d_ref[rows, :] = z - run
        tail = carry_ref[0:1, :]
        offs = [None] * SEGS
        for s in reversed(range(SEGS)):
            offs[s] = tail
            tail = tail + run[s:s + 1, :]
        carry_ref[...] = jnp.broadcast_to(tail, carry_ref.shape)
        off = jnp.concatenate(offs, axis=0)
        for i in range(SEG_LEN):
            rows = slice(i * SEGS, (i + 1) * SEGS)
            w = jnp.exp2(d_ref[rows, :] - off)
            if masked:
                w = jnp.where(causal(i), w, 0.0)
            w_ref[rows, :] = w.astype(BF16)

    kn = jnp.max(kn_ref[0], axis=0)
    head = pl.program_id(0) % N_HEADS
    kn = jnp.where(lax.broadcasted_iota(jnp.int32, kn.shape, 0) == head, kn, 0.0)
    k_norm = jnp.sqrt(jnp.max(kn, axis=0, keepdims=True)[:, 0:1])

    q32 = qT.astype(F32)
    q_norm = jnp.sqrt(jnp.sum(q32 * q32, axis=0, keepdims=True))
    dead_at = q_norm * k_norm * SCORE_BOUND_SLACK + EXP2_UNDERFLOW

    def rest_is_zero():
        return jnp.min(carry_ref[0:1, :] - dead_at) >= 0.0

    acc_ref[...] = jnp.zeros_like(acc_ref)
    carry_ref[...] = jnp.zeros_like(carry_ref)
    z_ref[0] = scores(j_top)

    visit(j_top, 0, True)
    if tq == KV_CHUNK:
        first = j_top >= 1
    else:
        first = jnp.logical_and(j_top >= 1, jnp.logical_not(rest_is_zero()))

    @pl.when(first)
    def _():
        visit(j_top - 1, 1, False)

    n_pairs = jnp.maximum(j_top - 1, 0) >> 1

    def more_pairs(c):
        t, done = c
        return jnp.logical_and(t < n_pairs, jnp.logical_not(done))

    def pair(c):
        t, _ = c
        j = j_top - 2 - 2 * t
        visit(j, 0, False)
        visit(j - 1, 1, False)
        return t + 1, rest_is_zero()

    stop = jnp.logical_or(jnp.logical_not(first), rest_is_zero())
    t_end, done = lax.while_loop(more_pairs, pair, (jnp.int32(0), stop))
    odd_visit = jnp.logical_and(((j_top - 1) & 1) == 1, jnp.logical_not(done))

    @pl.when(odd_visit)
    def _():
        visit(0, 0, False)

    j_last = jnp.where(odd_visit, 0, jnp.where(first, j_top - 1 - 2 * t_end, j_top))
    add_values(j_last)

    out = acc_ref[...]
    ms = jnp.mean(out * out, axis=0, keepdims=True)
    y = (out * lax.rsqrt(ms + EPS)) * g_ref[0]
    o_ref[0] = y.T.astype(o_ref.dtype)


def _attention(qT, k_perm, vT_perm, key_norms, g_att, *, past, tq):
    bh_total, _, t_q = qT.shape
    b = bh_total // N_HEADS
    t_k = k_perm.shape[1]
    n_chunks = t_k // KV_CHUNK
    assert KV_CHUNK % tq == 0 and past % KV_CHUNK == 0 and t_q % tq == 0
    assert t_k % KV_CHUNK == 0 and t_k >= -(-(past + t_q) // KV_CHUNK) * KV_CHUNK
    assert key_norms.shape == (b, n_chunks, N_HEADS, LANES)
    kernel = functools.partial(_attn_kernel, past=past, tq=tq)
    return pl.pallas_call(
        kernel,
        out_shape=jax.ShapeDtypeStruct((b, t_q, ATT_WIDTH), BF16),
        grid=(b * N_HEADS, t_q // tq),
        in_specs=[
            pl.BlockSpec((1, HEAD_DIM, tq), lambda bh, i: (bh, 0, i)),
            pl.BlockSpec((1, t_k, HEAD_DIM), lambda bh, i: (bh, 0, 0)),
            pl.BlockSpec((1, HEAD_DIM, t_k), lambda bh, i: (bh, 0, 0)),
            pl.BlockSpec((1, n_chunks, N_HEADS, LANES), lambda bh, i: (bh // N_HEADS, 0, 0, 0)),
            pl.BlockSpec((1, HEAD_DIM, 1), lambda bh, i: (bh % N_HEADS, 0, 0)),
        ],
        out_specs=pl.BlockSpec((1, tq, HEAD_DIM), lambda bh, i: (bh // N_HEADS, i, bh % N_HEADS)),
        scratch_shapes=[
            pltpu.VMEM((2, KV_CHUNK, tq), F32),
            pltpu.VMEM((KV_CHUNK, tq), F32),
            pltpu.VMEM((KV_CHUNK, tq), BF16),
            pltpu.VMEM((HEAD_DIM, tq), F32),
            pltpu.VMEM((SUBLANES, tq), F32),
        ],
        compiler_params=_params(("arbitrary", "arbitrary")),
        name="sb_attention",
    )(qT, k_perm, vT_perm, key_norms, g_att.reshape(N_HEADS, HEAD_DIM, 1))


def _dwconv3_rows(cur, w_ref, full_ref, cols):
    n = cur.shape[0]
    full_ref[SUBLANES:SUBLANES + n, cols] = cur
    return (full_ref[SUBLANES - 2:SUBLANES - 2 + n, cols] * w_ref[0:1, cols]
            + full_ref[SUBLANES - 1:SUBLANES - 1 + n, cols] * w_ref[1:2, cols]
            + cur * w_ref[2:3, cols])


def _seq_geometry(tm, seq_len):
    sub_len = min(seq_len, tm)
    return sub_len, tm // sub_len, max(seq_len // tm, 1)


def _mix_kernel(att_ref, b_ref, c_ref, u_ref, cp_ref, up_ref, st_ref, wc_ref, gc_ref, woa_ref, wob_ref,
                x_ref, o_ref, nc_ref, full_ref, yc_ref, *, tm, seq_len):
    i = pl.program_id(0)
    sub_len, n_sub, tiles_per_seq = _seq_geometry(tm, seq_len)
    halo_rows = slice(0, SUBLANES)
    all_cols = slice(0, CONV_WIDTH)
    if tiles_per_seq > 1:
        @pl.when(i % tiles_per_seq == 0)
        def _():
            full_ref[halo_rows, :] = st_ref[0]

        @pl.when(i % tiles_per_seq != 0)
        def _():
            full_ref[halo_rows, :] = cp_ref[...] * up_ref[...]

    out = x_ref[...] + jnp.dot(att_ref[...], woa_ref[...], preferred_element_type=F32)
    cu = c_ref[...] * u_ref[...]
    for s in range(n_sub):
        rows = slice(s * sub_len, (s + 1) * sub_len)
        if tiles_per_seq == 1:
            full_ref[halo_rows, :] = st_ref[s]
        cur = cu[rows, :]
        y = b_ref[rows, :] * _dwconv3_rows(cur, wc_ref, full_ref, all_cols)
        for g in range(N_GROUPS):
            cols = slice(g * GROUP, (g + 1) * GROUP)
            yc_ref[rows, cols] = _rms_rows(y[:, cols], gc_ref[:, cols]).astype(BF16)
        if tiles_per_seq == 1:
            nc_ref[i * n_sub + s] = cur[sub_len - (CONV_K - 1):, :]

    o_ref[...] = out + jnp.dot(yc_ref[...], wob_ref[...], preferred_element_type=F32)

    if tiles_per_seq > 1:
        @pl.when(i % tiles_per_seq == tiles_per_seq - 1)
        def _():
            last = SUBLANES + sub_len
            nc_ref[i // tiles_per_seq] = full_ref[last - (CONV_K - 1):last, :]


def _state_rows(state):
    return jnp.pad(state, ((0, 0), (SUBLANES - (CONV_K - 1), 0), (0, 0)))


def _mix(att2d, proj, state, w_conv, g_conv, w_o_bf16, x2d, *, tm, seq_len):
    m = x2d.shape[0]
    n_seq = m // seq_len
    sub_len, n_sub, tiles_per_seq = _seq_geometry(tm, seq_len)
    rows_per_block = tm // SUBLANES
    col0 = 3 * ATT_WIDTH // CONV_WIDTH
    prev_map = lambda c: (lambda i: (jnp.maximum(i * rows_per_block - 1, 0), c))
    kernel = functools.partial(_mix_kernel, tm=tm, seq_len=seq_len)
    return pl.pallas_call(
        kernel,
        out_shape=(jax.ShapeDtypeStruct((m, D_MODEL), F32),
                   jax.ShapeDtypeStruct((n_seq, CONV_K - 1, CONV_WIDTH), F32)),
        grid=(m // tm,),
        in_specs=[
            pl.BlockSpec((tm, ATT_WIDTH), lambda i: (i, 0)),
            pl.BlockSpec((tm, CONV_WIDTH), lambda i: (i, col0)),
            pl.BlockSpec((tm, CONV_WIDTH), lambda i: (i, col0 + 1)),
            pl.BlockSpec((tm, CONV_WIDTH), lambda i: (i, col0 + 2)),
            pl.BlockSpec((SUBLANES, CONV_WIDTH), prev_map(col0 + 1)),
            pl.BlockSpec((SUBLANES, CONV_WIDTH), prev_map(col0 + 2)),
            pl.BlockSpec((n_sub, SUBLANES, CONV_WIDTH), lambda i: (i // tiles_per_seq, 0, 0)),
            pl.BlockSpec((CONV_K, CONV_WIDTH), lambda i: (0, 0)),
            pl.BlockSpec((1, CONV_WIDTH), lambda i: (0, 0)),
            pl.BlockSpec((ATT_WIDTH, D_MODEL), lambda i: (0, 0)),
            pl.BlockSpec((CONV_WIDTH, D_MODEL), lambda i: (1, 0)),
            pl.BlockSpec((tm, D_MODEL), lambda i: (i, 0)),
        ],
        out_specs=(pl.BlockSpec((tm, D_MODEL), lambda i: (i, 0)),
                   pl.BlockSpec((n_seq, CONV_K - 1, CONV_WIDTH), lambda i: (0, 0, 0))),
        scratch_shapes=[pltpu.VMEM((SUBLANES + sub_len, CONV_WIDTH), F32),
                        pltpu.VMEM((tm, CONV_WIDTH), BF16)],
        compiler_params=_params(("arbitrary",)),
        name="mix_out_proj",
    )(att2d, proj, proj, proj, proj, proj, _state_rows(state), w_conv, g_conv.reshape(1, CONV_WIDTH),
      w_o_bf16, w_o_bf16, x2d)


def _ffn_kernel(x_ref, g_ref, wg_ref, wu_ref, wc_ref, st_ref, wd_ref, gf_ref, o_ref, nf_ref,
                h_ref, halo_ref, full_ref, act_ref, *, tm, seq_len, n_ff):
    i = pl.program_id(0)
    j = pl.program_id(1)
    sub_len, n_sub, tiles_per_seq = _seq_geometry(tm, seq_len)
    tf = wg_ref.shape[1]
    half = tf // 2

    @pl.when(j == 0)
    def _():
        x = x_ref[...]
        h_ref[...] = _rms_rows(x, g_ref[...]).astype(BF16)
        o_ref[...] = x

    halo_rows = slice(0, SUBLANES)
    if tiles_per_seq > 1:
        @pl.when(i % tiles_per_seq == 0)
        def _():
            halo_ref[j] = st_ref[0]

    h = h_ref[...]
    for c in range(2):
        cols = slice(c * half, (c + 1) * half)
        gate = jnp.dot(h, wg_ref[:, cols], preferred_element_type=F32)
        up = jnp.dot(h, wu_ref[:, cols], preferred_element_type=F32)
        for s in range(n_sub):
            rows = slice(s * sub_len, (s + 1) * sub_len)
            if tiles_per_seq == 1:
                full_ref[halo_rows, cols] = st_ref[s, :, cols]
            else:
                full_ref[halo_rows, cols] = halo_ref[j, :, cols]
            cur = gate[rows, :]
            gc = _dwconv3_rows(cur, wc_ref, full_ref, cols)
            act_ref[rows, cols] = (gc * jax.nn.sigmoid(gc) * up[rows, :]).astype(BF16)
            if tiles_per_seq == 1:
                nf_ref[i * n_sub + s, j, :, cols] = cur[sub_len - (CONV_K - 1):, :]
            else:
                halo_ref[j, :, cols] = cur[sub_len - SUBLANES:, :]

    o_ref[...] += jnp.dot(act_ref[...], wd_ref[...], preferred_element_type=F32)

    if tiles_per_seq > 1:
        @pl.when(i % tiles_per_seq == tiles_per_seq - 1)
        def _():
            nf_ref[i // tiles_per_seq, j] = halo_ref[j, SUBLANES - (CONV_K - 1):, :]

    @pl.when(j == n_ff - 1)
    def _():
        o_ref[...] = _rms_rows(o_ref[...], gf_ref[...])


def _ffn(x2d, g_ffn, w_gate_up_bf16, w_ffn_conv, state, w_down_bf16, g_final, *, tm, seq_len):
    m = x2d.shape[0]
    n_seq = m // seq_len
    tf = FF_TILE
    n_ff = D_FF // tf
    sub_len, n_sub, tiles_per_seq = _seq_geometry(tm, seq_len)
    kernel = functools.partial(_ffn_kernel, tm=tm, seq_len=seq_len, n_ff=n_ff)
    f32_b, bf16_b = 4, 2
    vmem_bytes = (4 * tm * D_MODEL * f32_b + 2 * 3 * D_MODEL * tf * bf16_b + tm * D_MODEL * bf16_b
                  + tm * tf * bf16_b + (SUBLANES + sub_len) * tf * f32_b + 4 * tm * tf * f32_b)
    y, new_ffn = pl.pallas_call(
        kernel,
        out_shape=(jax.ShapeDtypeStruct((m, D_MODEL), F32),
                   jax.ShapeDtypeStruct((n_seq, n_ff, CONV_K - 1, tf), F32)),
        grid=(m // tm, n_ff),
        in_specs=[
            pl.BlockSpec((tm, D_MODEL), lambda i, j: (i, 0)),
            pl.BlockSpec((1, D_MODEL), lambda i, j: (0, 0)),
            pl.BlockSpec((D_MODEL, tf), lambda i, j: (0, j)),
            pl.BlockSpec((D_MODEL, tf), lambda i, j: (0, j + n_ff)),
            pl.BlockSpec((CONV_K, tf), lambda i, j: (0, j)),
            pl.BlockSpec((n_sub, SUBLANES, tf), lambda i, j: (i // tiles_per_seq, 0, j)),
            pl.BlockSpec((tf, D_MODEL), lambda i, j: (j, 0)),
            pl.BlockSpec((1, D_MODEL), lambda i, j: (0, 0)),
        ],
        out_specs=(pl.BlockSpec((tm, D_MODEL), lambda i, j: (i, 0)),
                   pl.BlockSpec((n_seq, n_ff, CONV_K - 1, tf), lambda i, j: (0, 0, 0, 0))),
        scratch_shapes=[pltpu.VMEM((tm, D_MODEL), BF16),
                        pltpu.VMEM((n_ff, SUBLANES, tf), F32),
                        pltpu.VMEM((SUBLANES + sub_len, tf), F32),
                        pltpu.VMEM((tm, tf), BF16)],
        compiler_params=_params(("arbitrary", "arbitrary"), vmem_bytes),
        name="conv_ffn",
    )(x2d, g_ffn.reshape(1, D_MODEL), w_gate_up_bf16, w_gate_up_bf16, w_ffn_conv, _state_rows(state),
      w_down_bf16, g_final.reshape(1, D_MODEL))
    return y, new_ffn.transpose(0, 2, 1, 3).reshape(n_seq, CONV_K - 1, D_FF)


def _tiles(batch, seq_len):
    rows = batch * seq_len
    if rows >= 4096:
        return dict(tm_proj=1024, tm_mix=256, tm_ffn=512, tq=KV_CHUNK)
    return dict(tm_proj=rows, tm_mix=rows, tm_ffn=rows, tq=LANES)


def _trunk(x, cache_k, cache_v, state_conv, state_ffn, weights):
    g_mix, w_in, w_conv, g_att, g_conv, w_o, g_ffn, w_gate_up, w_ffn_conv, w_down, g_final = weights
    b, t, _ = x.shape
    past = cache_k.shape[1]
    tiles = _tiles(b, t)
    tq = tiles["tq"]
    x2d = x.reshape(b * t, D_MODEL)

    proj = _in_proj(x2d, g_mix, w_in, tm=tiles["tm_proj"])
    proj3 = proj.reshape(b, t, IN_WIDTH)
    tq_pad = -(-t // tq) * tq
    tk_pad = -(-(past + tq_pad) // KV_CHUNK) * KV_CHUNK
    if past == 0 and tq_pad == t and tk_pad == t:
        q_h, k_h, v_h, k, v, k_norms = _qkv_layouts(proj3)
    else:
        k = proj3[:, :, ATT_WIDTH:2 * ATT_WIDTH]
        v = proj3[:, :, 2 * ATT_WIDTH:3 * ATT_WIDTH]

        def key_heads(cache, new, **kw):
            tail = jnp.pad(new, ((0, 0), (0, tk_pad - past - t), (0, 0)))
            return _to_heads(cache, tail, permute=True, **kw)

        q_pad = jnp.pad(proj3[:, :, :ATT_WIDTH], ((0, 0), (0, tq_pad - t), (0, 0)))
        q_h = _to_heads(jnp.zeros((b, 0, N_HEADS, HEAD_DIM), F32), q_pad, permute=False, transpose=True)
        k_h, k_norms = key_heads(cache_k, k, transpose=False, key_norms=True)
        v_h = key_heads(cache_v, v, transpose=True)
    att = _attention(q_h, k_h, v_h, k_norms, g_att, past=past, tq=tq)
    att2d = att[:, :t, :].reshape(b * t, ATT_WIDTH)

    x1, new_conv = _mix(att2d, proj, state_conv, w_conv, g_conv, w_o, x2d, tm=tiles["tm_mix"], seq_len=t)
    y, new_ffn = _ffn(x1, g_ffn, w_gate_up, w_ffn_conv, state_ffn, w_down, g_final, tm=tiles["tm_ffn"], seq_len=t)
    shape4 = (b, t, N_HEADS, HEAD_DIM)
    return y.reshape(b, t, D_MODEL), k.reshape(shape4), v.reshape(shape4), new_conv, new_ffn


def kernel(x_prompt, x_sample, cache_k, cache_v, state_conv, state_ffn_conv, g_mix, w_in, w_conv, g_att_out,
           g_conv_out, w_o, g_ffn, w_gate_up, w_ffn_conv, w_down, g_final):
    depth = g_mix.shape[0]
    assert depth == 1
    weights = (g_mix[0], w_in[0].astype(BF16), w_conv[0], g_att_out[0], g_conv_out[0], w_o[0].astype(BF16),
               g_ffn[0], w_gate_up[0].astype(BF16), w_ffn_conv[0], w_down[0].astype(BF16), g_final)
    bp = x_prompt.shape[0]
    dt = x_prompt.dtype
    empty = jnp.zeros((bp, 0, N_HEADS, HEAD_DIM), dt)
    y_p, k_p, v_p, c_p, f_p = _trunk(
        x_prompt, empty, empty, jnp.zeros((bp, CONV_K - 1, CONV_WIDTH), dt), jnp.zeros((bp, CONV_K - 1, D_FF), dt),
        weights)
    y_s, k_s, v_s, c_s, f_s = _trunk(x_sample, cache_k[0], cache_v[0], state_conv[0], state_ffn_conv[0], weights)
    return (y_p, y_s, k_p[None], v_p[None], c_p[None], f_p[None], k_s[None], v_s[None], c_s[None], f_s[None])
```

```python
import functools
import math

import jax
import jax.numpy as jnp
import numpy as np
from jax import lax
from jax.experimental import pallas as pl
from jax.experimental.pallas import tpu as pltpu

F32 = jnp.float32
BF16 = jnp.bfloat16

D_MODEL = 2048
N_HEADS = 8
HEAD_DIM = 128
ATT_WIDTH = N_HEADS * HEAD_DIM
N_GROUPS = 8
GROUP = 128
CONV_WIDTH = N_GROUPS * GROUP
IN_WIDTH = 3 * ATT_WIDTH + 3 * CONV_WIDTH
D_FF = 5632
CONV_K = 3
EPS = 1e-6
Q_SCALE = (HEAD_DIM ** -0.5) * math.log2(math.e)

SUBLANES = 8
LANES = 128
SEGS = 2 * SUBLANES
SEG_LEN = 32
KV_CHUNK = SEGS * SEG_LEN
VMEM_LIMIT_BYTES = 48 * 1024 * 1024

FF_TILE = 512
IN_PROJ_COL_TILE = 1024
assert ATT_WIDTH == IN_PROJ_COL_TILE and CONV_WIDTH == IN_PROJ_COL_TILE


def _params(sem, vmem_limit_bytes=VMEM_LIMIT_BYTES):
    return pltpu.CompilerParams(dimension_semantics=sem, vmem_limit_bytes=vmem_limit_bytes)


def _rms_rows(x, g):
    ms = jnp.mean(x * x, axis=-1, keepdims=True)
    return (x * lax.rsqrt(ms + EPS)) * g


def _key_row_permutation():
    p = np.zeros((KV_CHUNK, KV_CHUNK), np.float32)
    for i in range(SEG_LEN):
        for s in range(SEGS):
            p[SEGS * i + s, SEG_LEN * s + SEG_LEN - 1 - i] = 1.0
    return jnp.asarray(p, BF16)


def _in_proj_kernel(x_ref, g_ref, w_ref, o_ref, h_ref):
    j = pl.program_id(1)

    @pl.when(j == 0)
    def _():
        h_ref[...] = _rms_rows(x_ref[...], g_ref[...]).astype(BF16)

    acc = jnp.dot(h_ref[...], w_ref[...], preferred_element_type=F32)
    o_ref[...] = acc * jnp.where(j == 0, Q_SCALE, 1.0).astype(F32)


def _in_proj(x2d, g_mix, w_in_bf16, *, tm):
    m = x2d.shape[0]
    tn = IN_PROJ_COL_TILE
    return pl.pallas_call(
        _in_proj_kernel,
        out_shape=jax.ShapeDtypeStruct((m, IN_WIDTH), F32),
        grid=(m // tm, IN_WIDTH // tn),
        in_specs=[
            pl.BlockSpec((tm, D_MODEL), lambda i, j: (i, 0)),
            pl.BlockSpec((1, D_MODEL), lambda i, j: (0, 0)),
            pl.BlockSpec((D_MODEL, tn), lambda i, j: (0, j)),
        ],
        out_specs=pl.BlockSpec((tm, tn), lambda i, j: (i, j)),
        scratch_shapes=[pltpu.VMEM((tm, D_MODEL), BF16)],
        compiler_params=_params(("arbitrary", "arbitrary")),
        name="in_proj",
    )(x2d, g_mix.reshape(1, D_MODEL), w_in_bf16)


def _store_heads(x32, p_ref, o_ref, *, permute, transpose):
    if permute:
        x32 = jnp.dot(p_ref[...], x32.astype(BF16), preferred_element_type=F32)
    for h in range(N_HEADS):
        xh = x32[:, h * HEAD_DIM:(h + 1) * HEAD_DIM]
        o_ref[0, h] = (xh.T if transpose else xh).astype(BF16)


def _max_key_norm2(k32):
    kb = k32.astype(BF16).astype(F32)
    out = []
    for h in range(N_HEADS):
        kh = kb[:, h * HEAD_DIM:(h + 1) * HEAD_DIM]
        ss = jnp.sum(kh * kh, axis=1, keepdims=True)
        out.append(jnp.broadcast_to(jnp.max(ss, axis=0, keepdims=True), (1, LANES)))
    return jnp.concatenate(out, axis=0)


def _to_heads_kernel(p_ref, x_ref, tail_ref, o_ref, *kn_ref, permute, transpose, n_src_blocks):
    src32 = x_ref[0].reshape(x_ref.shape[1], ATT_WIDTH)
    x32 = jnp.where(pl.program_id(1) < n_src_blocks, src32, tail_ref[0])
    for ref in kn_ref:
        ref[0, 0] = _max_key_norm2(x32)
    _store_heads(x32, p_ref, o_ref, permute=permute, transpose=transpose)


def _qkv_layouts_kernel(p_ref, x_ref, qT_ref, k_ref, vT_ref, k32_ref, v32_ref, kn_ref):
    q32 = x_ref[0, :, 0:ATT_WIDTH]
    k32 = x_ref[0, :, ATT_WIDTH:2 * ATT_WIDTH]
    v32 = x_ref[0, :, 2 * ATT_WIDTH:3 * ATT_WIDTH]
    kn_ref[0, 0] = _max_key_norm2(k32)
    k32_ref[0] = k32.reshape(k32.shape[0], N_HEADS, HEAD_DIM)
    v32_ref[0] = v32.reshape(v32.shape[0], N_HEADS, HEAD_DIM)
    _store_heads(q32, p_ref, qT_ref, permute=False, transpose=True)
    _store_heads(k32, p_ref, k_ref, permute=True, transpose=False)
    _store_heads(v32, p_ref, vT_ref, permute=True, transpose=True)


def _qkv_layouts(proj3):
    b, t, _ = proj3.shape
    rows = KV_CHUNK
    assert t % rows == 0
    heads_t = jax.ShapeDtypeStruct((b, N_HEADS, HEAD_DIM, t), BF16)
    heads = jax.ShapeDtypeStruct((b, N_HEADS, t, HEAD_DIM), BF16)
    plain = jax.ShapeDtypeStruct((b, t, N_HEADS, HEAD_DIM), F32)
    spec_t = pl.BlockSpec((1, N_HEADS, HEAD_DIM, rows), lambda bi, i: (bi, 0, 0, i))
    spec_h = pl.BlockSpec((1, N_HEADS, rows, HEAD_DIM), lambda bi, i: (bi, 0, i, 0))
    spec_p = pl.BlockSpec((1, rows, N_HEADS, HEAD_DIM), lambda bi, i: (bi, i, 0, 0))
    norms = jax.ShapeDtypeStruct((b, t // rows, N_HEADS, LANES), F32)
    spec_n = pl.BlockSpec((1, 1, N_HEADS, LANES), lambda bi, i: (bi, i, 0, 0))
    qT, k_h, vT, k32, v32, kn = pl.pallas_call(
        _qkv_layouts_kernel,
        out_shape=(heads_t, heads, heads_t, plain, plain, norms),
        grid=(b, t // rows),
        in_specs=[
            pl.BlockSpec((KV_CHUNK, KV_CHUNK), lambda bi, i: (0, 0)),
            pl.BlockSpec((1, rows, 3 * ATT_WIDTH), lambda bi, i: (bi, i, 0)),
        ],
        out_specs=(spec_t, spec_h, spec_t, spec_p, spec_p, spec_n),
        compiler_params=_params(("arbitrary", "arbitrary")),
        name="qkv_layouts",
    )(_key_row_permutation(), proj3)
    merge = lambda a: a.reshape((b * N_HEADS,) + a.shape[2:])
    return merge(qT), merge(k_h), merge(vT), k32, v32, kn


def _to_heads(src, tail, *, permute, transpose, key_norms=False):
    b, t_src = src.shape[:2]
    t = t_src + tail.shape[1]
    rows = KV_CHUNK if permute else math.gcd(math.gcd(t_src, tail.shape[1]), KV_CHUNK)
    assert t_src % rows == 0 and tail.shape[1] % rows == 0 and rows % LANES == 0
    n_src = t_src // rows
    if n_src == 0:
        src = tail.reshape(b, tail.shape[1], N_HEADS, HEAD_DIM)
    if transpose:
        out_shape = jax.ShapeDtypeStruct((b, N_HEADS, HEAD_DIM, t), BF16)
        out_spec = pl.BlockSpec((1, N_HEADS, HEAD_DIM, rows), lambda bi, i: (bi, 0, 0, i))
    else:
        out_shape = jax.ShapeDtypeStruct((b, N_HEADS, t, HEAD_DIM), BF16)
        out_spec = pl.BlockSpec((1, N_HEADS, rows, HEAD_DIM), lambda bi, i: (bi, 0, i, 0))
    if key_norms:
        out_shape = (out_shape, jax.ShapeDtypeStruct((b, t // rows, N_HEADS, LANES), F32))
        out_spec = (out_spec, pl.BlockSpec((1, 1, N_HEADS, LANES), lambda bi, i: (bi, i, 0, 0)))
    out = pl.pallas_call(
        functools.partial(_to_heads_kernel, permute=permute, transpose=transpose, n_src_blocks=n_src),
        out_shape=out_shape,
        grid=(b, t // rows),
        in_specs=[
            pl.BlockSpec((KV_CHUNK, KV_CHUNK), lambda bi, i: (0, 0)),
            pl.BlockSpec((1, rows, N_HEADS, HEAD_DIM), lambda bi, i: (bi, jnp.minimum(i, max(n_src - 1, 0)), 0, 0)),
            pl.BlockSpec((1, rows, ATT_WIDTH), lambda bi, i: (bi, jnp.maximum(i - n_src, 0), 0)),
        ],
        out_specs=out_spec,
        compiler_params=_params(("arbitrary", "arbitrary")),
        name="to_heads",
    )(_key_row_permutation(), src, tail)
    merge = lambda a: a.reshape((b * N_HEADS,) + a.shape[2:])
    return (merge(out[0]), out[1]) if key_norms else merge(out)


EXP2_UNDERFLOW = 160.0
SCORE_BOUND_SLACK = 1.01


def _softplus2(z):
    e = jnp.exp2(-jnp.abs(z))
    return jnp.maximum(z, 0.0) + jnp.log(1.0 + e) * math.log2(math.e)


def _attn_kernel(qT_ref, k_ref, vT_ref, kn_ref, g_ref, o_ref, z_ref, d_ref, w_ref, acc_ref, carry_ref,
                 *, past, tq):
    qi = pl.program_id(1)
    q_lo = past + qi * tq
    j_top = q_lo >> (KV_CHUNK.bit_length() - 1)

    qT = qT_ref[0]
    qpos = q_lo + lax.broadcasted_iota(jnp.int32, (SEGS, tq), 1)
    top_key = j_top * KV_CHUNK + SEG_LEN * lax.broadcasted_iota(jnp.int32, (SEGS, tq), 0) + (SEG_LEN - 1)

    def causal(i):
        return top_key - i < qpos

    def scores(j):
        base = pl.multiple_of(j * KV_CHUNK, KV_CHUNK)
        kc = k_ref[0, pl.ds(base, KV_CHUNK), :]
        return jnp.dot(kc, qT, preferred_element_type=F32)

    def add_values(j):
        base = pl.multiple_of(j * KV_CHUNK, KV_CHUNK)
        vc = vT_ref[0, :, pl.ds(base, KV_CHUNK)]
        acc_ref[...] += jnp.dot(vc, w_ref[...], preferred_element_type=F32)

    def visit(j, cur, masked, prefetch=True):
        if prefetch:
            z_ref[1 - cur] = scores(jnp.maximum(j - 1, 0))
        if not masked:
            add_values(j + 1)
        run = jnp.zeros((SEGS, tq), F32)
        for i in range(SEG_LEN):
            rows = slice(i * SEGS, (i + 1) * SEGS)
            z = z_ref[cur, rows, :]
            sp = _softplus2(z)
            if masked:
                sp = jnp.where(causal(i), sp, 0.0)
            run = run + sp
            d_ref[rows, :] = z - run
        tail = carry_ref[0:1, :]
        offs = [None] * SEGS
        for s in reversed(range(SEGS)):
            offs[s] = tail
            tail = tail + run[s:s + 1, :]
        carry_ref[...] = jnp.broadcast_to(tail, carry_ref.shape)
        off = jnp.concatenate(offs, axis=0)
        for i in range(SEG_LEN):
            rows = slice(i * SEGS, (i + 1) * SEGS)
            w = jnp.exp2(d_ref[rows, :] - off)
            if masked:
                w = jnp.where(causal(i), w, 0.0)
            w_ref[rows, :] = w.astype(BF16)

    kn = jnp.max(kn_ref[0], axis=0)
    head = pl.program_id(0) % N_HEADS
    kn = jnp.where(lax.broadcasted_iota(jnp.int32, kn.shape, 0) == head, kn, 0.0)
    k_norm = jnp.sqrt(jnp.max(kn, axis=0, keepdims=True)[:, 0:1])

    q32 = qT.astype(F32)
    q_norm = jnp.sqrt(jnp.sum(q32 * q32, axis=0, keepdims=True))
    dead_at = q_norm * k_norm * SCORE_BOUND_SLACK + EXP2_UNDERFLOW

    def rest_is_zero():
        return jnp.min(carry_ref[0:1, :] - dead_at) >= 0.0

    acc_ref[...] = jnp.zeros_like(acc_ref)
    carry_ref[...] = jnp.zeros_like(carry_ref)
    z_ref[0] = scores(j_top)

    visit(j_top, 0, True)
    if tq == KV_CHUNK:
        first = j_top >= 1
    else:
        first = jnp.logical_and(j_top >= 1, jnp.logical_not(rest_is_zero()))

    @pl.when(first)
    def _():
        visit(j_top - 1, 1, False, prefetch=False)

    stop = jnp.logical_or(jnp.logical_not(first), rest_is_zero())

    @pl.when(jnp.logical_and(jnp.logical_not(stop), j_top >= 2))
    def _():
        z_ref[0] = scores(j_top - 2)

    n_pairs = jnp.maximum(j_top - 1, 0) >> 1

    def more_pairs(c):
        t, done = c
        return jnp.logical_and(t < n_pairs, jnp.logical_not(done))

    def pair(c):
        t, _ = c
        j = j_top - 2 - 2 * t
        visit(j, 0, False)
        visit(j - 1, 1, False)
        return t + 1, rest_is_zero()

    t_end, done = lax.while_loop(more_pairs, pair, (jnp.int32(0), stop))
    odd_visit = jnp.logical_and(((j_top - 1) & 1) == 1, jnp.logical_not(done))

    @pl.when(odd_visit)
    def _():
        visit(0, 0, False)

    j_last = jnp.where(odd_visit, 0, jnp.where(first, j_top - 1 - 2 * t_end, j_top))
    add_values(j_last)

    out = acc_ref[...]
    ms = jnp.mean(out * out, axis=0, keepdims=True)
    y = (out * lax.rsqrt(ms + EPS)) * g_ref[0]
    o_ref[0] = y.T.astype(o_ref.dtype)


def _attention(qT, k_perm, vT_perm, key_norms, g_att, *, past, tq):
    bh_total, _, t_q = qT.shape
    b = bh_total // N_HEADS
    t_k = k_perm.shape[1]
    n_chunks = t_k // KV_CHUNK
    assert KV_CHUNK % tq == 0 and past % KV_CHUNK == 0 and t_q % tq == 0
    assert t_k % KV_CHUNK == 0 and t_k >= -(-(past + t_q) // KV_CHUNK) * KV_CHUNK
    assert key_norms.shape == (b, n_chunks, N_HEADS, LANES)
    kernel = functools.partial(_attn_kernel, past=past, tq=tq)
    return pl.pallas_call(
        kernel,
        out_shape=jax.ShapeDtypeStruct((b, t_q, ATT_WIDTH), BF16),
        grid=(b * N_HEADS, t_q // tq),
        in_specs=[
            pl.BlockSpec((1, HEAD_DIM, tq), lambda bh, i: (bh, 0, i)),
            pl.BlockSpec((1, t_k, HEAD_DIM), lambda bh, i: (bh, 0, 0)),
            pl.BlockSpec((1, HEAD_DIM, t_k), lambda bh, i: (bh, 0, 0)),
            pl.BlockSpec((1, n_chunks, N_HEADS, LANES), lambda bh, i: (bh // N_HEADS, 0, 0, 0)),
            pl.BlockSpec((1, HEAD_DIM, 1), lambda bh, i: (bh % N_HEADS, 0, 0)),
        ],
        out_specs=pl.BlockSpec((1, tq, HEAD_DIM), lambda bh, i: (bh // N_HEADS, i, bh % N_HEADS)),
        scratch_shapes=[
            pltpu.VMEM((2, KV_CHUNK, tq), F32),
            pltpu.VMEM((KV_CHUNK, tq), F32),
            pltpu.VMEM((KV_CHUNK, tq), BF16),
            pltpu.VMEM((HEAD_DIM, tq), F32),
            pltpu.VMEM((SUBLANES, tq), F32),
        ],
        compiler_params=_params(("arbitrary", "arbitrary")),
        name="sb_attention",
    )(qT, k_perm, vT_perm, key_norms, g_att.reshape(N_HEADS, HEAD_DIM, 1))


def _dwconv3_rows(cur, w_ref, full_ref, cols):
    n = cur.shape[0]
    full_ref[SUBLANES:SUBLANES + n, cols] = cur
    return (full_ref[SUBLANES - 2:SUBLANES - 2 + n, cols] * w_ref[0:1, cols]
            + full_ref[SUBLANES - 1:SUBLANES - 1 + n, cols] * w_ref[1:2, cols]
            + cur * w_ref[2:3, cols])


def _seq_geometry(tm, seq_len):
    sub_len = min(seq_len, tm)
    return sub_len, tm // sub_len, max(seq_len // tm, 1)


def _mix_kernel(att_ref, b_ref, c_ref, u_ref, cp_ref, up_ref, st_ref, wc_ref, gc_ref, woa_ref, wob_ref,
                x_ref, o_ref, nc_ref, full_ref, yc_ref, *, tm, seq_len):
    i = pl.program_id(0)
    sub_len, n_sub, tiles_per_seq = _seq_geometry(tm, seq_len)
    halo_rows = slice(0, SUBLANES)
    all_cols = slice(0, CONV_WIDTH)
    if tiles_per_seq > 1:
        @pl.when(i % tiles_per_seq == 0)
        def _():
            full_ref[halo_rows, :] = st_ref[0]

        @pl.when(i % tiles_per_seq != 0)
        def _():
            full_ref[halo_rows, :] = cp_ref[...] * up_ref[...]

    out = x_ref[...] + jnp.dot(att_ref[...], woa_ref[...], preferred_element_type=F32)
    cu = c_ref[...] * u_ref[...]
    for s in range(n_sub):
        rows = slice(s * sub_len, (s + 1) * sub_len)
        if tiles_per_seq == 1:
            full_ref[halo_rows, :] = st_ref[s]
        cur = cu[rows, :]
        y = b_ref[rows, :] * _dwconv3_rows(cur, wc_ref, full_ref, all_cols)
        for g in range(N_GROUPS):
            cols = slice(g * GROUP, (g + 1) * GROUP)
            yc_ref[rows, cols] = _rms_rows(y[:, cols], gc_ref[:, cols]).astype(BF16)
        if tiles_per_seq == 1:
            nc_ref[i * n_sub + s] = cur[sub_len - (CONV_K - 1):, :]

    o_ref[...] = out + jnp.dot(yc_ref[...], wob_ref[...], preferred_element_type=F32)

    if tiles_per_seq > 1:
        @pl.when(i % tiles_per_seq == tiles_per_seq - 1)
        def _():
            last = SUBLANES + sub_len
            nc_ref[i // tiles_per_seq] = full_ref[last - (CONV_K - 1):last, :]


def _state_rows(state):
    return jnp.pad(state, ((0, 0), (SUBLANES - (CONV_K - 1), 0), (0, 0)))


def _mix(att2d, proj, state, w_conv, g_conv, w_o_bf16, x2d, *, tm, seq_len):
    m = x2d.shape[0]
    n_seq = m // seq_len
    sub_len, n_sub, tiles_per_seq = _seq_geometry(tm, seq_len)
    rows_per_block = tm // SUBLANES
    col0 = 3 * ATT_WIDTH // CONV_WIDTH
    prev_map = lambda c: (lambda i: (jnp.maximum(i * rows_per_block - 1, 0), c))
    kernel = functools.partial(_mix_kernel, tm=tm, seq_len=seq_len)
    return pl.pallas_call(
        kernel,
        out_shape=(jax.ShapeDtypeStruct((m, D_MODEL), F32),
                   jax.ShapeDtypeStruct((n_seq, CONV_K - 1, CONV_WIDTH), F32)),
        grid=(m // tm,),
        in_specs=[
            pl.BlockSpec((tm, ATT_WIDTH), lambda i: (i, 0)),
            pl.BlockSpec((tm, CONV_WIDTH), lambda i: (i, col0)),
            pl.BlockSpec((tm, CONV_WIDTH), lambda i: (i, col0 + 1)),
            pl.BlockSpec((tm, CONV_WIDTH), lambda i: (i, col0 + 2)),
            pl.BlockSpec((SUBLANES, CONV_WIDTH), prev_map(col0 + 1)),
            pl.BlockSpec((SUBLANES, CONV_WIDTH), prev_map(col0 + 2)),
            pl.BlockSpec((n_sub, SUBLANES, CONV_WIDTH), lambda i: (i // tiles_per_seq, 0, 0)),
            pl.BlockSpec((CONV_K, CONV_WIDTH), lambda i: (0, 0)),
            pl.BlockSpec((1, CONV_WIDTH), lambda i: (0, 0)),
            pl.BlockSpec((ATT_WIDTH, D_MODEL), lambda i: (0, 0)),
            pl.BlockSpec((CONV_WIDTH, D_MODEL), lambda i: (1, 0)),
            pl.BlockSpec((tm, D_MODEL), lambda i: (i, 0)),
        ],
        out_specs=(pl.BlockSpec((tm, D_MODEL), lambda i: (i, 0)),
                   pl.BlockSpec((n_seq, CONV_K - 1, CONV_WIDTH), lambda i: (0, 0, 0))),
        scratch_shapes=[pltpu.VMEM((SUBLANES + sub_len, CONV_WIDTH), F32),
                        pltpu.VMEM((tm, CONV_WIDTH), BF16)],
        compiler_params=_params(("arbitrary",)),
        name="mix_out_proj",
    )(att2d, proj, proj, proj, proj, proj, _state_rows(state), w_conv, g_conv.reshape(1, CONV_WIDTH),
      w_o_bf16, w_o_bf16, x2d)


def _ffn_kernel(x_ref, g_ref, wg_ref, wu_ref, wc_ref, st_ref, wd_ref, gf_ref, o_ref, nf_ref,
                h_ref, halo_ref, full_ref, act_ref, *, tm, seq_len, n_ff):
    i = pl.program_id(0)
    j = pl.program_id(1)
    sub_len, n_sub, tiles_per_seq = _seq_geometry(tm, seq_len)
    tf = wg_ref.shape[1]
    half = tf // 2

    @pl.when(j == 0)
    def _():
        x = x_ref[...]
        h_ref[...] = _rms_rows(x, g_ref[...]).astype(BF16)
        o_ref[...] = x

    halo_rows = slice(0, SUBLANES)
    if tiles_per_seq > 1:
        @pl.when(i % tiles_per_seq == 0)
        def _():
            halo_ref[j] = st_ref[0]

    h = h_ref[...]
    for c in range(2):
        cols = slice(c * half, (c + 1) * half)
        gate = jnp.dot(h, wg_ref[:, cols], preferred_element_type=F32)
        up = jnp.dot(h, wu_ref[:, cols], preferred_element_type=F32)
        for s in range(n_sub):
            rows = slice(s * sub_len, (s + 1) * sub_len)
            if tiles_per_seq == 1:
                full_ref[halo_rows, cols] = st_ref[s, :, cols]
            else:
                full_ref[halo_rows, cols] = halo_ref[j, :, cols]
            cur = gate[rows, :]
            gc = _dwconv3_rows(cur, wc_ref, full_ref, cols)
            act_ref[rows, cols] = (gc * jax.nn.sigmoid(gc) * up[rows, :]).astype(BF16)
            if tiles_per_seq == 1:
                nf_ref[i * n_sub + s, j, :, cols] = cur[sub_len - (CONV_K - 1):, :]
            else:
                halo_ref[j, :, cols] = cur[sub_len - SUBLANES:, :]

    o_ref[...] += jnp.dot(act_ref[...], wd_ref[...], preferred_element_type=F32)

    if tiles_per_seq > 1:
        @pl.when(i % tiles_per_seq == tiles_per_seq - 1)
        def _():
            nf_ref[i // tiles_per_seq, j] = halo_ref[j, SUBLANES - (CONV_K - 1):, :]

    @pl.when(j == n_ff - 1)
    def _():
        o_ref[...] = _rms_rows(o_ref[...], gf_ref[...])


def _ffn(x2d, g_ffn, w_gate_up_bf16, w_ffn_conv, state, w_down_bf16, g_final, *, tm, seq_len):
    m = x2d.shape[0]
    n_seq = m // seq_len
    tf = FF_TILE
    n_ff = D_FF // tf
    sub_len, n_sub, tiles_per_seq = _seq_geometry(tm, seq_len)
    kernel = functools.partial(_ffn_kernel, tm=tm, seq_len=seq_len, n_ff=n_ff)
    f32_b, bf16_b = 4, 2
    vmem_bytes = (4 * tm * D_MODEL * f32_b + 2 * 3 * D_MODEL * tf * bf16_b + tm * D_MODEL * bf16_b
                  + tm * tf * bf16_b + (SUBLANES + sub_len) * tf * f32_b + 4 * tm * tf * f32_b)
    y, new_ffn = pl.pallas_call(
        kernel,
        out_shape=(jax.ShapeDtypeStruct((m, D_MODEL), F32),
                   jax.ShapeDtypeStruct((n_seq, n_ff, CONV_K - 1, tf), F32)),
        grid=(m // tm, n_ff),
        in_specs=[
            pl.BlockSpec((tm, D_MODEL), lambda i, j: (i, 0)),
            pl.BlockSpec((1, D_MODEL), lambda i, j: (0, 0)),
            pl.BlockSpec((D_MODEL, tf), lambda i, j: (0, j)),
            pl.BlockSpec((D_MODEL, tf), lambda i, j: (0, j + n_ff)),
            pl.BlockSpec((CONV_K, tf), lambda i, j: (0, j)),
            pl.BlockSpec((n_sub, SUBLANES, tf), lambda i, j: (i // tiles_per_seq, 0, j)),
            pl.BlockSpec((tf, D_MODEL), lambda i, j: (j, 0)),
            pl.BlockSpec((1, D_MODEL), lambda i, j: (0, 0)),
        ],
        out_specs=(pl.BlockSpec((tm, D_MODEL), lambda i, j: (i, 0)),
                   pl.BlockSpec((n_seq, n_ff, CONV_K - 1, tf), lambda i, j: (0, 0, 0, 0))),
        scratch_shapes=[pltpu.VMEM((tm, D_MODEL), BF16),
                        pltpu.VMEM((n_ff, SUBLANES, tf), F32),
                        pltpu.VMEM((SUBLANES + sub_len, tf), F32),
                        pltpu.VMEM((tm, tf), BF16)],
        compiler_params=_params(("arbitrary", "arbitrary"), vmem_bytes),
        name="conv_ffn",
    )(x2d, g_ffn.reshape(1, D_MODEL), w_gate_up_bf16, w_gate_up_bf16, w_ffn_conv, _state_rows(state),
      w_down_bf16, g_final.reshape(1, D_MODEL))
    return y, new_ffn.transpose(0, 2, 1, 3).reshape(n_seq, CONV_K - 1, D_FF)


def _tiles(batch, seq_len):
    rows = batch * seq_len
    if rows >= 4096:
        return dict(tm_proj=1024, tm_mix=256, tm_ffn=512, tq=KV_CHUNK)
    return dict(tm_proj=rows, tm_mix=rows, tm_ffn=rows, tq=LANES)


def _trunk(x, cache_k, cache_v, state_conv, state_ffn, weights):
    g_mix, w_in, w_conv, g_att, g_conv, w_o, g_ffn, w_gate_up, w_ffn_conv, w_down, g_final = weights
    b, t, _ = x.shape
    past = cache_k.shape[1]
    tiles = _tiles(b, t)
    tq = tiles["tq"]
    x2d = x.reshape(b * t, D_MODEL)

    proj = _in_proj(x2d, g_mix, w_in, tm=tiles["tm_proj"])
    proj3 = proj.reshape(b, t, IN_WIDTH)
    tq_pad = -(-t // tq) * tq
    tk_pad = -(-(past + tq_pad) // KV_CHUNK) * KV_CHUNK
    if past == 0 and tq_pad == t and tk_pad == t:
        q_h, k_h, v_h, k, v, k_norms = _qkv_layouts(proj3)
    else:
        k = proj3[:, :, ATT_WIDTH:2 * ATT_WIDTH]
        v = proj3[:, :, 2 * ATT_WIDTH:3 * ATT_WIDTH]

        def key_heads(cache, new, **kw):
            tail = jnp.pad(new, ((0, 0), (0, tk_pad - past - t), (0, 0)))
            return _to_heads(cache, tail, permute=True, **kw)

        q_pad = jnp.pad(proj3[:, :, :ATT_WIDTH], ((0, 0), (0, tq_pad - t), (0, 0)))
        q_h = _to_heads(jnp.zeros((b, 0, N_HEADS, HEAD_DIM), F32), q_pad, permute=False, transpose=True)
        k_h, k_norms = key_heads(cache_k, k, transpose=False, key_norms=True)
        v_h = key_heads(cache_v, v, transpose=True)
    att = _attention(q_h, k_h, v_h, k_norms, g_att, past=past, tq=tq)
    att2d = att[:, :t, :].reshape(b * t, ATT_WIDTH)

    x1, new_conv = _mix(att2d, proj, state_conv, w_conv, g_conv, w_o, x2d, tm=tiles["tm_mix"], seq_len=t)
    y, new_ffn = _ffn(x1, g_ffn, w_gate_up, w_ffn_conv, state_ffn, w_down, g_final, tm=tiles["tm_ffn"], seq_len=t)
    shape4 = (b, t, N_HEADS, HEAD_DIM)
    return y.reshape(b, t, D_MODEL), k.reshape(shape4), v.reshape(shape4), new_conv, new_ffn


def kernel(x_prompt, x_sample, cache_k, cache_v, state_conv, state_ffn_conv, g_mix, w_in, w_conv, g_att_out,
           g_conv_out, w_o, g_ffn, w_gate_up, w_ffn_conv, w_down, g_final):
    depth = g_mix.shape[0]
    assert depth == 1
    weights = (g_mix[0], w_in[0].astype(BF16), w_conv[0], g_att_out[0], g_conv_out[0], w_o[0].astype(BF16),
               g_ffn[0], w_gate_up[0].astype(BF16), w_ffn_conv[0], w_down[0].astype(BF16), g_final)
    bp = x_prompt.shape[0]
    dt = x_prompt.dtype
    empty = jnp.zeros((bp, 0, N_HEADS, HEAD_DIM), dt)
    y_p, k_p, v_p, c_p, f_p = _trunk(
        x_prompt, empty, empty, jnp.zeros((bp, CONV_K - 1, CONV_WIDTH), dt), jnp.zeros((bp, CONV_K - 1, D_FF), dt),
        weights)
    y_s, k_s, v_s, c_s, f_s = _trunk(x_sample, cache_k[0], cache_v[0], state_conv[0], state_ffn_conv[0], weights)
    return (y_p, y_s, k_p[None], v_p[None], c_p[None], f_p[None], k_s[None], v_s[None], c_s[None], f_s[None])
```

```python
import functools
import math

import jax
import jax.numpy as jnp
import numpy as np
from jax import lax
from jax.experimental import pallas as pl
from jax.experimental.pallas import tpu as pltpu

F32 = jnp.float32
BF16 = jnp.bfloat16

D_MODEL = 2048
N_HEADS = 8
HEAD_DIM = 128
ATT_WIDTH = N_HEADS * HEAD_DIM
N_GROUPS = 8
GROUP = 128
CONV_WIDTH = N_GROUPS * GROUP
IN_WIDTH = 3 * ATT_WIDTH + 3 * CONV_WIDTH
D_FF = 5632
CONV_K = 3
EPS = 1e-6
Q_SCALE = (HEAD_DIM ** -0.5) * math.log2(math.e)

SUBLANES = 8
LANES = 128
SEGS = 2 * SUBLANES
SEG_LEN = 32
KV_CHUNK = SEGS * SEG_LEN
VMEM_LIMIT_BYTES = 48 * 1024 * 1024

FF_TILE = 512
IN_PROJ_COL_TILE = 1024
assert ATT_WIDTH == IN_PROJ_COL_TILE and CONV_WIDTH == IN_PROJ_COL_TILE


def _params(sem, vmem_limit_bytes=VMEM_LIMIT_BYTES):
    return pltpu.CompilerParams(dimension_semantics=sem, vmem_limit_bytes=vmem_limit_bytes)


def _rms_rows(x, g):
    ms = jnp.mean(x * x, axis=-1, keepdims=True)
    return (x * lax.rsqrt(ms + EPS)) * g


def _tile_columns(w, tile):
    rows, cols = w.shape
    return w.reshape(rows, cols // tile, tile).transpose(1, 0, 2)


def _key_row_permutation():
    p = np.zeros((KV_CHUNK, KV_CHUNK), np.float32)
    for i in range(SEG_LEN):
        for s in range(SEGS):
            p[SEGS * i + s, SEG_LEN * s + SEG_LEN - 1 - i] = 1.0
    return jnp.asarray(p, BF16)


def _in_proj_kernel(x_ref, g_ref, w_ref, o_ref, h_ref):
    j = pl.program_id(1)

    @pl.when(j == 0)
    def _():
        h_ref[...] = _rms_rows(x_ref[...], g_ref[...]).astype(BF16)

    acc = jnp.dot(h_ref[...], w_ref[...], preferred_element_type=F32)
    o_ref[...] = acc * jnp.where(j == 0, Q_SCALE, 1.0).astype(F32)


def _in_proj(x2d, g_mix, w_in_bf16, *, tm):
    m = x2d.shape[0]
    tn = IN_PROJ_COL_TILE
    return pl.pallas_call(
        _in_proj_kernel,
        out_shape=jax.ShapeDtypeStruct((m, IN_WIDTH), F32),
        grid=(m // tm, IN_WIDTH // tn),
        in_specs=[
            pl.BlockSpec((tm, D_MODEL), lambda i, j: (i, 0)),
            pl.BlockSpec((1, D_MODEL), lambda i, j: (0, 0)),
            pl.BlockSpec((D_MODEL, tn), lambda i, j: (0, j)),
        ],
        out_specs=pl.BlockSpec((tm, tn), lambda i, j: (i, j)),
        scratch_shapes=[pltpu.VMEM((tm, D_MODEL), BF16)],
        compiler_params=_params(("arbitrary", "arbitrary")),
        name="in_proj",
    )(x2d, g_mix.reshape(1, D_MODEL), w_in_bf16)


def _store_heads(x32, p_ref, o_ref, *, permute, transpose):
    if permute:
        x32 = jnp.dot(p_ref[...], x32.astype(BF16), preferred_element_type=F32)
    for h in range(N_HEADS):
        xh = x32[:, h * HEAD_DIM:(h + 1) * HEAD_DIM]
        o_ref[0, h] = (xh.T if transpose else xh).astype(BF16)


def _max_key_norm2(k32):
    kb = k32.astype(BF16).astype(F32)
    out = []
    for h in range(N_HEADS):
        kh = kb[:, h * HEAD_DIM:(h + 1) * HEAD_DIM]
        ss = jnp.sum(kh * kh, axis=1, keepdims=True)
        out.append(jnp.broadcast_to(jnp.max(ss, axis=0, keepdims=True), (1, LANES)))
    return jnp.concatenate(out, axis=0)


def _to_heads_kernel(p_ref, x_ref, tail_ref, o_ref, *kn_ref, permute, transpose, n_src_blocks):
    src32 = x_ref[0].reshape(x_ref.shape[1], ATT_WIDTH)
    x32 = jnp.where(pl.program_id(1) < n_src_blocks, src32, tail_ref[0])
    for ref in kn_ref:
        ref[0, 0] = _max_key_norm2(x32)
    _store_heads(x32, p_ref, o_ref, permute=permute, transpose=transpose)


def _qkv_layouts_kernel(p_ref, x_ref, qT_ref, k_ref, vT_ref, k32_ref, v32_ref, kn_ref):
    q32 = x_ref[0, :, 0:ATT_WIDTH]
    k32 = x_ref[0, :, ATT_WIDTH:2 * ATT_WIDTH]
    v32 = x_ref[0, :, 2 * ATT_WIDTH:3 * ATT_WIDTH]
    kn_ref[0, 0] = _max_key_norm2(k32)
    k32_ref[0] = k32.reshape(k32.shape[0], N_HEADS, HEAD_DIM)
    v32_ref[0] = v32.reshape(v32.shape[0], N_HEADS, HEAD_DIM)
    _store_heads(q32, p_ref, qT_ref, permute=False, transpose=True)
    _store_heads(k32, p_ref, k_ref, permute=True, transpose=False)
    _store_heads(v32, p_ref, vT_ref, permute=True, transpose=True)


def _qkv_layouts(proj3):
    b, t, _ = proj3.shape
    rows = KV_CHUNK
    assert t % rows == 0
    heads_t = jax.ShapeDtypeStruct((b, N_HEADS, HEAD_DIM, t), BF16)
    heads = jax.ShapeDtypeStruct((b, N_HEADS, t, HEAD_DIM), BF16)
    plain = jax.ShapeDtypeStruct((b, t, N_HEADS, HEAD_DIM), F32)
    spec_t = pl.BlockSpec((1, N_HEADS, HEAD_DIM, rows), lambda bi, i: (bi, 0, 0, i))
    spec_h = pl.BlockSpec((1, N_HEADS, rows, HEAD_DIM), lambda bi, i: (bi, 0, i, 0))
    spec_p = pl.BlockSpec((1, rows, N_HEADS, HEAD_DIM), lambda bi, i: (bi, i, 0, 0))
    norms = jax.ShapeDtypeStruct((b, t // rows, N_HEADS, LANES), F32)
    spec_n = pl.BlockSpec((1, 1, N_HEADS, LANES), lambda bi, i: (bi, i, 0, 0))
    qT, k_h, vT, k32, v32, kn = pl.pallas_call(
        _qkv_layouts_kernel,
        out_shape=(heads_t, heads, heads_t, plain, plain, norms),
        grid=(b, t // rows),
        in_specs=[
            pl.BlockSpec((KV_CHUNK, KV_CHUNK), lambda bi, i: (0, 0)),
            pl.BlockSpec((1, rows, 3 * ATT_WIDTH), lambda bi, i: (bi, i, 0)),
        ],
        out_specs=(spec_t, spec_h, spec_t, spec_p, spec_p, spec_n),
        compiler_params=_params(("arbitrary", "arbitrary")),
        name="qkv_layouts",
    )(_key_row_permutation(), proj3)
    merge = lambda a: a.reshape((b * N_HEADS,) + a.shape[2:])
    return merge(qT), merge(k_h), merge(vT), k32, v32, kn


def _to_heads(src, tail, *, permute, transpose, key_norms=False):
    b, t_src = src.shape[:2]
    t = t_src + tail.shape[1]
    rows = KV_CHUNK if permute else math.gcd(math.gcd(t_src, tail.shape[1]), KV_CHUNK)
    assert t_src % rows == 0 and tail.shape[1] % rows == 0 and rows % LANES == 0
    n_src = t_src // rows
    if n_src == 0:
        src = tail.reshape(b, tail.shape[1], N_HEADS, HEAD_DIM)
    if transpose:
        out_shape = jax.ShapeDtypeStruct((b, N_HEADS, HEAD_DIM, t), BF16)
        out_spec = pl.BlockSpec((1, N_HEADS, HEAD_DIM, rows), lambda bi, i: (bi, 0, 0, i))
    else:
        out_shape = jax.ShapeDtypeStruct((b, N_HEADS, t, HEAD_DIM), BF16)
        out_spec = pl.BlockSpec((1, N_HEADS, rows, HEAD_DIM), lambda bi, i: (bi, 0, i, 0))
    if key_norms:
        out_shape = (out_shape, jax.ShapeDtypeStruct((b, t // rows, N_HEADS, LANES), F32))
        out_spec = (out_spec, pl.BlockSpec((1, 1, N_HEADS, LANES), lambda bi, i: (bi, i, 0, 0)))
    out = pl.pallas_call(
        functools.partial(_to_heads_kernel, permute=permute, transpose=transpose, n_src_blocks=n_src),
        out_shape=out_shape,
        grid=(b, t // rows),
        in_specs=[
            pl.BlockSpec((KV_CHUNK, KV_CHUNK), lambda bi, i: (0, 0)),
            pl.BlockSpec((1, rows, N_HEADS, HEAD_DIM), lambda bi, i: (bi, jnp.minimum(i, max(n_src - 1, 0)), 0, 0)),
            pl.BlockSpec((1, rows, ATT_WIDTH), lambda bi, i: (bi, jnp.maximum(i - n_src, 0), 0)),
        ],
        out_specs=out_spec,
        compiler_params=_params(("arbitrary", "arbitrary")),
        name="to_heads",
    )(_key_row_permutation(), src, tail)
    merge = lambda a: a.reshape((b * N_HEADS,) + a.shape[2:])
    return (merge(out[0]), out[1]) if key_norms else merge(out)


EXP2_UNDERFLOW = 160.0
SCORE_BOUND_SLACK = 1.01


def _softplus2(z):
    e = jnp.exp2(-jnp.abs(z))
    return jnp.maximum(z, 0.0) + jnp.log(1.0 + e) * math.log2(math.e)


def _attn_kernel(qT_ref, k_ref, vT_ref, kn_ref, g_ref, o_ref, z_ref, d_ref, w_ref, acc_ref, carry_ref,
                 *, past, tq):
    qi = pl.program_id(1)
    q_lo = past + qi * tq
    j_top = q_lo >> (KV_CHUNK.bit_length() - 1)

    qT = qT_ref[0]
    qpos = q_lo + lax.broadcasted_iota(jnp.int32, (SEGS, tq), 1)
    top_key = j_top * KV_CHUNK + SEG_LEN * lax.broadcasted_iota(jnp.int32, (SEGS, tq), 0) + (SEG_LEN - 1)

    def causal(i):
        return top_key - i < qpos

    def scores(j):
        base = pl.multiple_of(j * KV_CHUNK, KV_CHUNK)
        kc = k_ref[0, pl.ds(base, KV_CHUNK), :]
        return jnp.dot(kc, qT, preferred_element_type=F32)

    def add_values(j):
        base = pl.multiple_of(j * KV_CHUNK, KV_CHUNK)
        vc = vT_ref[0, :, pl.ds(base, KV_CHUNK)]
        acc_ref[...] += jnp.dot(vc, w_ref[...], preferred_element_type=F32)

    def visit(j, cur, masked, prefetch=True):
        if prefetch:
            z_ref[1 - cur] = scores(jnp.maximum(j - 1, 0))
        if not masked:
            add_values(j + 1)
        run = jnp.zeros((SEGS, tq), F32)
        for i in range(SEG_LEN):
            rows = slice(i * SEGS, (i + 1) * SEGS)
            z = z_ref[cur, rows, :]
            sp = _softplus2(z)
            if masked:
                sp = jnp.where(causal(i), sp, 0.0)
            run = run + sp
            d_ref[rows, :] = z - run
        tail = carry_ref[0:1, :]
        offs = [None] * SEGS
        for s in reversed(range(SEGS)):
            offs[s] = tail
            tail = tail + run[s:s + 1, :]
        carry_ref[...] = jnp.broadcast_to(tail, carry_ref.shape)
        off = jnp.concatenate(offs, axis=0)
        for i in range(SEG_LEN):
            rows = slice(i * SEGS, (i + 1) * SEGS)
            w = jnp.exp2(d_ref[rows, :] - off)
            if masked:
                w = jnp.where(causal(i), w, 0.0)
            w_ref[rows, :] = w.astype(BF16)

    kn = jnp.max(kn_ref[0], axis=0)
    head = pl.program_id(0) % N_HEADS
    kn = jnp.where(lax.broadcasted_iota(jnp.int32, kn.shape, 0) == head, kn, 0.0)
    k_norm = jnp.sqrt(jnp.max(kn, axis=0, keepdims=True)[:, 0:1])

    q32 = qT.astype(F32)
    q_norm = jnp.sqrt(jnp.sum(q32 * q32, axis=0, keepdims=True))
    dead_at = q_norm * k_norm * SCORE_BOUND_SLACK + EXP2_UNDERFLOW

    def rest_is_zero():
        return jnp.min(carry_ref[0:1, :] - dead_at) >= 0.0

    acc_ref[...] = jnp.zeros_like(acc_ref)
    carry_ref[...] = jnp.zeros_like(carry_ref)
    z_ref[0] = scores(j_top)

    visit(j_top, 0, True)
    if tq == KV_CHUNK:
        first = j_top >= 1
    else:
        first = jnp.logical_and(j_top >= 1, jnp.logical_not(rest_is_zero()))

    @pl.when(first)
    def _():
        visit(j_top - 1, 1, False, prefetch=False)

    stop = jnp.logical_or(jnp.logical_not(first), rest_is_zero())

    @pl.when(jnp.logical_and(jnp.logical_not(stop), j_top >= 2))
    def _():
        z_ref[0] = scores(j_top - 2)

    n_pairs = jnp.maximum(j_top - 1, 0) >> 1

    def more_pairs(c):
        t, done = c
        return jnp.logical_and(t < n_pairs, jnp.logical_not(done))

    def pair(c):
        t, _ = c
        j = j_top - 2 - 2 * t
        visit(j, 0, False)
        visit(j - 1, 1, False)
        return t + 1, rest_is_zero()

    t_end, done = lax.while_loop(more_pairs, pair, (jnp.int32(0), stop))
    odd_visit = jnp.logical_and(((j_top - 1) & 1) == 1, jnp.logical_not(done))

    @pl.when(odd_visit)
    def _():
        visit(0, 0, False)

    j_last = jnp.where(odd_visit, 0, jnp.where(first, j_top - 1 - 2 * t_end, j_top))
    add_values(j_last)

    out = acc_ref[...]
    ms = jnp.mean(out * out, axis=0, keepdims=True)
    y = (out * lax.rsqrt(ms + EPS)) * g_ref[0]
    o_ref[0] = y.T.astype(o_ref.dtype)


def _attention(qT, k_perm, vT_perm, key_norms, g_att, *, past, tq):
    bh_total, _, t_q = qT.shape
    b = bh_total // N_HEADS
    t_k = k_perm.shape[1]
    n_chunks = t_k // KV_CHUNK
    assert KV_CHUNK % tq == 0 and past % KV_CHUNK == 0 and t_q % tq == 0
    assert t_k % KV_CHUNK == 0 and t_k >= -(-(past + t_q) // KV_CHUNK) * KV_CHUNK
    assert key_norms.shape == (b, n_chunks, N_HEADS, LANES)
    kernel = functools.partial(_attn_kernel, past=past, tq=tq)
    return pl.pallas_call(
        kernel,
        out_shape=jax.ShapeDtypeStruct((b, t_q, ATT_WIDTH), BF16),
        grid=(b * N_HEADS, t_q // tq),
        in_specs=[
            pl.BlockSpec((1, HEAD_DIM, tq), lambda bh, i: (bh, 0, i)),
            pl.BlockSpec((1, t_k, HEAD_DIM), lambda bh, i: (bh, 0, 0)),
            pl.BlockSpec((1, HEAD_DIM, t_k), lambda bh, i: (bh, 0, 0)),
            pl.BlockSpec((1, n_chunks, N_HEADS, LANES), lambda bh, i: (bh // N_HEADS, 0, 0, 0)),
            pl.BlockSpec((1, HEAD_DIM, 1), lambda bh, i: (bh % N_HEADS, 0, 0)),
        ],
        out_specs=pl.BlockSpec((1, tq, HEAD_DIM), lambda bh, i: (bh // N_HEADS, i, bh % N_HEADS)),
        scratch_shapes=[
            pltpu.VMEM((2, KV_CHUNK, tq), F32),
            pltpu.VMEM((KV_CHUNK, tq), F32),
            pltpu.VMEM((KV_CHUNK, tq), BF16),
            pltpu.VMEM((HEAD_DIM, tq), F32),
            pltpu.VMEM((SUBLANES, tq), F32),
        ],
        compiler_params=_params(("arbitrary", "arbitrary")),
        name="sb_attention",
    )(qT, k_perm, vT_perm, key_norms, g_att.reshape(N_HEADS, HEAD_DIM, 1))


def _dwconv3_rows(cur, w_ref, full_ref, cols):
    n = cur.shape[0]
    full_ref[SUBLANES:SUBLANES + n, cols] = cur
    return (full_ref[SUBLANES - 2:SUBLANES - 2 + n, cols] * w_ref[0:1, cols]
            + full_ref[SUBLANES - 1:SUBLANES - 1 + n, cols] * w_ref[1:2, cols]
            + cur * w_ref[2:3, cols])


def _seq_geometry(tm, seq_len):
    sub_len = min(seq_len, tm)
    return sub_len, tm // sub_len, max(seq_len // tm, 1)


def _mix_kernel(att_ref, b_ref, c_ref, u_ref, cp_ref, up_ref, st_ref, wc_ref, gc_ref, woa_ref, wob_ref,
                x_ref, o_ref, nc_ref, full_ref, yc_ref, *, tm, seq_len):
    i = pl.program_id(0)
    sub_len, n_sub, tiles_per_seq = _seq_geometry(tm, seq_len)
    halo_rows = slice(0, SUBLANES)
    all_cols = slice(0, CONV_WIDTH)
    if tiles_per_seq > 1:
        @pl.when(i % tiles_per_seq == 0)
        def _():
            full_ref[halo_rows, :] = st_ref[0]

        @pl.when(i % tiles_per_seq != 0)
        def _():
            full_ref[halo_rows, :] = cp_ref[...] * up_ref[...]

    out = x_ref[...] + jnp.dot(att_ref[...], woa_ref[...], preferred_element_type=F32)
    cu = c_ref[...] * u_ref[...]
    for s in range(n_sub):
        rows = slice(s * sub_len, (s + 1) * sub_len)
        if tiles_per_seq == 1:
            full_ref[halo_rows, :] = st_ref[s]
        cur = cu[rows, :]
        y = b_ref[rows, :] * _dwconv3_rows(cur, wc_ref, full_ref, all_cols)
        for g in range(N_GROUPS):
            cols = slice(g * GROUP, (g + 1) * GROUP)
            yc_ref[rows, cols] = _rms_rows(y[:, cols], gc_ref[:, cols]).astype(BF16)
        if tiles_per_seq == 1:
            nc_ref[i * n_sub + s] = cur[sub_len - (CONV_K - 1):, :]

    o_ref[...] = out + jnp.dot(yc_ref[...], wob_ref[...], preferred_element_type=F32)

    if tiles_per_seq > 1:
        @pl.when(i % tiles_per_seq == tiles_per_seq - 1)
        def _():
            last = SUBLANES + sub_len
            nc_ref[i // tiles_per_seq] = full_ref[last - (CONV_K - 1):last, :]


def _state_rows(state):
    return jnp.pad(state, ((0, 0), (SUBLANES - (CONV_K - 1), 0), (0, 0)))


def _mix(att2d, proj, state, w_conv, g_conv, w_o_bf16, x2d, *, tm, seq_len):
    m = x2d.shape[0]
    n_seq = m // seq_len
    sub_len, n_sub, tiles_per_seq = _seq_geometry(tm, seq_len)
    rows_per_block = tm // SUBLANES
    col0 = 3 * ATT_WIDTH // CONV_WIDTH
    prev_map = lambda c: (lambda i: (jnp.maximum(i * rows_per_block - 1, 0), c))
    kernel = functools.partial(_mix_kernel, tm=tm, seq_len=seq_len)
    return pl.pallas_call(
        kernel,
        out_shape=(jax.ShapeDtypeStruct((m, D_MODEL), F32),
                   jax.ShapeDtypeStruct((n_seq, CONV_K - 1, CONV_WIDTH), F32)),
        grid=(m // tm,),
        in_specs=[
            pl.BlockSpec((tm, ATT_WIDTH), lambda i: (i, 0)),
            pl.BlockSpec((tm, CONV_WIDTH), lambda i: (i, col0)),
            pl.BlockSpec((tm, CONV_WIDTH), lambda i: (i, col0 + 1)),
            pl.BlockSpec((tm, CONV_WIDTH), lambda i: (i, col0 + 2)),
            pl.BlockSpec((SUBLANES, CONV_WIDTH), prev_map(col0 + 1)),
            pl.BlockSpec((SUBLANES, CONV_WIDTH), prev_map(col0 + 2)),
            pl.BlockSpec((n_sub, SUBLANES, CONV_WIDTH), lambda i: (i // tiles_per_seq, 0, 0)),
            pl.BlockSpec((CONV_K, CONV_WIDTH), lambda i: (0, 0)),
            pl.BlockSpec((1, CONV_WIDTH), lambda i: (0, 0)),
            pl.BlockSpec((ATT_WIDTH, D_MODEL), lambda i: (0, 0)),
            pl.BlockSpec((CONV_WIDTH, D_MODEL), lambda i: (1, 0)),
            pl.BlockSpec((tm, D_MODEL), lambda i: (i, 0)),
        ],
        out_specs=(pl.BlockSpec((tm, D_MODEL), lambda i: (i, 0)),
                   pl.BlockSpec((n_seq, CONV_K - 1, CONV_WIDTH), lambda i: (0, 0, 0))),
        scratch_shapes=[pltpu.VMEM((SUBLANES + sub_len, CONV_WIDTH), F32),
                        pltpu.VMEM((tm, CONV_WIDTH), BF16)],
        compiler_params=_params(("arbitrary",)),
        name="mix_out_proj",
    )(att2d, proj, proj, proj, proj, proj, _state_rows(state), w_conv, g_conv.reshape(1, CONV_WIDTH),
      w_o_bf16, w_o_bf16, x2d)


def _ffn_kernel(x_ref, g_ref, wg_ref, wu_ref, wc_ref, st_ref, wd_ref, gf_ref, o_ref, nf_ref,
                h_ref, halo_ref, full_ref, act_ref, *, tm, seq_len, n_ff):
    i = pl.program_id(0)
    j = pl.program_id(1)
    sub_len, n_sub, tiles_per_seq = _seq_geometry(tm, seq_len)
    tf = wg_ref.shape[2]
    half = tf // 2

    @pl.when(j == 0)
    def _():
        x = x_ref[...]
        h_ref[...] = _rms_rows(x, g_ref[...]).astype(BF16)
        o_ref[...] = x

    halo_rows = slice(0, SUBLANES)
    if tiles_per_seq > 1:
        @pl.when(i % tiles_per_seq == 0)
        def _():
            halo_ref[j] = st_ref[0]

    h = h_ref[...]
    for c in range(2):
        cols = slice(c * half, (c + 1) * half)
        gate = jnp.dot(h, wg_ref[0, :, cols], preferred_element_type=F32)
        up = jnp.dot(h, wu_ref[0, :, cols], preferred_element_type=F32)
        for s in range(n_sub):
            rows = slice(s * sub_len, (s + 1) * sub_len)
            if tiles_per_seq == 1:
                full_ref[halo_rows, cols] = st_ref[s, :, cols]
            else:
                full_ref[halo_rows, cols] = halo_ref[j, :, cols]
            cur = gate[rows, :]
            gc = _dwconv3_rows(cur, wc_ref, full_ref, cols)
            act_ref[rows, cols] = (gc * jax.nn.sigmoid(gc) * up[rows, :]).astype(BF16)
            if tiles_per_seq == 1:
                nf_ref[i * n_sub + s, j, :, cols] = cur[sub_len - (CONV_K - 1):, :]
            else:
                halo_ref[j, :, cols] = cur[sub_len - SUBLANES:, :]

    o_ref[...] += jnp.dot(act_ref[...], wd_ref[...], preferred_element_type=F32)

    if tiles_per_seq > 1:
        @pl.when(i % tiles_per_seq == tiles_per_seq - 1)
        def _():
            nf_ref[i // tiles_per_seq, j] = halo_ref[j, SUBLANES - (CONV_K - 1):, :]

    @pl.when(j == n_ff - 1)
    def _():
        o_ref[...] = _rms_rows(o_ref[...], gf_ref[...])


def _ffn(x2d, g_ffn, w_gate_up_bf16, w_ffn_conv, state, w_down_bf16, g_final, *, tm, seq_len):
    m = x2d.shape[0]
    n_seq = m // seq_len
    tf = FF_TILE
    n_ff = D_FF // tf
    sub_len, n_sub, tiles_per_seq = _seq_geometry(tm, seq_len)
    kernel = functools.partial(_ffn_kernel, tm=tm, seq_len=seq_len, n_ff=n_ff)
    f32_b, bf16_b = 4, 2
    vmem_bytes = (4 * tm * D_MODEL * f32_b + 2 * 3 * D_MODEL * tf * bf16_b + tm * D_MODEL * bf16_b
                  + tm * tf * bf16_b + (SUBLANES + sub_len) * tf * f32_b + 4 * tm * tf * f32_b)
    y, new_ffn = pl.pallas_call(
        kernel,
        out_shape=(jax.ShapeDtypeStruct((m, D_MODEL), F32),
                   jax.ShapeDtypeStruct((n_seq, n_ff, CONV_K - 1, tf), F32)),
        grid=(m // tm, n_ff),
        in_specs=[
            pl.BlockSpec((tm, D_MODEL), lambda i, j: (i, 0)),
            pl.BlockSpec((1, D_MODEL), lambda i, j: (0, 0)),
            pl.BlockSpec((1, D_MODEL, tf), lambda i, j: (j, 0, 0)),
            pl.BlockSpec((1, D_MODEL, tf), lambda i, j: (j + n_ff, 0, 0)),
            pl.BlockSpec((CONV_K, tf), lambda i, j: (0, j)),
            pl.BlockSpec((n_sub, SUBLANES, tf), lambda i, j: (i // tiles_per_seq, 0, j)),
            pl.BlockSpec((tf, D_MODEL), lambda i, j: (j, 0)),
            pl.BlockSpec((1, D_MODEL), lambda i, j: (0, 0)),
        ],
        out_specs=(pl.BlockSpec((tm, D_MODEL), lambda i, j: (i, 0)),
                   pl.BlockSpec((n_seq, n_ff, CONV_K - 1, tf), lambda i, j: (0, 0, 0, 0))),
        scratch_shapes=[pltpu.VMEM((tm, D_MODEL), BF16),
                        pltpu.VMEM((n_ff, SUBLANES, tf), F32),
                        pltpu.VMEM((SUBLANES + sub_len, tf), F32),
                        pltpu.VMEM((tm, tf), BF16)],
        compiler_params=_params(("arbitrary", "arbitrary"), vmem_bytes),
        name="conv_ffn",
    )(x2d, g_ffn.reshape(1, D_MODEL), w_gate_up_bf16, w_gate_up_bf16, w_ffn_conv, _state_rows(state),
      w_down_bf16, g_final.reshape(1, D_MODEL))
    return y, new_ffn.transpose(0, 2, 1, 3).reshape(n_seq, CONV_K - 1, D_FF)


def _tiles(batch, seq_len):
    rows = batch * seq_len
    if rows >= 4096:
        return dict(tm_proj=1024, tm_mix=256, tm_ffn=512, tq=KV_CHUNK)
    return dict(tm_proj=rows, tm_mix=rows, tm_ffn=rows, tq=LANES)


def _trunk(x, cache_k, cache_v, state_conv, state_ffn, weights):
    g_mix, w_in, w_conv, g_att, g_conv, w_o, g_ffn, w_gate_up, w_ffn_conv, w_down, g_final = weights
    b, t, _ = x.shape
    past = cache_k.shape[1]
    tiles = _tiles(b, t)
    tq = tiles["tq"]
    x2d = x.reshape(b * t, D_MODEL)

    proj = _in_proj(x2d, g_mix, w_in, tm=tiles["tm_proj"])
    proj3 = proj.reshape(b, t, IN_WIDTH)
    tq_pad = -(-t // tq) * tq
    tk_pad = -(-(past + tq_pad) // KV_CHUNK) * KV_CHUNK
    if past == 0 and tq_pad == t and tk_pad == t:
        q_h, k_h, v_h, k, v, k_norms = _qkv_layouts(proj3)
    else:
        k = proj3[:, :, ATT_WIDTH:2 * ATT_WIDTH]
        v = proj3[:, :, 2 * ATT_WIDTH:3 * ATT_WIDTH]

        def key_heads(cache, new, **kw):
            tail = jnp.pad(new, ((0, 0), (0, tk_pad - past - t), (0, 0)))
            return _to_heads(cache, tail, permute=True, **kw)

        q_pad = jnp.pad(proj3[:, :, :ATT_WIDTH], ((0, 0), (0, tq_pad - t), (0, 0)))
        q_h = _to_heads(jnp.zeros((b, 0, N_HEADS, HEAD_DIM), F32), q_pad, permute=False, transpose=True)
        k_h, k_norms = key_heads(cache_k, k, transpose=False, key_norms=True)
        v_h = key_heads(cache_v, v, transpose=True)
    att = _attention(q_h, k_h, v_h, k_norms, g_att, past=past, tq=tq)
    att2d = att[:, :t, :].reshape(b * t, ATT_WIDTH)

    x1, new_conv = _mix(att2d, proj, state_conv, w_conv, g_conv, w_o, x2d, tm=tiles["tm_mix"], seq_len=t)
    y, new_ffn = _ffn(x1, g_ffn, w_gate_up, w_ffn_conv, state_ffn, w_down, g_final, tm=tiles["tm_ffn"], seq_len=t)
    shape4 = (b, t, N_HEADS, HEAD_DIM)
    return y.reshape(b, t, D_MODEL), k.reshape(shape4), v.reshape(shape4), new_conv, new_ffn


def kernel(x_prompt, x_sample, cache_k, cache_v, state_conv, state_ffn_conv, g_mix, w_in, w_conv, g_att_out,
           g_conv_out, w_o, g_ffn, w_gate_up, w_ffn_conv, w_down, g_final):
    depth = g_mix.shape[0]
    assert depth == 1
    weights = (g_mix[0], w_in[0].astype(BF16), w_conv[0], g_att_out[0], g_conv_out[0], w_o[0].astype(BF16),
               g_ffn[0], _tile_columns(w_gate_up[0].astype(BF16), FF_TILE), w_ffn_conv[0], w_down[0].astype(BF16),
               g_final)
    bp = x_prompt.shape[0]
    dt = x_prompt.dtype
    empty = jnp.zeros((bp, 0, N_HEADS, HEAD_DIM), dt)
    y_p, k_p, v_p, c_p, f_p = _trunk(
        x_prompt, empty, empty, jnp.zeros((bp, CONV_K - 1, CONV_WIDTH), dt), jnp.zeros((bp, CONV_K - 1, D_FF), dt),
        weights)
    y_s, k_s, v_s, c_s, f_s = _trunk(x_sample, cache_k[0], cache_v[0], state_conv[0], state_ffn_conv[0], weights)
    return (y_p, y_s, k_p[None], v_p[None], c_p[None], f_p[None], k_s[None], v_s[None], c_s[None], f_s[None])
```
